```python
import math
import jax, jax.numpy as jnp
from jax import lax
import numpy as np

D_MODEL = 4096
BATCH = 2
SEQ = 8192
DEPTH = 1

N_MEM = 256
BLK = 128
EPS = 1e-6
NEG = -1e30
SB_HEADS = 16
SB_DH = 128
SB_W = SB_HEADS * SB_DH
MLA_HEADS = 16
MLA_Q_RANK = 1024
MLA_KV_RANK = 512
MLA_NOPE = 128
MLA_ROPE = 64
MLA_DV = 128
ROPE_BASE = 10000.0
MEM_HEADS = 4
MEM_DH = 512
MEM_W = MEM_HEADS * MEM_DH
N_BRANCH = 3
BRANCH_W = 2048
COLS_SB = 3 * SB_W
COLS_DQ = MLA_Q_RANK
COLS_DKV = MLA_KV_RANK + MLA_ROPE
COLS_MEMQ = MEM_W
COLS_GATE = N_BRANCH * D_MODEL
IN_COLS = COLS_SB + COLS_DQ + COLS_DKV + COLS_MEMQ + COLS_GATE
PEER_HEADS = 8
PEER_NKEYS = 128
PEER_N = PEER_NKEYS * PEER_NKEYS
PEER_DK = 256
PEER_TOPK = 16
PEER_CHUNK = 128

kernel_name = "hybrid_sb_mla_mem_peer_layer"


def rmsnorm(x, g):
    xf = x.astype(jnp.float32)
    y = xf * lax.rsqrt(jnp.mean(xf * xf, axis=-1, keepdims=True) + EPS)
    return (y * g.astype(jnp.float32)).astype(x.dtype)


def apply_rope(t, cos, sin):
    half = MLA_ROPE // 2
    t1, t2 = t[..., :half], t[..., half:]
    out = jnp.concatenate([t1 * cos - t2 * sin, t1 * sin + t2 * cos], axis=-1)
    return out.astype(t.dtype)


def to_blocks(t):
    b, h, s, d = t.shape
    return t.reshape(b, h, s // BLK, BLK, d).transpose(2, 0, 1, 3, 4)


def from_blocks(o):
    n, b, h, l, d = o.shape
    return o.transpose(1, 0, 3, 2, 4).reshape(b, n * l, h * d)


def stick_breaking_attention(q, k, v):
    s_len, d = k.shape[2], k.shape[3]
    scale = 1.0 / math.sqrt(d)
    kpos = jnp.arange(s_len)

    def block(args):
        qb, bi = args
        qpos = bi * BLK + jnp.arange(BLK)
        z = jnp.einsum('bhqd,bhkd->bhqk', qb, k).astype(jnp.float32) * scale
        strict = kpos[None, :] < qpos[:, None]
        log_keep = jnp.where(strict, jax.nn.log_sigmoid(-z), 0.0)
        after = jnp.sum(log_keep, axis=-1, keepdims=True) - jnp.cumsum(log_keep, axis=-1)
        w = jnp.where(strict, jnp.exp(jax.nn.log_sigmoid(z) + after), 0.0)
        return jnp.einsum('bhqk,bhkd->bhqd', w.astype(v.dtype), v)

    o = lax.map(block, (to_blocks(q), jnp.arange(s_len // BLK)))
    return from_blocks(o)


def mla_attention(q_nope, q_rope, k_nope, k_rope, v):
    s_len = k_nope.shape[2]
    scale = 1.0 / math.sqrt(MLA_NOPE + MLA_ROPE)
    kpos = jnp.arange(s_len)

    def block(args):
        qn, qr, bi = args
        qpos = bi * BLK + jnp.arange(BLK)
        s = (jnp.einsum('bhqd,bhkd->bhqk', qn, k_nope)
             + jnp.einsum('bhqr,bkr->bhqk', qr, k_rope)).astype(jnp.float32) * scale
        causal = kpos[None, :] <= qpos[:, None]
        p = jax.nn.softmax(jnp.where(causal, s, NEG), axis=-1)
        return jnp.einsum('bhqk,bhkd->bhqd', p.astype(v.dtype), v)

    o = lax.map(block, (to_blocks(q_nope), to_blocks(q_rope), jnp.arange(s_len // BLK)))
    return from_blocks(o)


def memory_attention(mq, mem_n, w_mem_kv):
    b, m, _ = mem_n.shape
    mkv = (mem_n @ w_mem_kv).reshape(b, m, 2, MEM_HEADS, MEM_DH)
    mk, mv = mkv[:, :, 0], mkv[:, :, 1]
    s = jnp.einsum('bshd,bmhd->bhsm', mq, mk).astype(jnp.float32) / math.sqrt(MEM_DH)
    p = jax.nn.softmax(s, axis=-1)
    o = jnp.einsum('bhsm,bmhd->bshd', p.astype(mv.dtype), mv)
    return o.reshape(mq.shape[0], mq.shape[1], MEM_W)


def peer_ffn(xn, w_q, sub_keys, u_tab, v_tab):
    b, s, d = xn.shape
    t = b * s

    def chunk(xc):
        c = xc.shape[0]
        q = (xc @ w_q).reshape(c, PEER_HEADS, 2, PEER_DK // 2)
        sc = jnp.einsum('thpd,hpnd->thpn', q, sub_keys)
        vals, idx = lax.top_k(sc, PEER_TOPK)
        cand = (vals[:, :, 0, :, None] + vals[:, :, 1, None, :]).reshape(c, PEER_HEADS, PEER_TOPK * PEER_TOPK)
        cidx = (idx[:, :, 0, :, None] * PEER_NKEYS + idx[:, :, 1, None, :]).reshape(c, PEER_HEADS, PEER_TOPK * PEER_TOPK)
        top, pos = lax.top_k(cand, PEER_TOPK)
        expert = jnp.take_along_axis(cidx, pos, axis=-1)
        g = jax.nn.softmax(top.astype(jnp.float32), axis=-1).astype(xc.dtype)
        u = jnp.take(u_tab, expert, axis=0)
        vv = jnp.take(v_tab, expert, axis=0)
        a = jax.nn.gelu(jnp.einsum('td,thkd->thk', xc, u), approximate=False) * g
        return jnp.einsum('thk,thkd->td', a, vv)

    y = lax.map(chunk, xn.reshape(t // PEER_CHUNK, PEER_CHUNK, d))
    return y.reshape(b, s, d)


def hybrid_layer(h, mem, cos, sin, g_mix, w_in, mla_g_q, mla_w_uq, mla_g_kv, mla_w_ukv,
                 g_mem, w_mem_kv, w_branch, w_out, g_ffn, peer_w_q, peer_sub_keys, peer_u, peer_v):
    b, s, d = h.shape
    xn = rmsnorm(h, g_mix)
    proj = xn @ w_in
    c1 = COLS_SB
    c2 = c1 + COLS_DQ
    c3 = c2 + COLS_DKV
    c4 = c3 + COLS_MEMQ
    sb_qkv, dq, dkv, mem_q, gate_logits = jnp.split(proj, [c1, c2, c3, c4], axis=-1)

    qkv = sb_qkv.reshape(b, s, 3, SB_HEADS, SB_DH).transpose(2, 0, 3, 1, 4)
    y_sb = stick_breaking_attention(qkv[0], qkv[1], qkv[2])

    c_q = rmsnorm(dq, mla_g_q)
    q = (c_q @ mla_w_uq).reshape(b, s, MLA_HEADS, MLA_NOPE + MLA_ROPE)
    q_nope = q[..., :MLA_NOPE]
    q_rope = apply_rope(q[..., MLA_NOPE:], cos[:, :, None, :], sin[:, :, None, :])
    c_kv = rmsnorm(dkv[..., :MLA_KV_RANK], mla_g_kv)
    k_rope = apply_rope(dkv[..., MLA_KV_RANK:], cos, sin)
    kv = (c_kv @ mla_w_ukv).reshape(b, s, MLA_HEADS, MLA_NOPE + MLA_DV)
    k_nope, v = kv[..., :MLA_NOPE], kv[..., MLA_NOPE:]
    y_mla = mla_attention(q_nope.transpose(0, 2, 1, 3), q_rope.transpose(0, 2, 1, 3),
                          k_nope.transpose(0, 2, 1, 3), k_rope, v.transpose(0, 2, 1, 3))

    y_mem = memory_attention(mem_q.reshape(b, s, MEM_HEADS, MEM_DH), rmsnorm(mem, g_mem), w_mem_kv)

    gates = jax.nn.sigmoid(gate_logits.astype(jnp.float32)).astype(h.dtype).reshape(b, s, N_BRANCH, d)
    merged = (gates[:, :, 0] * (y_sb @ w_branch[0])
              + gates[:, :, 1] * (y_mla @ w_branch[1])
              + gates[:, :, 2] * (y_mem @ w_branch[2]))
    h = h + merged @ w_out

    h = h + peer_ffn(rmsnorm(h, g_ffn), peer_w_q, peer_sub_keys, peer_u, peer_v)
    return h


def setup_inputs(seed: int = 0) -> dict:
    key = jax.random.key(seed)
    ks = jax.random.split(key, 24)
    f32 = jnp.float32

    def nrm(k, shape, scale):
        return jax.random.normal(k, shape, f32) * scale

    def gain(k, n):
        return 1.0 + 0.02 * jax.random.normal(k, (DEPTH, n), f32)

    x = jax.random.normal(ks[0], (BATCH, SEQ, D_MODEL), f32)
    mem = jax.random.normal(ks[1], (BATCH, N_MEM, D_MODEL), f32)
    offsets = jax.random.randint(ks[2], (BATCH, 1), 0, 1024, dtype=jnp.int32)
    positions = offsets + jnp.arange(SEQ, dtype=jnp.int32)[None, :]
    return {
        "x": x,
        "mem": mem,
        "positions": positions,
        "g_mix": gain(ks[3], D_MODEL),
        "w_in": nrm(ks[4], (DEPTH, D_MODEL, IN_COLS), D_MODEL ** -0.5),
        "mla_g_q": gain(ks[5], MLA_Q_RANK),
        "mla_w_uq": nrm(ks[6], (DEPTH, MLA_Q_RANK, MLA_HEADS * (MLA_NOPE + MLA_ROPE)), MLA_Q_RANK ** -0.5),
        "mla_g_kv": gain(ks[7], MLA_KV_RANK),
        "mla_w_ukv": nrm(ks[8], (DEPTH, MLA_KV_RANK, MLA_HEADS * (MLA_NOPE + MLA_DV)), MLA_KV_RANK ** -0.5),
        "g_mem": gain(ks[9], D_MODEL),
        "w_mem_kv": nrm(ks[10], (DEPTH, D_MODEL, 2 * MEM_W), D_MODEL ** -0.5),
        "w_branch": nrm(ks[11], (DEPTH, N_BRANCH, BRANCH_W, D_MODEL), BRANCH_W ** -0.5),
        "w_out": nrm(ks[12], (DEPTH, D_MODEL, D_MODEL), D_MODEL ** -0.5),
        "g_ffn": gain(ks[13], D_MODEL),
        "peer_w_q": nrm(ks[14], (DEPTH, D_MODEL, PEER_HEADS * PEER_DK), D_MODEL ** -0.5),
        "peer_sub_keys": nrm(ks[15], (DEPTH, PEER_HEADS, 2, PEER_NKEYS, PEER_DK // 2), (PEER_DK // 2) ** -0.5),
        "peer_u": nrm(ks[16], (DEPTH, PEER_N, D_MODEL), D_MODEL ** -0.5),
        "peer_v": nrm(ks[17], (DEPTH, PEER_N, D_MODEL), 0.5),
        "g_final": 1.0 + 0.02 * jax.random.normal(ks[18], (D_MODEL,), f32),
    }


def reference(x, mem, positions, g_mix, w_in, mla_g_q, mla_w_uq, mla_g_kv, mla_w_ukv,
              g_mem, w_mem_kv, w_branch, w_out, g_ffn, peer_w_q, peer_sub_keys, peer_u, peer_v, g_final):
    half = MLA_ROPE // 2
    freqs = ROPE_BASE ** (-jnp.arange(half, dtype=jnp.float32) / half)
    ang = positions.astype(jnp.float32)[..., None] * freqs
    cos, sin = jnp.cos(ang), jnp.sin(ang)
    h = x
    for layer in range(DEPTH):
        h = hybrid_layer(h, mem, cos, sin, g_mix[layer], w_in[layer], mla_g_q[layer], mla_w_uq[layer],
                         mla_g_kv[layer], mla_w_ukv[layer], g_mem[layer], w_mem_kv[layer],
                         w_branch[layer], w_out[layer], g_ffn[layer], peer_w_q[layer],
                         peer_sub_keys[layer], peer_u[layer], peer_v[layer])
    return rmsnorm(h, g_final)
```

```python
import functools
import math

import jax
import jax.numpy as jnp
from jax import lax
from jax.experimental import pallas as pl
from jax.experimental.pallas import tpu as pltpu

F32 = jnp.float32
BF16 = jnp.bfloat16

EPS = 1e-6
NEG = -1e30
LANES = 128

SB_HEADS = 16
SB_DH = 128
SB_W = SB_HEADS * SB_DH
MLA_HEADS = 16
MLA_Q_RANK = 1024
MLA_KV_RANK = 512
MLA_NOPE = 128
MLA_ROPE = 64
MLA_DV = 128
MLA_QK_PAD = 256
ROPE_BASE = 10000.0
MEM_HEADS = 4
MEM_DH = 512
MEM_W = MEM_HEADS * MEM_DH
N_BRANCH = 3
BRANCH_W = 2048
PEER_HEADS = 8
PEER_NKEYS = 128
PEER_DK = 256
PEER_TOPK = 16

COL_SB = 0
COL_DQ = 3 * SB_W
COL_DKV = COL_DQ + MLA_Q_RANK
DKV_PAD = 1024
COL_MEMQ = COL_DKV + DKV_PAD
COL_GATE = COL_MEMQ + MEM_W

VMEM_LIMIT_BYTES = 56 * 1024 * 1024


def _cparams(*sem):
    return pltpu.CompilerParams(dimension_semantics=sem, vmem_limit_bytes=VMEM_LIMIT_BYTES)


def _nt_dot(a, b):
    return lax.dot_general(a, b, (((1,), (1,)), ((), ())), preferred_element_type=F32)


def _rmsnorm_kernel(x_ref, g_ref, o_ref):
    x = x_ref[...].astype(F32)
    ms = jnp.mean(x * x, axis=-1, keepdims=True)
    o_ref[...] = (x * lax.rsqrt(ms + EPS) * g_ref[...]).astype(o_ref.dtype)


def _rmsnorm(x, g, out_dtype, tm=256):
    m, d = x.shape
    tm = min(tm, m)
    return pl.pallas_call(
        _rmsnorm_kernel,
        grid=(m // tm,),
        in_specs=[pl.BlockSpec((tm, d), lambda i: (i, 0)), pl.BlockSpec((1, d), lambda i: (0, 0))],
        out_specs=pl.BlockSpec((tm, d), lambda i: (i, 0)),
        out_shape=jax.ShapeDtypeStruct((m, d), out_dtype),
        compiler_params=_cparams("parallel"),
        name="rmsnorm",
    )(x, g.reshape(1, d).astype(F32))


def _add_rmsnorm_kernel(a_ref, b_ref, g_ref, o_ref):
    x = a_ref[...] + b_ref[...]
    ms = jnp.mean(x * x, axis=-1, keepdims=True)
    o_ref[...] = (x * lax.rsqrt(ms + EPS) * g_ref[...]).astype(o_ref.dtype)


def _add_rmsnorm(a, b, g, tm=256):
    m, d = a.shape
    tm = min(tm, m)
    return pl.pallas_call(
        _add_rmsnorm_kernel,
        grid=(m // tm,),
        in_specs=[pl.BlockSpec((tm, d), lambda i: (i, 0)), pl.BlockSpec((tm, d), lambda i: (i, 0)),
                  pl.BlockSpec((1, d), lambda i: (0, 0))],
        out_specs=pl.BlockSpec((tm, d), lambda i: (i, 0)),
        out_shape=jax.ShapeDtypeStruct((m, d), F32),
        compiler_params=_cparams("parallel"),
        name="add_rmsnorm",
    )(a, b, g.reshape(1, d).astype(F32))


def _mm_kernel(a_ref, b_ref, o_ref):
    o_ref[...] = jnp.dot(a_ref[...], b_ref[...], preferred_element_type=F32).astype(o_ref.dtype)


def _mm_res_kernel(a_ref, b_ref, r_ref, o_ref):
    acc = jnp.dot(a_ref[...], b_ref[...], preferred_element_type=F32)
    o_ref[...] = (acc + r_ref[...]).astype(o_ref.dtype)


def _matmul(a, b, out_dtype, tm, tn, residual=None, name="matmul"):
    m, k = a.shape
    _, n = b.shape
    tm, tn = min(tm, m), min(tn, n)
    in_specs = [pl.BlockSpec((tm, k), lambda j, i: (i, 0)), pl.BlockSpec((k, tn), lambda j, i: (0, j))]
    args = [a, b]
    body = _mm_kernel
    if residual is not None:
        in_specs.append(pl.BlockSpec((tm, tn), lambda j, i: (i, j)))
        args.append(residual)
        body = _mm_res_kernel
    return pl.pallas_call(
        body,
        grid=(n // tn, m // tm),
        in_specs=in_specs,
        out_specs=pl.BlockSpec((tm, tn), lambda j, i: (i, j)),
        out_shape=jax.ShapeDtypeStruct((m, n), out_dtype),
        compiler_params=_cparams("parallel", "parallel"),
        name=name,
    )(*args)


SB_BQ = 256
SB_BK = 128


def _sb_kernel(q_ref, k_ref, v_ref, o_ref, *, scale):
    qi = pl.program_id(2)
    bq = q_ref.shape[1]
    q = (q_ref[0].astype(F32) * scale).astype(BF16)
    r = lax.broadcasted_iota(jnp.int32, (SB_BK, 2 * SB_BK), 0)
    c = lax.broadcasted_iota(jnp.int32, (SB_BK, 2 * SB_BK), 1)
    u_ext = jnp.where((c >= SB_BK) | (r > c), 1.0, 0.0).astype(BF16)
    qpos = qi * bq + lax.broadcasted_iota(jnp.int32, (bq, SB_BK), 0)
    kofs = lax.broadcasted_iota(jnp.int32, (bq, SB_BK), 1)

    def block(kb, carry, masked):
        run, acc = carry
        start = pl.multiple_of(kb * SB_BK, SB_BK)
        kblk = k_ref[0, pl.ds(start, SB_BK), :]
        vblk = v_ref[0, pl.ds(start, SB_BK), :]
        z = _nt_dot(q, kblk)
        lk = -(jnp.maximum(z, 0.0) + jnp.log(1.0 + jnp.exp(-jnp.abs(z))))
        if masked:
            strict = (kb * SB_BK + kofs) < qpos
            lkm = jnp.where(strict, lk, 0.0)
        else:
            lkm = lk
        hi = lkm.astype(BF16)
        lo = (lkm - hi.astype(F32)).astype(BF16)
        cs = (jnp.dot(hi, u_ext, preferred_element_type=F32)
              + jnp.dot(lo, u_ext, preferred_element_type=F32))
        suf = cs[:, :SB_BK]
        tot = cs[:, SB_BK:]
        w = jnp.exp(z + lk + run + suf)
        if masked:
            w = jnp.where(strict, w, 0.0)
        acc = acc + jnp.dot(w.astype(BF16), vblk, preferred_element_type=F32)
        return run + tot, acc

    carry = (jnp.zeros((bq, SB_BK), F32), jnp.zeros((bq, SB_DH), F32))
    nd = bq // SB_BK
    for d in range(nd):
        carry = block(qi * nd + (nd - 1 - d), carry, True)
    carry = lax.fori_loop(0, qi * nd, lambda t, cr: block(qi * nd - 1 - t, cr, False), carry)
    o_ref[0] = carry[1].astype(o_ref.dtype)


def _sb_attention(proj3):
    b, s, _ = proj3.shape
    bq = min(SB_BQ, s)
    nh = SB_HEADS
    return pl.pallas_call(
        functools.partial(_sb_kernel, scale=1.0 / math.sqrt(SB_DH)),
        grid=(b, nh, s // bq),
        in_specs=[
            pl.BlockSpec((1, bq, SB_DH), lambda bi, h, qi: (bi, qi, h)),
            pl.BlockSpec((1, s, SB_DH), lambda bi, h, qi: (bi, 0, nh + h)),
            pl.BlockSpec((1, s, SB_DH), lambda bi, h, qi: (bi, 0, 2 * nh + h)),
        ],
        out_specs=pl.BlockSpec((1, bq, SB_DH), lambda bi, h, qi: (bi, qi, h)),
        out_shape=jax.ShapeDtypeStruct((b, s, SB_W), BF16),
        compiler_params=_cparams("parallel", "parallel", "arbitrary"),
        name="sb_attention",
    )(proj3, proj3, proj3)


def _mla_prep_kernel(dq_ref, dkv_ref, c_ref, s_ref, gq_ref, gkv_ref, wab_ref, wkv_ref,
                     qcat_ref, kcat_ref, v_ref, *, scale):
    cmap = c_ref[...]
    smap = s_ref[...]
    dq = dq_ref[...].astype(F32)
    cq = dq * lax.rsqrt(jnp.mean(dq * dq, axis=-1, keepdims=True) + EPS) * gq_ref[...]
    qab = jnp.dot(cq.astype(BF16), wab_ref[...], preferred_element_type=F32)
    swap0 = MLA_HEADS * MLA_QK_PAD
    for h in range(MLA_HEADS):
        lo = h * MLA_QK_PAD
        qcat_ref[:, lo:lo + LANES] = (qab[:, lo:lo + LANES] * scale).astype(BF16)
        rope = qab[:, lo + LANES:lo + 2 * LANES] * cmap + qab[:, swap0 + h * LANES:swap0 + (h + 1) * LANES] * smap
        qcat_ref[:, lo + LANES:lo + 2 * LANES] = (rope * scale).astype(BF16)
    dkv = dkv_ref[...].astype(F32)
    ckv_in = dkv[:, :MLA_KV_RANK]
    ckv = ckv_in * lax.rsqrt(jnp.mean(ckv_in * ckv_in, axis=-1, keepdims=True) + EPS) * gkv_ref[...]
    kv = jnp.dot(ckv.astype(BF16), wkv_ref[...], preferred_element_type=F32)
    kr = (dkv[:, MLA_KV_RANK:MLA_KV_RANK + LANES] * cmap
          + dkv[:, MLA_KV_RANK + LANES:MLA_KV_RANK + 2 * LANES] * smap).astype(BF16)
    for h in range(MLA_HEADS):
        lo = h * MLA_QK_PAD
        kcat_ref[:, lo:lo + LANES] = kv[:, h * LANES:(h + 1) * LANES].astype(BF16)
        kcat_ref[:, lo + LANES:lo + 2 * LANES] = kr
    v_ref[...] = kv[:, MLA_HEADS * MLA_NOPE:].astype(BF16)


def _mla_prep(proj, cmap, smap, gq, gkv, wab, wkv, tt=256):
    t = proj.shape[0]
    tt = min(tt, t)
    wq = MLA_HEADS * MLA_QK_PAD
    wv = MLA_HEADS * MLA_DV
    return pl.pallas_call(
        functools.partial(_mla_prep_kernel, scale=1.0 / math.sqrt(MLA_NOPE + MLA_ROPE)),
        grid=(t // tt,),
        in_specs=[
            pl.BlockSpec((tt, MLA_Q_RANK), lambda i: (i, COL_DQ // MLA_Q_RANK)),
            pl.BlockSpec((tt, DKV_PAD), lambda i: (i, COL_DKV // DKV_PAD)),
            pl.BlockSpec((tt, LANES), lambda i: (i, 0)),
            pl.BlockSpec((tt, LANES), lambda i: (i, 0)),
            pl.BlockSpec((1, MLA_Q_RANK), lambda i: (0, 0)),
            pl.BlockSpec((1, MLA_KV_RANK), lambda i: (0, 0)),
            pl.BlockSpec(wab.shape, lambda i: (0, 0)),
            pl.BlockSpec(wkv.shape, lambda i: (0, 0)),
        ],
        out_specs=[
            pl.BlockSpec((tt, wq), lambda i: (i, 0)),
            pl.BlockSpec((tt, wq), lambda i: (i, 0)),
            pl.BlockSpec((tt, wv), lambda i: (i, 0)),
        ],
        out_shape=[
            jax.ShapeDtypeStruct((t, wq), BF16),
            jax.ShapeDtypeStruct((t, wq), BF16),
            jax.ShapeDtypeStruct((t, wv), BF16),
        ],
        compiler_params=_cparams("parallel"),
        name="mla_prep",
    )(proj, proj, cmap, smap, gq.reshape(1, -1).astype(F32), gkv.reshape(1, -1).astype(F32), wab, wkv)


MLA_BQ = 256


def _mla_kernel(q_ref, k_ref, v_ref, o_ref):
    qi = pl.program_id(2)
    bq = q_ref.shape[1]
    q = q_ref[0]
    rows = lax.broadcasted_iota(jnp.int32, (bq, bq), 0)
    cols = lax.broadcasted_iota(jnp.int32, (bq, bq), 1)

    def block(kb, carry, masked):
        m, l, acc = carry
        start = pl.multiple_of(kb * bq, bq)
        kblk = k_ref[0, pl.ds(start, bq), :]
        vblk = v_ref[0, pl.ds(start, bq), :]
        s = _nt_dot(q, kblk)
        if masked:
            s = jnp.where(cols <= rows, s, NEG)
        m_new = jnp.maximum(m, jnp.max(s, axis=-1, keepdims=True))
        alpha = jnp.exp(m - m_new)
        p = jnp.exp(s - m_new)
        l = alpha * l + jnp.sum(p, axis=-1, keepdims=True)
        acc = alpha * acc + jnp.dot(p.astype(BF16), vblk, preferred_element_type=F32)
        return m_new, l, acc

    carry = (jnp.full((bq, 1), NEG, F32), jnp.zeros((bq, 1), F32), jnp.zeros((bq, MLA_DV), F32))
    carry = lax.fori_loop(0, qi, lambda kb, cr: block(kb, cr, False), carry)
    _, l, acc = block(qi, carry, True)
    o_ref[0] = (acc / l).astype(o_ref.dtype)


def _mla_attention(qcat3, kcat3, v3):
    b, s, _ = qcat3.shape
    bq = min(MLA_BQ, s)
    return pl.pallas_call(
        _mla_kernel,
        grid=(b, MLA_HEADS, s // bq),
        in_specs=[
            pl.BlockSpec((1, bq, MLA_QK_PAD), lambda bi, h, qi: (bi, qi, h)),
            pl.BlockSpec((1, s, MLA_QK_PAD), lambda bi, h, qi: (bi, 0, h)),
            pl.BlockSpec((1, s, MLA_DV), lambda bi, h, qi: (bi, 0, h)),
        ],
        out_specs=pl.BlockSpec((1, bq, MLA_DV), lambda bi, h, qi: (bi, qi, h)),
        out_shape=jax.ShapeDtypeStruct((b, s, MLA_HEADS * MLA_DV), BF16),
        compiler_params=_cparams("parallel", "parallel", "arbitrary"),
        name="mla_attention",
    )(qcat3, kcat3, v3)


def _mem_kernel(q_ref, k_ref, v_ref, o_ref, *, scale):
    s = _nt_dot(q_ref[0], k_ref[0]) * scale
    m = jnp.max(s, axis=-1, keepdims=True)
    p = jnp.exp(s - m)
    p = p / jnp.sum(p, axis=-1, keepdims=True)
    o_ref[0] = jnp.dot(p.astype(BF16), v_ref[0], preferred_element_type=F32).astype(o_ref.dtype)


def _mem_attention(proj3, mkv3, tt=512):
    b, s, _ = proj3.shape
    n_mem = mkv3.shape[1]
    tt = min(tt, s)
    q0 = COL_MEMQ // MEM_DH
    return pl.pallas_call(
        functools.partial(_mem_kernel, scale=1.0 / math.sqrt(MEM_DH)),
        grid=(b, s // tt, MEM_HEADS),
        in_specs=[
            pl.BlockSpec((1, tt, MEM_DH), lambda bi, ti, h: (bi, ti, q0 + h)),
            pl.BlockSpec((1, n_mem, MEM_DH), lambda bi, ti, h: (bi, 0, h)),
            pl.BlockSpec((1, n_mem, MEM_DH), lambda bi, ti, h: (bi, 0, MEM_HEADS + h)),
        ],
        out_specs=pl.BlockSpec((1, tt, MEM_DH), lambda bi, ti, h: (bi, ti, h)),
        out_shape=jax.ShapeDtypeStruct((b, s, MEM_W), BF16),
        compiler_params=_cparams("parallel", "parallel", "parallel"),
        name="mem_attention",
    )(proj3, mkv3, mkv3)


def _merge_kernel(ya_ref, yb_ref, yc_ref, ga_ref, gb_ref, gc_ref, w_ref, o_ref):
    acc = jax.nn.sigmoid(ga_ref[...].astype(F32)) * jnp.dot(ya_ref[...], w_ref[0], preferred_element_type=F32)
    acc += jax.nn.sigmoid(gb_ref[...].astype(F32)) * jnp.dot(yb_ref[...], w_ref[1], preferred_element_type=F32)
    acc += jax.nn.sigmoid(gc_ref[...].astype(F32)) * jnp.dot(yc_ref[...], w_ref[2], preferred_element_type=F32)
    o_ref[...] = acc.astype(o_ref.dtype)


def _merge(y_sb, y_mla, y_mem, proj, wb, d, tm=1024, tn=512):
    t = y_sb.shape[0]
    tm, tn = min(tm, t), min(tn, d)
    g0 = COL_GATE // tn
    gstep = d // tn

    def gate_spec(br):
        return pl.BlockSpec((tm, tn), lambda j, i: (i, g0 + br * gstep + j))

    yspec = pl.BlockSpec((tm, BRANCH_W), lambda j, i: (i, 0))
    return pl.pallas_call(
        _merge_kernel,
        grid=(d // tn, t // tm),
        in_specs=[yspec, yspec, yspec, gate_spec(0), gate_spec(1), gate_spec(2),
                  pl.BlockSpec((N_BRANCH, BRANCH_W, tn), lambda j, i: (0, 0, j))],
        out_specs=pl.BlockSpec((tm, tn), lambda j, i: (i, j)),
        out_shape=jax.ShapeDtypeStruct((t, d), BF16),
        compiler_params=_cparams("parallel", "parallel"),
        name="branch_merge",
    )(y_sb, y_mla, y_mem, proj, proj, proj, wb)


def _take_top(vals, n):
    out = []
    rem = vals
    for _ in range(n):
        mx = jnp.max(rem, axis=0, keepdims=True)
        out.append(mx)
        rem = jnp.where(rem == mx, -jnp.inf, rem)
    return out


def _peer_route_kernel(q_ref, keys_ref, s0_ref, s1_ref, e1_ref, w0_ref, tau_ref):
    def head(h, carry):
        s0 = _nt_dot(keys_ref[h, 0], q_ref[2 * h].astype(BF16))
        s1 = _nt_dot(keys_ref[h, 1], q_ref[2 * h + 1].astype(BF16))
        top0 = _take_top(s0, PEER_TOPK)
        top1 = _take_top(s1, PEER_TOPK)
        v1 = jnp.concatenate(top1, axis=0)
        cand = jnp.concatenate([t0 + v1 for t0 in top0], axis=0)
        ctop = _take_top(cand, PEER_TOPK)
        cmax = ctop[0]
        zsum = jnp.zeros_like(cmax)
        for cv in ctop:
            zsum = zsum + jnp.exp(cv - cmax)
        s0_ref[h] = s0
        s1_ref[h] = s1
        w0_ref[h] = jnp.exp(s0 - top0[0]) / zsum
        e1_ref[h] = jnp.exp(s1 - top1[0])
        tau_ref[h] = ctop[PEER_TOPK - 1]
        return carry

    lax.fori_loop(0, PEER_HEADS, head, 0)


def _peer_route(q3, keys, tt=256):
    _, t, dh = q3.shape
    tt = min(tt, t)
    big = jax.ShapeDtypeStruct((PEER_HEADS, PEER_NKEYS, t), F32)
    bigspec = pl.BlockSpec((PEER_HEADS, PEER_NKEYS, tt), lambda i: (0, 0, i))
    return pl.pallas_call(
        _peer_route_kernel,
        grid=(t // tt,),
        in_specs=[pl.BlockSpec((2 * PEER_HEADS, tt, dh), lambda i: (0, i, 0)),
                  pl.BlockSpec(keys.shape, lambda i: (0, 0, 0, 0))],
        out_specs=[bigspec, bigspec, bigspec, bigspec,
                   pl.BlockSpec((PEER_HEADS, 1, tt), lambda i: (0, 0, i))],
        out_shape=[big, big, big, big, jax.ShapeDtypeStruct((PEER_HEADS, 1, t), F32)],
        compiler_params=_cparams("parallel"),
        name="peer_route",
    )(q3, keys)


PEER_EB = 512
PEER_GROUP = PEER_EB // PEER_NKEYS


def _peer_dense_kernel(xn_ref, u_ref, v_ref, sw_ref, s1_ref, e1_ref, tau_ref, o_ref):
    e = pl.program_id(1)

    @pl.when(e == 0)
    def _():
        o_ref[...] = jnp.zeros_like(o_ref)

    pre = _nt_dot(u_ref[...], xn_ref[...])
    act = 0.5 * pre * (1.0 + lax.erf(pre * (1.0 / math.sqrt(2.0))))
    parts = []
    for ii in range(PEER_GROUP):
        gate = None
        for h in range(PEER_HEADS):
            s0row = sw_ref[0, h, ii:ii + 1, :]
            w0row = sw_ref[0, h, PEER_GROUP + ii:PEER_GROUP + ii + 1, :]
            csum = s0row + s1_ref[h]
            g = jnp.where(csum >= tau_ref[h], e1_ref[h] * w0row, 0.0)
            gate = g if gate is None else gate + g
        parts.append((act[ii * PEER_NKEYS:(ii + 1) * PEER_NKEYS] * gate).astype(BF16))
    a_t = jnp.concatenate(parts, axis=0)
    o_ref[...] += lax.dot_general(a_t, v_ref[...], (((0,), (0,)), ((), ())), preferred_element_type=F32)


def _peer_dense(xn, u, v, sw, s1, e1, tau, tt=512):
    t, d = xn.shape
    n = u.shape[0]
    tt = min(tt, t)
    hspec = pl.BlockSpec((PEER_HEADS, PEER_NKEYS, tt), lambda ti, e: (0, 0, ti))
    return pl.pallas_call(
        _peer_dense_kernel,
        grid=(t // tt, n // PEER_EB),
        in_specs=[
            pl.BlockSpec((tt, d), lambda ti, e: (ti, 0)),
            pl.BlockSpec((PEER_EB, d), lambda ti, e: (e, 0)),
            pl.BlockSpec((PEER_EB, d), lambda ti, e: (e, 0)),
            pl.BlockSpec((1, PEER_HEADS, 2 * PEER_GROUP, tt), lambda ti, e: (e, 0, 0, ti)),
            hspec,
            hspec,
            pl.BlockSpec((PEER_HEADS, 1, tt), lambda ti, e: (0, 0, ti)),
        ],
        out_specs=pl.BlockSpec((tt, d), lambda ti, e: (ti, 0)),
        out_shape=jax.ShapeDtypeStruct((t, d), F32),
        compiler_params=_cparams("parallel", "arbitrary"),
        name="peer_dense",
    )(xn, u, v, sw, s1, e1, tau)


def _pack_w_in(w_in, d):
    c1 = 3 * SB_W
    c2 = c1 + MLA_Q_RANK
    c3 = c2 + MLA_KV_RANK + MLA_ROPE
    c4 = c3 + MEM_W
    half = MLA_ROPE // 2
    kr = w_in[:, c2 + MLA_KV_RANK:c3]
    z = lambda n: jnp.zeros((d, n), w_in.dtype)
    dkv = jnp.concatenate([
        w_in[:, c2:c2 + MLA_KV_RANK],
        kr, z(LANES - MLA_ROPE),
        kr[:, half:], kr[:, :half], z(LANES - MLA_ROPE),
        z(DKV_PAD - MLA_KV_RANK - 2 * LANES)], axis=1)
    return jnp.concatenate([w_in[:, :c2], dkv, w_in[:, c3:c4], w_in[:, c4:]], axis=1).astype(BF16)


def _pack_mla_weights(w_uq, w_ukv):
    half = MLA_ROPE // 2
    rq = w_uq.shape[0]
    wq = w_uq.reshape(rq, MLA_HEADS, MLA_NOPE + MLA_ROPE)
    nope, t1, t2 = wq[:, :, :MLA_NOPE], wq[:, :, MLA_NOPE:MLA_NOPE + half], wq[:, :, MLA_NOPE + half:]
    zq = jnp.zeros((rq, MLA_HEADS, LANES - MLA_ROPE), w_uq.dtype)
    w_a = jnp.concatenate([nope, t1, t2, zq], axis=2).reshape(rq, MLA_HEADS * MLA_QK_PAD)
    w_b = jnp.concatenate([t2, t1, zq], axis=2).reshape(rq, MLA_HEADS * LANES)
    wab = jnp.concatenate([w_a, w_b], axis=1).astype(BF16)
    rkv = w_ukv.shape[0]
    wkv = w_ukv.reshape(rkv, MLA_HEADS, MLA_NOPE + MLA_DV)
    wkv = jnp.concatenate([wkv[:, :, :MLA_NOPE].reshape(rkv, -1), wkv[:, :, MLA_NOPE:].reshape(rkv, -1)], axis=1)
    return wab, wkv.astype(BF16)


def _rope_maps(positions):
    half = MLA_ROPE // 2
    freqs = ROPE_BASE ** (-jnp.arange(half, dtype=F32) / half)
    ang = positions.astype(F32).reshape(-1)[:, None] * freqs
    cos, sin = jnp.cos(ang), jnp.sin(ang)
    z = jnp.zeros((ang.shape[0], LANES - MLA_ROPE), F32)
    return jnp.concatenate([cos, cos, z], axis=1), jnp.concatenate([-sin, sin, z], axis=1)


def _layer(h, mem, cmap, smap, g_mix, w_in, mla_g_q, mla_w_uq, mla_g_kv, mla_w_ukv, g_mem, w_mem_kv,
           w_branch, w_out, g_ffn, peer_w_q, peer_sub_keys, peer_u, peer_v):
    b, s, d = h.shape
    t = b * s
    n_mem = mem.shape[1]
    h2 = h.reshape(t, d)

    xn = _rmsnorm(h2, g_mix, BF16)
    proj = _matmul(xn, _pack_w_in(w_in, d), BF16, tm=512, tn=1024, name="in_proj")
    proj3 = proj.reshape(b, s, -1)

    y_sb = _sb_attention(proj3).reshape(t, SB_W)

    wab, wkv = _pack_mla_weights(mla_w_uq, mla_w_ukv)
    qcat, kcat, v = _mla_prep(proj, cmap, smap, mla_g_q, mla_g_kv, wab, wkv)
    y_mla = _mla_attention(qcat.reshape(b, s, -1), kcat.reshape(b, s, -1), v.reshape(b, s, -1)).reshape(t, -1)

    mem_n = _rmsnorm(mem.reshape(b * n_mem, d), g_mem, BF16)
    mkv = _matmul(mem_n, w_mem_kv.astype(BF16), BF16, tm=512, tn=1024, name="mem_kv")
    y_mem = _mem_attention(proj3, mkv.reshape(b, n_mem, 2 * MEM_W)).reshape(t, MEM_W)

    merged = _merge(y_sb, y_mla, y_mem, proj, w_branch.astype(BF16), d)
    h1 = _matmul(merged, w_out.astype(BF16), F32, tm=512, tn=1024, residual=h2, name="out_proj")

    xn2 = _rmsnorm(h1, g_ffn, BF16)
    q = _matmul(xn2, peer_w_q.astype(BF16), F32, tm=512, tn=1024, name="peer_q")
    dh = PEER_DK // 2
    q3 = q.reshape(t, 2 * PEER_HEADS, dh).transpose(1, 0, 2)
    s0, s1, e1, w0, tau = _peer_route(q3, peer_sub_keys.astype(BF16))
    ngrp = PEER_NKEYS // PEER_GROUP
    sw = jnp.concatenate([s0.reshape(PEER_HEADS, ngrp, PEER_GROUP, t), w0.reshape(PEER_HEADS, ngrp, PEER_GROUP, t)],
                         axis=2).transpose(1, 0, 2, 3)
    y = _peer_dense(xn2, peer_u.astype(BF16), peer_v.astype(BF16), sw, s1, e1, tau)
    return h1, y


def kernel(x, mem, positions, g_mix, w_in, mla_g_q, mla_w_uq, mla_g_kv, mla_w_ukv, g_mem, w_mem_kv, w_branch,
           w_out, g_ffn, peer_w_q, peer_sub_keys, peer_u, peer_v, g_final):
    b, s, d = x.shape
    cmap, smap = _rope_maps(positions)
    depth = w_in.shape[0]
    h = x
    for layer in range(depth):
        h1, y = _layer(h, mem, cmap, smap, g_mix[layer], w_in[layer], mla_g_q[layer], mla_w_uq[layer],
                       mla_g_kv[layer], mla_w_ukv[layer], g_mem[layer], w_mem_kv[layer], w_branch[layer],
                       w_out[layer], g_ffn[layer], peer_w_q[layer], peer_sub_keys[layer], peer_u[layer],
                       peer_v[layer])
        if layer + 1 < depth:
            h = (h1 + y).reshape(b, s, d)
    return _add_rmsnorm(h1, y, g_final).reshape(b, s, d)
```

```python
import functools
import math

import jax
import jax.numpy as jnp
from jax import lax
from jax.experimental import pallas as pl
from jax.experimental.pallas import tpu as pltpu

F32 = jnp.float32
BF16 = jnp.bfloat16

EPS = 1e-6
NEG = -1e30
LANES = 128

SB_HEADS = 16
SB_DH = 128
SB_W = SB_HEADS * SB_DH
MLA_HEADS = 16
MLA_Q_RANK = 1024
MLA_KV_RANK = 512
MLA_NOPE = 128
MLA_ROPE = 64
MLA_DV = 128
MLA_QK_PAD = 256
ROPE_BASE = 10000.0
MEM_HEADS = 4
MEM_DH = 512
MEM_W = MEM_HEADS * MEM_DH
N_BRANCH = 3
BRANCH_W = 2048
PEER_HEADS = 8
PEER_NKEYS = 128
PEER_DK = 256
PEER_TOPK = 16

COL_SB = 0
COL_DQ = 3 * SB_W
COL_DKV = COL_DQ + MLA_Q_RANK
DKV_PAD = 1024
COL_MEMQ = COL_DKV + DKV_PAD
COL_GATE = COL_MEMQ + MEM_W

VMEM_LIMIT_BYTES = 56 * 1024 * 1024


def _cparams(*sem):
    return pltpu.CompilerParams(dimension_semantics=sem, vmem_limit_bytes=VMEM_LIMIT_BYTES)


def _nt_dot(a, b):
    return lax.dot_general(a, b, (((1,), (1,)), ((), ())), preferred_element_type=F32)


def _rmsnorm_kernel(x_ref, g_ref, o_ref):
    x = x_ref[...].astype(F32)
    ms = jnp.mean(x * x, axis=-1, keepdims=True)
    o_ref[...] = (x * lax.rsqrt(ms + EPS) * g_ref[...]).astype(o_ref.dtype)


def _rmsnorm(x, g, out_dtype, tm=256):
    m, d = x.shape
    tm = min(tm, m)
    return pl.pallas_call(
        _rmsnorm_kernel,
        grid=(m // tm,),
        in_specs=[pl.BlockSpec((tm, d), lambda i: (i, 0)), pl.BlockSpec((1, d), lambda i: (0, 0))],
        out_specs=pl.BlockSpec((tm, d), lambda i: (i, 0)),
        out_shape=jax.ShapeDtypeStruct((m, d), out_dtype),
        compiler_params=_cparams("parallel"),
        name="rmsnorm",
    )(x, g.reshape(1, d).astype(F32))


def _add_rmsnorm_kernel(a_ref, b_ref, g_ref, o_ref):
    x = a_ref[...] + b_ref[...]
    ms = jnp.mean(x * x, axis=-1, keepdims=True)
    o_ref[...] = (x * lax.rsqrt(ms + EPS) * g_ref[...]).astype(o_ref.dtype)


def _add_rmsnorm(a, b, g, tm=256):
    m, d = a.shape
    tm = min(tm, m)
    return pl.pallas_call(
        _add_rmsnorm_kernel,
        grid=(m // tm,),
        in_specs=[pl.BlockSpec((tm, d), lambda i: (i, 0)), pl.BlockSpec((tm, d), lambda i: (i, 0)),
                  pl.BlockSpec((1, d), lambda i: (0, 0))],
        out_specs=pl.BlockSpec((tm, d), lambda i: (i, 0)),
        out_shape=jax.ShapeDtypeStruct((m, d), F32),
        compiler_params=_cparams("parallel"),
        name="add_rmsnorm",
    )(a, b, g.reshape(1, d).astype(F32))


def _mm_kernel(a_ref, b_ref, o_ref):
    o_ref[...] = jnp.dot(a_ref[...], b_ref[...], preferred_element_type=F32).astype(o_ref.dtype)


def _mm_res_kernel(a_ref, b_ref, r_ref, o_ref):
    acc = jnp.dot(a_ref[...], b_ref[...], preferred_element_type=F32)
    o_ref[...] = (acc + r_ref[...]).astype(o_ref.dtype)


def _matmul(a, b, out_dtype, tm, tn, residual=None, name="matmul"):
    m, k = a.shape
    _, n = b.shape
    tm, tn = min(tm, m), min(tn, n)
    in_specs = [pl.BlockSpec((tm, k), lambda j, i: (i, 0)), pl.BlockSpec((k, tn), lambda j, i: (0, j))]
    args = [a, b]
    body = _mm_kernel
    if residual is not None:
        in_specs.append(pl.BlockSpec((tm, tn), lambda j, i: (i, j)))
        args.append(residual)
        body = _mm_res_kernel
    return pl.pallas_call(
        body,
        grid=(n // tn, m // tm),
        in_specs=in_specs,
        out_specs=pl.BlockSpec((tm, tn), lambda j, i: (i, j)),
        out_shape=jax.ShapeDtypeStruct((m, n), out_dtype),
        compiler_params=_cparams("parallel", "parallel"),
        name=name,
    )(*args)


SB_BQ = 256
SB_BK = 128
SB_HG = 4
SB_STOP = -105.0


def _sb_kernel(q_ref, k_ref, v_ref, o_ref, *, scale):
    qi = pl.program_id(2)
    bq = q_ref.shape[1]
    qs = [(q_ref[0, :, h * SB_DH:(h + 1) * SB_DH].astype(F32) * scale).astype(BF16) for h in range(SB_HG)]
    r = lax.broadcasted_iota(jnp.int32, (SB_BK, 2 * SB_BK), 0)
    c = lax.broadcasted_iota(jnp.int32, (SB_BK, 2 * SB_BK), 1)
    u_ext = jnp.where((c >= SB_BK) | (r > c), 1.0, 0.0).astype(BF16)
    qpos = qi * bq + lax.broadcasted_iota(jnp.int32, (bq, SB_BK), 0)
    kofs = lax.broadcasted_iota(jnp.int32, (bq, SB_BK), 1)

    def block(kb, runs, accs, masked):
        start = pl.multiple_of(kb * SB_BK, SB_BK)
        if masked:
            strict = (kb * SB_BK + kofs) < qpos
        heads = range(SB_HG)
        zs = [_nt_dot(qs[h], k_ref[0, pl.ds(start, SB_BK), h * SB_DH:(h + 1) * SB_DH]) for h in heads]
        lks = [-(jnp.maximum(z, 0.0) + jnp.log(1.0 + jnp.exp(-jnp.abs(z)))) for z in zs]
        lkms = [jnp.where(strict, lk, 0.0) for lk in lks] if masked else lks
        his = [lkm.astype(BF16) for lkm in lkms]
        los = [(lkm - hi.astype(F32)).astype(BF16) for lkm, hi in zip(lkms, his)]
        css = [jnp.dot(hi, u_ext, preferred_element_type=F32) + jnp.dot(lo, u_ext, preferred_element_type=F32)
               for hi, lo in zip(his, los)]
        ws = [jnp.exp(zs[h] + lks[h] + runs[h] + css[h][:, :SB_BK]) for h in heads]
        if masked:
            ws = [jnp.where(strict, w, 0.0) for w in ws]
        new_accs = [accs[h] + jnp.dot(ws[h].astype(BF16), v_ref[0, pl.ds(start, SB_BK), h * SB_DH:(h + 1) * SB_DH],
                                      preferred_element_type=F32) for h in heads]
        new_runs = [runs[h] + css[h][:, SB_BK:] for h in heads]
        return tuple(new_runs), tuple(new_accs)

    def least_decayed(runs):
        mx = jnp.max(runs[0])
        for h in range(1, SB_HG):
            mx = jnp.maximum(mx, jnp.max(runs[h]))
        return mx

    runs = tuple(jnp.zeros((bq, SB_BK), F32) for _ in range(SB_HG))
    accs = tuple(jnp.zeros((bq, SB_DH), F32) for _ in range(SB_HG))
    nd = bq // SB_BK
    for d in range(nd):
        runs, accs = block(qi * nd + (nd - 1 - d), runs, accs, True)

    def cond(state):
        kb, mx, _, _ = state
        return jnp.logical_and(kb >= 0, mx > SB_STOP)

    def body(state):
        kb, _, rs, ac = state
        rs, ac = block(kb, rs, ac, False)
        return kb - 1, least_decayed(rs), rs, ac

    _, _, _, accs = lax.while_loop(cond, body, (qi * nd - 1, least_decayed(runs), runs, accs))
    for h in range(SB_HG):
        o_ref[0, :, h * SB_DH:(h + 1) * SB_DH] = accs[h].astype(o_ref.dtype)


def _sb_attention(proj3):
    b, s, _ = proj3.shape
    bq = min(SB_BQ, s)
    ng = SB_HEADS // SB_HG
    wg = SB_HG * SB_DH
    return pl.pallas_call(
        functools.partial(_sb_kernel, scale=1.0 / math.sqrt(SB_DH)),
        grid=(b, ng, s // bq),
        in_specs=[
            pl.BlockSpec((1, bq, wg), lambda bi, g, qi: (bi, qi, g)),
            pl.BlockSpec((1, s, wg), lambda bi, g, qi: (bi, 0, ng + g), pipeline_mode=pl.Buffered(1)),
            pl.BlockSpec((1, s, wg), lambda bi, g, qi: (bi, 0, 2 * ng + g), pipeline_mode=pl.Buffered(1)),
        ],
        out_specs=pl.BlockSpec((1, bq, wg), lambda bi, g, qi: (bi, qi, g)),
        out_shape=jax.ShapeDtypeStruct((b, s, SB_W), BF16),
        compiler_params=_cparams("parallel", "parallel", "arbitrary"),
        name="sb_attention",
    )(proj3, proj3, proj3)


def _mla_prep_kernel(dq_ref, dkv_ref, c_ref, s_ref, gq_ref, gkv_ref, wab_ref, wkv_ref,
                     qcat_ref, kcat_ref, v_ref, *, scale):
    cmap = c_ref[...]
    smap = s_ref[...]
    dq = dq_ref[...].astype(F32)
    cq = dq * lax.rsqrt(jnp.mean(dq * dq, axis=-1, keepdims=True) + EPS) * gq_ref[...]
    qab = jnp.dot(cq.astype(BF16), wab_ref[...], preferred_element_type=F32)
    swap0 = MLA_HEADS * MLA_QK_PAD
    for h in range(MLA_HEADS):
        lo = h * MLA_QK_PAD
        qcat_ref[:, lo:lo + LANES] = (qab[:, lo:lo + LANES] * scale).astype(BF16)
        rope = qab[:, lo + LANES:lo + 2 * LANES] * cmap + qab[:, swap0 + h * LANES:swap0 + (h + 1) * LANES] * smap
        qcat_ref[:, lo + LANES:lo + 2 * LANES] = (rope * scale).astype(BF16)
    dkv = dkv_ref[...].astype(F32)
    ckv_in = dkv[:, :MLA_KV_RANK]
    ckv = ckv_in * lax.rsqrt(jnp.mean(ckv_in * ckv_in, axis=-1, keepdims=True) + EPS) * gkv_ref[...]
    kv = jnp.dot(ckv.astype(BF16), wkv_ref[...], preferred_element_type=F32)
    kr = (dkv[:, MLA_KV_RANK:MLA_KV_RANK + LANES] * cmap
          + dkv[:, MLA_KV_RANK + LANES:MLA_KV_RANK + 2 * LANES] * smap).astype(BF16)
    for h in range(MLA_HEADS):
        lo = h * MLA_QK_PAD
        kcat_ref[:, lo:lo + LANES] = kv[:, h * LANES:(h + 1) * LANES].astype(BF16)
        kcat_ref[:, lo + LANES:lo + 2 * LANES] = kr
    v_ref[...] = kv[:, MLA_HEADS * MLA_NOPE:].astype(BF16)


def _mla_prep(proj, cmap, smap, gq, gkv, wab, wkv, tt=256):
    t = proj.shape[0]
    tt = min(tt, t)
    wq = MLA_HEADS * MLA_QK_PAD
    wv = MLA_HEADS * MLA_DV
    return pl.pallas_call(
        functools.partial(_mla_prep_kernel, scale=1.0 / math.sqrt(MLA_NOPE + MLA_ROPE)),
        grid=(t // tt,),
        in_specs=[
            pl.BlockSpec((tt, MLA_Q_RANK), lambda i: (i, COL_DQ // MLA_Q_RANK)),
            pl.BlockSpec((tt, DKV_PAD), lambda i: (i, COL_DKV // DKV_PAD)),
            pl.BlockSpec((tt, LANES), lambda i: (i, 0)),
            pl.BlockSpec((tt, LANES), lambda i: (i, 0)),
            pl.BlockSpec((1, MLA_Q_RANK), lambda i: (0, 0)),
            pl.BlockSpec((1, MLA_KV_RANK), lambda i: (0, 0)),
            pl.BlockSpec(wab.shape, lambda i: (0, 0)),
            pl.BlockSpec(wkv.shape, lambda i: (0, 0)),
        ],
        out_specs=[
            pl.BlockSpec((tt, wq), lambda i: (i, 0)),
            pl.BlockSpec((tt, wq), lambda i: (i, 0)),
            pl.BlockSpec((tt, wv), lambda i: (i, 0)),
        ],
        out_shape=[
            jax.ShapeDtypeStruct((t, wq), BF16),
            jax.ShapeDtypeStruct((t, wq), BF16),
            jax.ShapeDtypeStruct((t, wv), BF16),
        ],
        compiler_params=_cparams("parallel"),
        name="mla_prep",
    )(proj, proj, cmap, smap, gq.reshape(1, -1).astype(F32), gkv.reshape(1, -1).astype(F32), wab, wkv)


MLA_BQ = 512
MLA_BK = 256
MLA_HG = 2


def _mla_kernel(q_ref, k_ref, v_ref, o_ref):
    qi = pl.program_id(2)
    bq = q_ref.shape[1]
    bk = min(MLA_BK, bq)
    qs = [q_ref[0, :, h * MLA_QK_PAD:(h + 1) * MLA_QK_PAD] for h in range(MLA_HG)]
    rows = qi * bq + lax.broadcasted_iota(jnp.int32, (bq, bk), 0)
    cols = lax.broadcasted_iota(jnp.int32, (bq, bk), 1)
    ones = jnp.ones((bk, MLA_DV), BF16)

    def block(kb, ms, accs, masked):
        start = pl.multiple_of(kb * bk, bk)
        if masked:
            allowed = (kb * bk + cols) <= rows
        heads = range(MLA_HG)
        ss = [_nt_dot(qs[h], k_ref[0, pl.ds(start, bk), h * MLA_QK_PAD:(h + 1) * MLA_QK_PAD]) for h in heads]
        if masked:
            ss = [jnp.where(allowed, s, NEG) for s in ss]
        new_ms = [jnp.maximum(ms[h], jnp.max(ss[h], axis=-1, keepdims=True)) for h in heads]
        ps = [jnp.exp(ss[h] - new_ms[h]).astype(BF16) for h in heads]
        pvs = [jnp.dot(ps[h], jnp.concatenate([v_ref[0, pl.ds(start, bk), h * MLA_DV:(h + 1) * MLA_DV], ones], axis=1),
                       preferred_element_type=F32) for h in heads]
        new_accs = [jnp.exp(ms[h] - new_ms[h]) * accs[h] + pvs[h] for h in heads]
        return tuple(new_ms), tuple(new_accs)

    ms = tuple(jnp.full((bq, 1), NEG, F32) for _ in range(MLA_HG))
    accs = tuple(jnp.zeros((bq, 2 * MLA_DV), F32) for _ in range(MLA_HG))
    nd = bq // bk
    ms, accs = lax.fori_loop(0, qi * nd, lambda kb, cr: block(kb, cr[0], cr[1], False), (ms, accs))
    for d in range(nd):
        ms, accs = block(qi * nd + d, ms, accs, True)
    for h in range(MLA_HG):
        o_ref[0, :, h * MLA_DV:(h + 1) * MLA_DV] = (accs[h][:, :MLA_DV] / accs[h][:, MLA_DV:]).astype(o_ref.dtype)


def _mla_attention(qcat3, kcat3, v3):
    b, s, _ = qcat3.shape
    bq = min(MLA_BQ, s)
    wq = MLA_HG * MLA_QK_PAD
    wv = MLA_HG * MLA_DV
    return pl.pallas_call(
        _mla_kernel,
        grid=(b, MLA_HEADS // MLA_HG, s // bq),
        in_specs=[
            pl.BlockSpec((1, bq, wq), lambda bi, g, qi: (bi, qi, g)),
            pl.BlockSpec((1, s, wq), lambda bi, g, qi: (bi, 0, g), pipeline_mode=pl.Buffered(1)),
            pl.BlockSpec((1, s, wv), lambda bi, g, qi: (bi, 0, g), pipeline_mode=pl.Buffered(1)),
        ],
        out_specs=pl.BlockSpec((1, bq, wv), lambda bi, g, qi: (bi, qi, g)),
        out_shape=jax.ShapeDtypeStruct((b, s, MLA_HEADS * MLA_DV), BF16),
        compiler_params=_cparams("parallel", "parallel", "arbitrary"),
        name="mla_attention",
    )(qcat3, kcat3, v3)


def _mem_kernel(q_ref, k_ref, v_ref, o_ref, *, scale):
    s = _nt_dot(q_ref[0], k_ref[0]) * scale
    m = jnp.max(s, axis=-1, keepdims=True)
    p = jnp.exp(s - m)
    p = p / jnp.sum(p, axis=-1, keepdims=True)
    o_ref[0] = jnp.dot(p.astype(BF16), v_ref[0], preferred_element_type=F32).astype(o_ref.dtype)


def _mem_attention(proj3, mkv3, tt=512):
    b, s, _ = proj3.shape
    n_mem = mkv3.shape[1]
    tt = min(tt, s)
    q0 = COL_MEMQ // MEM_DH
    return pl.pallas_call(
        functools.partial(_mem_kernel, scale=1.0 / math.sqrt(MEM_DH)),
        grid=(b, s // tt, MEM_HEADS),
        in_specs=[
            pl.BlockSpec((1, tt, MEM_DH), lambda bi, ti, h: (bi, ti, q0 + h)),
            pl.BlockSpec((1, n_mem, MEM_DH), lambda bi, ti, h: (bi, 0, h)),
            pl.BlockSpec((1, n_mem, MEM_DH), lambda bi, ti, h: (bi, 0, MEM_HEADS + h)),
        ],
        out_specs=pl.BlockSpec((1, tt, MEM_DH), lambda bi, ti, h: (bi, ti, h)),
        out_shape=jax.ShapeDtypeStruct((b, s, MEM_W), BF16),
        compiler_params=_cparams("parallel", "parallel", "parallel"),
        name="mem_attention",
    )(proj3, mkv3, mkv3)


def _merge_kernel(ya_ref, yb_ref, yc_ref, ga_ref, gb_ref, gc_ref, w_ref, o_ref):
    acc = jax.nn.sigmoid(ga_ref[...].astype(F32)) * jnp.dot(ya_ref[...], w_ref[0], preferred_element_type=F32)
    acc += jax.nn.sigmoid(gb_ref[...].astype(F32)) * jnp.dot(yb_ref[...], w_ref[1], preferred_element_type=F32)
    acc += jax.nn.sigmoid(gc_ref[...].astype(F32)) * jnp.dot(yc_ref[...], w_ref[2], preferred_element_type=F32)
    o_ref[...] = acc.astype(o_ref.dtype)


def _merge(y_sb, y_mla, y_mem, proj, wb, d, tm=1024, tn=512):
    t = y_sb.shape[0]
    tm, tn = min(tm, t), min(tn, d)
    g0 = COL_GATE // tn
    gstep = d // tn

    def gate_spec(br):
        return pl.BlockSpec((tm, tn), lambda j, i: (i, g0 + br * gstep + j))

    yspec = pl.BlockSpec((tm, BRANCH_W), lambda j, i: (i, 0))
    return pl.pallas_call(
        _merge_kernel,
        grid=(d // tn, t // tm),
        in_specs=[yspec, yspec, yspec, gate_spec(0), gate_spec(1), gate_spec(2),
                  pl.BlockSpec((N_BRANCH, BRANCH_W, tn), lambda j, i: (0, 0, j))],
        out_specs=pl.BlockSpec((tm, tn), lambda j, i: (i, j)),
        out_shape=jax.ShapeDtypeStruct((t, d), BF16),
        compiler_params=_cparams("parallel", "parallel"),
        name="branch_merge",
    )(y_sb, y_mla, y_mem, proj, proj, proj, wb)


def _take_top(vals, n):
    out = []
    rem = vals
    for _ in range(n):
        mx = jnp.max(rem, axis=0, keepdims=True)
        out.append(mx)
        rem = jnp.where(rem == mx, -jnp.inf, rem)
    return out


def _peer_route_kernel(q_ref, keys_ref, s0_ref, s1_ref, e1_ref, w0_ref, tau_ref):
    def head(h, carry):
        s0 = _nt_dot(keys_ref[h, 0], q_ref[2 * h].astype(BF16))
        s1 = _nt_dot(keys_ref[h, 1], q_ref[2 * h + 1].astype(BF16))
        top0 = _take_top(s0, PEER_TOPK)
        top1 = _take_top(s1, PEER_TOPK)
        v1 = jnp.concatenate(top1, axis=0)
        cand = jnp.concatenate([t0 + v1 for t0 in top0], axis=0)
        ctop = _take_top(cand, PEER_TOPK)
        cmax = ctop[0]
        zsum = jnp.zeros_like(cmax)
        for cv in ctop:
            zsum = zsum + jnp.exp(cv - cmax)
        s0_ref[h] = s0
        s1_ref[h] = s1
        w0_ref[h] = jnp.exp(s0 - top0[0]) / zsum
        e1_ref[h] = jnp.exp(s1 - top1[0])
        tau_ref[h] = ctop[PEER_TOPK - 1]
        return carry

    lax.fori_loop(0, PEER_HEADS, head, 0)


def _peer_route(q3, keys, tt=256):
    _, t, dh = q3.shape
    tt = min(tt, t)
    big = jax.ShapeDtypeStruct((PEER_HEADS, PEER_NKEYS, t), F32)
    bigspec = pl.BlockSpec((PEER_HEADS, PEER_NKEYS, tt), lambda i: (0, 0, i))
    return pl.pallas_call(
        _peer_route_kernel,
        grid=(t // tt,),
        in_specs=[pl.BlockSpec((2 * PEER_HEADS, tt, dh), lambda i: (0, i, 0)),
                  pl.BlockSpec(keys.shape, lambda i: (0, 0, 0, 0))],
        out_specs=[bigspec, bigspec, bigspec, bigspec,
                   pl.BlockSpec((PEER_HEADS, 1, tt), lambda i: (0, 0, i))],
        out_shape=[big, big, big, big, jax.ShapeDtypeStruct((PEER_HEADS, 1, t), F32)],
        compiler_params=_cparams("parallel"),
        name="peer_route",
    )(q3, keys)


PEER_EB = 512
PEER_GROUP = PEER_EB // PEER_NKEYS
PEER_ROWS = 64


def _peer_dense_kernel(xn_ref, u_ref, v_ref, sw_ref, s1_ref, e1_ref, tau_ref, o_ref, pre_ref, at_ref):
    e = pl.program_id(1)

    @pl.when(e == 0)
    def _():
        o_ref[...] = jnp.zeros_like(o_ref)
        pre_ref[...] = jnp.zeros_like(pre_ref)

    tt = pre_ref.shape[1]
    for ii in range(PEER_GROUP):
        for l0 in range(0, tt, LANES):
            for r0 in range(0, PEER_NKEYS, PEER_ROWS):
                gate = None
                for h in range(PEER_HEADS):
                    s0row = sw_ref[0, h, ii:ii + 1, l0:l0 + LANES]
                    w0row = sw_ref[0, h, PEER_GROUP + ii:PEER_GROUP + ii + 1, l0:l0 + LANES]
                    csum = s0row + s1_ref[h, r0:r0 + PEER_ROWS, l0:l0 + LANES]
                    g = jnp.where(csum >= tau_ref[h, :, l0:l0 + LANES],
                                  e1_ref[h, r0:r0 + PEER_ROWS, l0:l0 + LANES] * w0row, 0.0)
                    gate = g if gate is None else gate + g
                e0 = ii * PEER_NKEYS + r0
                pre = pre_ref[e0:e0 + PEER_ROWS, l0:l0 + LANES]
                act = 0.5 * pre * (1.0 + lax.erf(pre * (1.0 / math.sqrt(2.0))))
                at_ref[e0:e0 + PEER_ROWS, l0:l0 + LANES] = (act * gate).astype(BF16)
    next_pre = _nt_dot(u_ref[...], xn_ref[...])
    o_ref[...] += lax.dot_general(at_ref[...], v_ref[...], (((0,), (0,)), ((), ())), preferred_element_type=F32)
    pre_ref[...] = next_pre


def _peer_dense(xn, u, v, sw, s1, e1, tau, tt=512):
    t, d = xn.shape
    n = u.shape[0]
    tt = min(tt, t)
    ne = n // PEER_EB
    last = ne - 1
    hspec = pl.BlockSpec((PEER_HEADS, PEER_NKEYS, tt), lambda ti, e: (0, 0, ti))
    return pl.pallas_call(
        _peer_dense_kernel,
        grid=(t // tt, ne + 1),
        in_specs=[
            pl.BlockSpec((tt, d), lambda ti, e: (ti, 0)),
            pl.BlockSpec((PEER_EB, d), lambda ti, e: (jnp.minimum(e, last), 0)),
            pl.BlockSpec((PEER_EB, d), lambda ti, e: (jnp.maximum(e - 1, 0), 0)),
            pl.BlockSpec((1, PEER_HEADS, 2 * PEER_GROUP, tt), lambda ti, e: (jnp.maximum(e - 1, 0), 0, 0, ti)),
            hspec,
            hspec,
            pl.BlockSpec((PEER_HEADS, 1, tt), lambda ti, e: (0, 0, ti)),
        ],
        out_specs=pl.BlockSpec((tt, d), lambda ti, e: (ti, 0)),
        out_shape=jax.ShapeDtypeStruct((t, d), F32),
        scratch_shapes=[pltpu.VMEM((PEER_EB, tt), F32), pltpu.VMEM((PEER_EB, tt), BF16)],
        compiler_params=_cparams("parallel", "arbitrary"),
        name="peer_dense",
    )(xn, u, v, sw, s1, e1, tau)


def _pack_w_in(w_in, d):
    c1 = 3 * SB_W
    c2 = c1 + MLA_Q_RANK
    c3 = c2 + MLA_KV_RANK + MLA_ROPE
    c4 = c3 + MEM_W
    half = MLA_ROPE // 2
    kr = w_in[:, c2 + MLA_KV_RANK:c3]
    z = lambda n: jnp.zeros((d, n), w_in.dtype)
    dkv = jnp.concatenate([
        w_in[:, c2:c2 + MLA_KV_RANK],
        kr, z(LANES - MLA_ROPE),
        kr[:, half:], kr[:, :half], z(LANES - MLA_ROPE),
        z(DKV_PAD - MLA_KV_RANK - 2 * LANES)], axis=1)
    return jnp.concatenate([w_in[:, :c2], dkv, w_in[:, c3:c4], w_in[:, c4:]], axis=1)


def _pack_mla_weights(w_uq, w_ukv):
    half = MLA_ROPE // 2
    rq = w_uq.shape[0]
    wq = w_uq.reshape(rq, MLA_HEADS, MLA_NOPE + MLA_ROPE)
    nope, t1, t2 = wq[:, :, :MLA_NOPE], wq[:, :, MLA_NOPE:MLA_NOPE + half], wq[:, :, MLA_NOPE + half:]
    zq = jnp.zeros((rq, MLA_HEADS, LANES - MLA_ROPE), w_uq.dtype)
    w_a = jnp.concatenate([nope, t1, t2, zq], axis=2).reshape(rq, MLA_HEADS * MLA_QK_PAD)
    w_b = jnp.concatenate([t2, t1, zq], axis=2).reshape(rq, MLA_HEADS * LANES)
    wab = jnp.concatenate([w_a, w_b], axis=1).astype(BF16)
    rkv = w_ukv.shape[0]
    wkv = w_ukv.reshape(rkv, MLA_HEADS, MLA_NOPE + MLA_DV)
    wkv = jnp.concatenate([wkv[:, :, :MLA_NOPE].reshape(rkv, -1), wkv[:, :, MLA_NOPE:].reshape(rkv, -1)], axis=1)
    return wab, wkv.astype(BF16)


def _rope_maps(positions):
    half = MLA_ROPE // 2
    freqs = ROPE_BASE ** (-jnp.arange(half, dtype=F32) / half)
    ang = positions.astype(F32).reshape(-1)[:, None] * freqs
    cos, sin = jnp.cos(ang), jnp.sin(ang)
    z = jnp.zeros((ang.shape[0], LANES - MLA_ROPE), F32)
    return jnp.concatenate([cos, cos, z], axis=1), jnp.concatenate([-sin, sin, z], axis=1)


def _layer(h, mem, cmap, smap, g_mix, w_in, mla_g_q, mla_w_uq, mla_g_kv, mla_w_ukv, g_mem, w_mem_kv,
           w_branch, w_out, g_ffn, peer_w_q, peer_sub_keys, peer_u, peer_v):
    b, s, d = h.shape
    t = b * s
    n_mem = mem.shape[1]
    h2 = h.reshape(t, d)

    xn = _rmsnorm(h2, g_mix, BF16)
    proj = _matmul(xn, _pack_w_in(w_in.astype(BF16), d), BF16, tm=512, tn=1024, name="in_proj")
    proj3 = proj.reshape(b, s, -1)

    y_sb = _sb_attention(proj3).reshape(t, SB_W)

    wab, wkv = _pack_mla_weights(mla_w_uq, mla_w_ukv)
    qcat, kcat, v = _mla_prep(proj, cmap, smap, mla_g_q, mla_g_kv, wab, wkv)
    y_mla = _mla_attention(qcat.reshape(b, s, -1), kcat.reshape(b, s, -1), v.reshape(b, s, -1)).reshape(t, -1)

    mem_n = _rmsnorm(mem.reshape(b * n_mem, d), g_mem, BF16)
    mkv = _matmul(mem_n, w_mem_kv.astype(BF16), BF16, tm=512, tn=1024, name="mem_kv")
    y_mem = _mem_attention(proj3, mkv.reshape(b, n_mem, 2 * MEM_W)).reshape(t, MEM_W)

    merged = _merge(y_sb, y_mla, y_mem, proj, w_branch.astype(BF16), d)
    h1 = _matmul(merged, w_out.astype(BF16), F32, tm=512, tn=1024, residual=h2, name="out_proj")

    xn2 = _rmsnorm(h1, g_ffn, BF16)
    q = _matmul(xn2, peer_w_q.astype(BF16), F32, tm=512, tn=1024, name="peer_q")
    dh = PEER_DK // 2
    q3 = q.reshape(t, 2 * PEER_HEADS, dh).transpose(1, 0, 2)
    s0, s1, e1, w0, tau = _peer_route(q3, peer_sub_keys.astype(BF16))
    ngrp = PEER_NKEYS // PEER_GROUP
    sw = jnp.concatenate([s0.reshape(PEER_HEADS, ngrp, PEER_GROUP, t), w0.reshape(PEER_HEADS, ngrp, PEER_GROUP, t)],
                         axis=2).transpose(1, 0, 2, 3)
    y = _peer_dense(xn2, peer_u.astype(BF16), peer_v.astype(BF16), sw, s1, e1, tau)
    return h1, y


def kernel(x, mem, positions, g_mix, w_in, mla_g_q, mla_w_uq, mla_g_kv, mla_w_ukv, g_mem, w_mem_kv, w_branch,
           w_out, g_ffn, peer_w_q, peer_sub_keys, peer_u, peer_v, g_final):
    b, s, d = x.shape
    cmap, smap = _rope_maps(positions)
    depth = w_in.shape[0]
    h = x
    for layer in range(depth):
        h1, y = _layer(h, mem, cmap, smap, g_mix[layer], w_in[layer], mla_g_q[layer], mla_w_uq[layer],
                       mla_g_kv[layer], mla_w_ukv[layer], g_mem[layer], w_mem_kv[layer], w_branch[layer],
                       w_out[layer], g_ffn[layer], peer_w_q[layer], peer_sub_keys[layer], peer_u[layer],
                       peer_v[layer])
        if layer + 1 < depth:
            h = (h1 + y).reshape(b, s, d)
    return _add_rmsnorm(h1, y, g_final).reshape(b, s, d)
```

```python
import functools
import math

import jax
import jax.numpy as jnp
from jax import lax
from jax.experimental import pallas as pl
from jax.experimental.pallas import tpu as pltpu

F32 = jnp.float32
BF16 = jnp.bfloat16

EPS = 1e-6
NEG = -1e30
LANES = 128

SB_HEADS = 16
SB_DH = 128
SB_W = SB_HEADS * SB_DH
MLA_HEADS = 16
MLA_Q_RANK = 1024
MLA_KV_RANK = 512
MLA_NOPE = 128
MLA_ROPE = 64
MLA_DV = 128
MLA_QK_PAD = 256
ROPE_BASE = 10000.0
MEM_HEADS = 4
MEM_DH = 512
MEM_W = MEM_HEADS * MEM_DH
N_BRANCH = 3
BRANCH_W = 2048
PEER_HEADS = 8
PEER_NKEYS = 128
PEER_DK = 256
PEER_TOPK = 16

COL_SB = 0
COL_DQ = 3 * SB_W
COL_DKV = COL_DQ + MLA_Q_RANK
DKV_PAD = 1024
COL_MEMQ = COL_DKV + DKV_PAD
COL_GATE = COL_MEMQ + MEM_W

VMEM_LIMIT_BYTES = 56 * 1024 * 1024


def _cparams(*sem):
    return pltpu.CompilerParams(dimension_semantics=sem, vmem_limit_bytes=VMEM_LIMIT_BYTES)


def _nt_dot(a, b):
    return lax.dot_general(a, b, (((1,), (1,)), ((), ())), preferred_element_type=F32)


def _rmsnorm_kernel(x_ref, g_ref, o_ref):
    x = x_ref[...].astype(F32)
    ms = jnp.mean(x * x, axis=-1, keepdims=True)
    o_ref[...] = (x * lax.rsqrt(ms + EPS) * g_ref[...]).astype(o_ref.dtype)


def _rmsnorm(x, g, out_dtype, tm=256):
    m, d = x.shape
    tm = min(tm, m)
    return pl.pallas_call(
        _rmsnorm_kernel,
        grid=(m // tm,),
        in_specs=[pl.BlockSpec((tm, d), lambda i: (i, 0)), pl.BlockSpec((1, d), lambda i: (0, 0))],
        out_specs=pl.BlockSpec((tm, d), lambda i: (i, 0)),
        out_shape=jax.ShapeDtypeStruct((m, d), out_dtype),
        compiler_params=_cparams("parallel"),
        name="rmsnorm",
    )(x, g.reshape(1, d).astype(F32))


def _add_rmsnorm_kernel(a_ref, b_ref, g_ref, o_ref):
    x = a_ref[...] + b_ref[...]
    ms = jnp.mean(x * x, axis=-1, keepdims=True)
    o_ref[...] = (x * lax.rsqrt(ms + EPS) * g_ref[...]).astype(o_ref.dtype)


def _add_rmsnorm(a, b, g, tm=256):
    m, d = a.shape
    tm = min(tm, m)
    return pl.pallas_call(
        _add_rmsnorm_kernel,
        grid=(m // tm,),
        in_specs=[pl.BlockSpec((tm, d), lambda i: (i, 0)), pl.BlockSpec((tm, d), lambda i: (i, 0)),
                  pl.BlockSpec((1, d), lambda i: (0, 0))],
        out_specs=pl.BlockSpec((tm, d), lambda i: (i, 0)),
        out_shape=jax.ShapeDtypeStruct((m, d), F32),
        compiler_params=_cparams("parallel"),
        name="add_rmsnorm",
    )(a, b, g.reshape(1, d).astype(F32))


def _mm_kernel(a_ref, b_ref, o_ref):
    o_ref[...] = jnp.dot(a_ref[...], b_ref[...], preferred_element_type=F32).astype(o_ref.dtype)


def _mm_res_kernel(a_ref, b_ref, r_ref, o_ref):
    acc = jnp.dot(a_ref[...], b_ref[...], preferred_element_type=F32)
    o_ref[...] = (acc + r_ref[...]).astype(o_ref.dtype)


def _matmul(a, b, out_dtype, tm, tn, residual=None, name="matmul"):
    m, k = a.shape
    _, n = b.shape
    tm, tn = min(tm, m), min(tn, n)
    in_specs = [pl.BlockSpec((tm, k), lambda j, i: (i, 0)), pl.BlockSpec((k, tn), lambda j, i: (0, j))]
    args = [a, b]
    body = _mm_kernel
    if residual is not None:
        in_specs.append(pl.BlockSpec((tm, tn), lambda j, i: (i, j)))
        args.append(residual)
        body = _mm_res_kernel
    return pl.pallas_call(
        body,
        grid=(n // tn, m // tm),
        in_specs=in_specs,
        out_specs=pl.BlockSpec((tm, tn), lambda j, i: (i, j)),
        out_shape=jax.ShapeDtypeStruct((m, n), out_dtype),
        compiler_params=_cparams("parallel", "parallel"),
        name=name,
    )(*args)


SB_BQ = 256
SB_BK = 128
SB_HG = 4
SB_STOP = -105.0


def _sb_kernel(q_ref, k_ref, v_ref, o_ref, *, scale):
    qi = pl.program_id(2)
    bq = q_ref.shape[1]
    qs = [(q_ref[0, :, h * SB_DH:(h + 1) * SB_DH].astype(F32) * scale).astype(BF16) for h in range(SB_HG)]
    r = lax.broadcasted_iota(jnp.int32, (SB_BK, 2 * SB_BK), 0)
    c = lax.broadcasted_iota(jnp.int32, (SB_BK, 2 * SB_BK), 1)
    u_ext = jnp.where((c >= SB_BK) | (r > c), 1.0, 0.0).astype(BF16)
    qpos = qi * bq + lax.broadcasted_iota(jnp.int32, (bq, SB_BK), 0)
    kofs = lax.broadcasted_iota(jnp.int32, (bq, SB_BK), 1)

    def block(kb, runs, accs, masked):
        start = pl.multiple_of(kb * SB_BK, SB_BK)
        if masked:
            strict = (kb * SB_BK + kofs) < qpos
        heads = range(SB_HG)
        zs = [_nt_dot(qs[h], k_ref[0, pl.ds(start, SB_BK), h * SB_DH:(h + 1) * SB_DH]) for h in heads]
        lks = [-(jnp.maximum(z, 0.0) + jnp.log(1.0 + jnp.exp(-jnp.abs(z)))) for z in zs]
        lkms = [jnp.where(strict, lk, 0.0) for lk in lks] if masked else lks
        his = [lkm.astype(BF16) for lkm in lkms]
        los = [(lkm - hi.astype(F32)).astype(BF16) for lkm, hi in zip(lkms, his)]
        css = [jnp.dot(hi, u_ext, preferred_element_type=F32) + jnp.dot(lo, u_ext, preferred_element_type=F32)
               for hi, lo in zip(his, los)]
        ws = [jnp.exp(zs[h] + lks[h] + runs[h] + css[h][:, :SB_BK]) for h in heads]
        if masked:
            ws = [jnp.where(strict, w, 0.0) for w in ws]
        new_accs = [accs[h] + jnp.dot(ws[h].astype(BF16), v_ref[0, pl.ds(start, SB_BK), h * SB_DH:(h + 1) * SB_DH],
                                      preferred_element_type=F32) for h in heads]
        new_runs = [runs[h] + css[h][:, SB_BK:] for h in heads]
        return tuple(new_runs), tuple(new_accs)

    def least_decayed(runs):
        mx = jnp.max(runs[0])
        for h in range(1, SB_HG):
            mx = jnp.maximum(mx, jnp.max(runs[h]))
        return mx

    runs = tuple(jnp.zeros((bq, SB_BK), F32) for _ in range(SB_HG))
    accs = tuple(jnp.zeros((bq, SB_DH), F32) for _ in range(SB_HG))
    nd = bq // SB_BK
    for d in range(nd):
        runs, accs = block(qi * nd + (nd - 1 - d), runs, accs, True)

    def cond(state):
        kb, mx, _, _ = state
        return jnp.logical_and(kb >= 0, mx > SB_STOP)

    def body(state):
        kb, _, rs, ac = state
        rs, ac = block(kb, rs, ac, False)
        return kb - 1, least_decayed(rs), rs, ac

    _, _, _, accs = lax.while_loop(cond, body, (qi * nd - 1, least_decayed(runs), runs, accs))
    for h in range(SB_HG):
        o_ref[0, :, h * SB_DH:(h + 1) * SB_DH] = accs[h].astype(o_ref.dtype)


def _sb_attention(proj3):
    b, s, _ = proj3.shape
    bq = min(SB_BQ, s)
    ng = SB_HEADS // SB_HG
    wg = SB_HG * SB_DH
    return pl.pallas_call(
        functools.partial(_sb_kernel, scale=1.0 / math.sqrt(SB_DH)),
        grid=(b, ng, s // bq),
        in_specs=[
            pl.BlockSpec((1, bq, wg), lambda bi, g, qi: (bi, qi, g)),
            pl.BlockSpec((1, s, wg), lambda bi, g, qi: (bi, 0, ng + g), pipeline_mode=pl.Buffered(1)),
            pl.BlockSpec((1, s, wg), lambda bi, g, qi: (bi, 0, 2 * ng + g), pipeline_mode=pl.Buffered(1)),
        ],
        out_specs=pl.BlockSpec((1, bq, wg), lambda bi, g, qi: (bi, qi, g)),
        out_shape=jax.ShapeDtypeStruct((b, s, SB_W), BF16),
        compiler_params=_cparams("parallel", "parallel", "arbitrary"),
        name="sb_attention",
    )(proj3, proj3, proj3)


def _mla_prep_kernel(dq_ref, dkv_ref, c_ref, s_ref, gq_ref, gkv_ref, wab_ref, wkv_ref,
                     qcat_ref, kcat_ref, v_ref, *, scale):
    cmap = c_ref[...]
    smap = s_ref[...]
    dq = dq_ref[...].astype(F32)
    cq = dq * lax.rsqrt(jnp.mean(dq * dq, axis=-1, keepdims=True) + EPS) * gq_ref[...]
    qab = jnp.dot(cq.astype(BF16), wab_ref[...], preferred_element_type=F32)
    swap0 = MLA_HEADS * MLA_QK_PAD
    for h in range(MLA_HEADS):
        lo = h * MLA_QK_PAD
        qcat_ref[:, lo:lo + LANES] = (qab[:, lo:lo + LANES] * scale).astype(BF16)
        rope = qab[:, lo + LANES:lo + 2 * LANES] * cmap + qab[:, swap0 + h * LANES:swap0 + (h + 1) * LANES] * smap
        qcat_ref[:, lo + LANES:lo + 2 * LANES] = (rope * scale).astype(BF16)
    dkv = dkv_ref[...].astype(F32)
    ckv_in = dkv[:, :MLA_KV_RANK]
    ckv = ckv_in * lax.rsqrt(jnp.mean(ckv_in * ckv_in, axis=-1, keepdims=True) + EPS) * gkv_ref[...]
    kv = jnp.dot(ckv.astype(BF16), wkv_ref[...], preferred_element_type=F32)
    kr = (dkv[:, MLA_KV_RANK:MLA_KV_RANK + LANES] * cmap
          + dkv[:, MLA_KV_RANK + LANES:MLA_KV_RANK + 2 * LANES] * smap).astype(BF16)
    for h in range(MLA_HEADS):
        lo = h * MLA_QK_PAD
        kcat_ref[:, lo:lo + LANES] = kv[:, h * LANES:(h + 1) * LANES].astype(BF16)
        kcat_ref[:, lo + LANES:lo + 2 * LANES] = kr
    v_ref[...] = kv[:, MLA_HEADS * MLA_NOPE:].astype(BF16)


def _mla_prep(proj, cmap, smap, gq, gkv, wab, wkv, tt=256):
    t = proj.shape[0]
    tt = min(tt, t)
    wq = MLA_HEADS * MLA_QK_PAD
    wv = MLA_HEADS * MLA_DV
    return pl.pallas_call(
        functools.partial(_mla_prep_kernel, scale=math.log2(math.e) / math.sqrt(MLA_NOPE + MLA_ROPE)),
        grid=(t // tt,),
        in_specs=[
            pl.BlockSpec((tt, MLA_Q_RANK), lambda i: (i, COL_DQ // MLA_Q_RANK)),
            pl.BlockSpec((tt, DKV_PAD), lambda i: (i, COL_DKV // DKV_PAD)),
            pl.BlockSpec((tt, LANES), lambda i: (i, 0)),
            pl.BlockSpec((tt, LANES), lambda i: (i, 0)),
            pl.BlockSpec((1, MLA_Q_RANK), lambda i: (0, 0)),
            pl.BlockSpec((1, MLA_KV_RANK), lambda i: (0, 0)),
            pl.BlockSpec(wab.shape, lambda i: (0, 0)),
            pl.BlockSpec(wkv.shape, lambda i: (0, 0)),
        ],
        out_specs=[
            pl.BlockSpec((tt, wq), lambda i: (i, 0)),
            pl.BlockSpec((tt, wq), lambda i: (i, 0)),
            pl.BlockSpec((tt, wv), lambda i: (i, 0)),
        ],
        out_shape=[
            jax.ShapeDtypeStruct((t, wq), BF16),
            jax.ShapeDtypeStruct((t, wq), BF16),
            jax.ShapeDtypeStruct((t, wv), BF16),
        ],
        compiler_params=_cparams("parallel"),
        name="mla_prep",
    )(proj, proj, cmap, smap, gq.reshape(1, -1).astype(F32), gkv.reshape(1, -1).astype(F32), wab, wkv)


MLA_BQ = 512
MLA_BK = 512
MLA_HG = 2


def _mla_kernel(q_ref, k_ref, v_ref, o_ref):
    qi = pl.program_id(2)
    bq = q_ref.shape[1]
    bk = min(MLA_BK, bq)
    heads = range(MLA_HG)
    qs = [q_ref[0, :, h * MLA_QK_PAD:(h + 1) * MLA_QK_PAD] for h in heads]
    kpos = lax.broadcasted_iota(jnp.int32, (bk, bq), 0)
    qpos = qi * bq + lax.broadcasted_iota(jnp.int32, (bk, bq), 1)

    def block(kb, ms, ls, accs, masked):
        start = pl.multiple_of(kb * bk, bk)
        ss = [_nt_dot(k_ref[0, pl.ds(start, bk), h * MLA_QK_PAD:(h + 1) * MLA_QK_PAD], qs[h]) for h in heads]
        if masked:
            allowed = (kb * bk + kpos) <= qpos
            ss = [jnp.where(allowed, s, NEG) for s in ss]
        new_ms = [jnp.maximum(ms[h], jnp.max(ss[h], axis=0, keepdims=True)) for h in heads]
        ps = [jnp.exp2(ss[h] - new_ms[h]) for h in heads]
        alphas = [jnp.exp2(ms[h] - new_ms[h]) for h in heads]
        new_ls = [alphas[h] * ls[h] + jnp.sum(ps[h], axis=0, keepdims=True) for h in heads]
        pvs = [lax.dot_general(v_ref[0, pl.ds(start, bk), h * MLA_DV:(h + 1) * MLA_DV], ps[h].astype(BF16),
                               (((0,), (0,)), ((), ())), preferred_element_type=F32) for h in heads]
        new_accs = [alphas[h] * accs[h] + pvs[h] for h in heads]
        return tuple(new_ms), tuple(new_ls), tuple(new_accs)

    ms = tuple(jnp.full((1, bq), NEG, F32) for _ in heads)
    ls = tuple(jnp.zeros((1, bq), F32) for _ in heads)
    accs = tuple(jnp.zeros((MLA_DV, bq), F32) for _ in heads)
    nd = bq // bk
    ms, ls, accs = lax.fori_loop(0, qi * nd, lambda kb, cr: block(kb, cr[0], cr[1], cr[2], False), (ms, ls, accs))
    for d in range(nd):
        ms, ls, accs = block(qi * nd + d, ms, ls, accs, True)
    for h in heads:
        o_ref[0, :, h * MLA_DV:(h + 1) * MLA_DV] = (accs[h] / ls[h]).T.astype(o_ref.dtype)


def _mla_attention(qcat3, kcat3, v3):
    b, s, _ = qcat3.shape
    bq = min(MLA_BQ, s)
    wq = MLA_HG * MLA_QK_PAD
    wv = MLA_HG * MLA_DV
    return pl.pallas_call(
        _mla_kernel,
        grid=(b, MLA_HEADS // MLA_HG, s // bq),
        in_specs=[
            pl.BlockSpec((1, bq, wq), lambda bi, g, qi: (bi, qi, g)),
            pl.BlockSpec((1, s, wq), lambda bi, g, qi: (bi, 0, g), pipeline_mode=pl.Buffered(1)),
            pl.BlockSpec((1, s, wv), lambda bi, g, qi: (bi, 0, g), pipeline_mode=pl.Buffered(1)),
        ],
        out_specs=pl.BlockSpec((1, bq, wv), lambda bi, g, qi: (bi, qi, g)),
        out_shape=jax.ShapeDtypeStruct((b, s, MLA_HEADS * MLA_DV), BF16),
        compiler_params=_cparams("parallel", "parallel", "arbitrary"),
        name="mla_attention",
    )(qcat3, kcat3, v3)


def _mem_kernel(q_ref, k_ref, v_ref, o_ref, *, scale):
    s = _nt_dot(q_ref[0], k_ref[0]) * scale
    m = jnp.max(s, axis=-1, keepdims=True)
    p = jnp.exp(s - m)
    p = p / jnp.sum(p, axis=-1, keepdims=True)
    o_ref[0] = jnp.dot(p.astype(BF16), v_ref[0], preferred_element_type=F32).astype(o_ref.dtype)


def _mem_attention(proj3, mkv3, tt=512):
    b, s, _ = proj3.shape
    n_mem = mkv3.shape[1]
    tt = min(tt, s)
    q0 = COL_MEMQ // MEM_DH
    return pl.pallas_call(
        functools.partial(_mem_kernel, scale=1.0 / math.sqrt(MEM_DH)),
        grid=(b, s // tt, MEM_HEADS),
        in_specs=[
            pl.BlockSpec((1, tt, MEM_DH), lambda bi, ti, h: (bi, ti, q0 + h)),
            pl.BlockSpec((1, n_mem, MEM_DH), lambda bi, ti, h: (bi, 0, h)),
            pl.BlockSpec((1, n_mem, MEM_DH), lambda bi, ti, h: (bi, 0, MEM_HEADS + h)),
        ],
        out_specs=pl.BlockSpec((1, tt, MEM_DH), lambda bi, ti, h: (bi, ti, h)),
        out_shape=jax.ShapeDtypeStruct((b, s, MEM_W), BF16),
        compiler_params=_cparams("parallel", "parallel", "parallel"),
        name="mem_attention",
    )(proj3, mkv3, mkv3)


def _merge_kernel(ya_ref, yb_ref, yc_ref, ga_ref, gb_ref, gc_ref, w_ref, o_ref):
    acc = jax.nn.sigmoid(ga_ref[...].astype(F32)) * jnp.dot(ya_ref[...], w_ref[0], preferred_element_type=F32)
    acc += jax.nn.sigmoid(gb_ref[...].astype(F32)) * jnp.dot(yb_ref[...], w_ref[1], preferred_element_type=F32)
    acc += jax.nn.sigmoid(gc_ref[...].astype(F32)) * jnp.dot(yc_ref[...], w_ref[2], preferred_element_type=F32)
    o_ref[...] = acc.astype(o_ref.dtype)


def _merge(y_sb, y_mla, y_mem, proj, wb, d, tm=1024, tn=512):
    t = y_sb.shape[0]
    tm, tn = min(tm, t), min(tn, d)
    g0 = COL_GATE // tn
    gstep = d // tn

    def gate_spec(br):
        return pl.BlockSpec((tm, tn), lambda j, i: (i, g0 + br * gstep + j))

    yspec = pl.BlockSpec((tm, BRANCH_W), lambda j, i: (i, 0))
    return pl.pallas_call(
        _merge_kernel,
        grid=(d // tn, t // tm),
        in_specs=[yspec, yspec, yspec, gate_spec(0), gate_spec(1), gate_spec(2),
                  pl.BlockSpec((N_BRANCH, BRANCH_W, tn), lambda j, i: (0, 0, j))],
        out_specs=pl.BlockSpec((tm, tn), lambda j, i: (i, j)),
        out_shape=jax.ShapeDtypeStruct((t, d), BF16),
        compiler_params=_cparams("parallel", "parallel"),
        name="branch_merge",
    )(y_sb, y_mla, y_mem, proj, proj, proj, wb)


PEER_RANK_NONE = 255.0
PEER_CAND_WIDTH = [PEER_TOPK // (a + 1) for a in range(PEER_TOPK)]


def _take_top(vals, n, with_rank=False):
    out = []
    rem = vals
    rank = jnp.full(vals.shape, PEER_RANK_NONE, F32) if with_rank else None
    for k in range(n):
        mx = jnp.max(rem, axis=0, keepdims=True)
        out.append(mx)
        hit = rem == mx
        if with_rank:
            rank = jnp.where(hit, float(k), rank)
        rem = jnp.where(hit, -jnp.inf, rem)
    return (out, rank) if with_rank else out


def _peer_route_kernel(q_ref, keys_ref, n0_ref, w0_ref, r1_ref, e1_ref):
    def head(h, carry):
        s0 = _nt_dot(keys_ref[h, 0], q_ref[2 * h].astype(BF16))
        s1 = _nt_dot(keys_ref[h, 1], q_ref[2 * h + 1].astype(BF16))
        top0, rank0 = _take_top(s0, PEER_TOPK, with_rank=True)
        top1, rank1 = _take_top(s1, PEER_TOPK, with_rank=True)
        cand = [[top0[a] + top1[b] for b in range(PEER_CAND_WIDTH[a])] for a in range(PEER_TOPK)]
        flat = [c for row in cand for c in row]
        pad = -len(flat) % 8
        stacked = jnp.concatenate(flat + [jnp.full_like(flat[0], -jnp.inf)] * pad, axis=0)
        ctop = _take_top(stacked, PEER_TOPK)
        cmax, tau = ctop[0], ctop[PEER_TOPK - 1]
        zsum = jnp.zeros_like(cmax)
        for cv in ctop:
            zsum = zsum + jnp.exp(cv - cmax)
        n0 = jnp.zeros_like(s0)
        for a in range(PEER_TOPK):
            count = jnp.zeros_like(tau)
            for c in cand[a]:
                count = count + jnp.where(c >= tau, 1.0, 0.0)
            n0 = jnp.where(rank0 == float(a), count, n0)
        n0_ref[h] = n0
        w0_ref[h] = jnp.exp(s0 - top0[0]) / zsum
        r1_ref[h] = pltpu.bitcast(rank1.astype(BF16), jnp.uint32)
        e1_ref[h] = pltpu.bitcast(jnp.exp(s1 - top1[0]).astype(BF16), jnp.uint32)
        return carry

    lax.fori_loop(0, PEER_HEADS, head, 0)


def _peer_route(q3, keys, tt=256):
    _, t, dh = q3.shape
    tt = min(tt, t)
    spec = pl.BlockSpec((PEER_HEADS, PEER_NKEYS, tt), lambda i: (0, 0, i))
    pspec = pl.BlockSpec((PEER_HEADS, PEER_NKEYS // 2, tt), lambda i: (0, 0, i))
    f32s = jax.ShapeDtypeStruct((PEER_HEADS, PEER_NKEYS, t), F32)
    packed = jax.ShapeDtypeStruct((PEER_HEADS, PEER_NKEYS // 2, t), jnp.uint32)
    return pl.pallas_call(
        _peer_route_kernel,
        grid=(t // tt,),
        in_specs=[pl.BlockSpec((2 * PEER_HEADS, tt, dh), lambda i: (0, i, 0)),
                  pl.BlockSpec(keys.shape, lambda i: (0, 0, 0, 0))],
        out_specs=[spec, spec, pspec, pspec],
        out_shape=[f32s, f32s, packed, packed],
        compiler_params=_cparams("parallel"),
        name="peer_route",
    )(q3, keys)


PEER_EB = 512
PEER_GROUP = PEER_EB // PEER_NKEYS
PEER_ROWS = 16


def _peer_dense_kernel(xn_ref, u_ref, v_ref, nw_ref, r1_ref, e1_ref, o_ref, at_ref):
    e = pl.program_id(1)

    @pl.when(e == 0)
    def _():
        o_ref[...] = jnp.zeros_like(o_ref)

    pre = _nt_dot(u_ref[...], xn_ref[...])
    at_ref[...] = (0.5 * pre * (1.0 + lax.erf(pre * (1.0 / math.sqrt(2.0))))).astype(BF16)
    tt = at_ref.shape[1]
    zero = jnp.zeros((PEER_ROWS, LANES), BF16)
    for ii in range(PEER_GROUP):
        for l0 in range(0, tt, LANES):
            lanes = slice(l0, l0 + LANES)
            counts = [jnp.broadcast_to(nw_ref[0, h, ii:ii + 1, lanes], (PEER_ROWS, LANES)).astype(BF16)
                      for h in range(PEER_HEADS)]
            weights = [jnp.broadcast_to(nw_ref[0, h, PEER_GROUP + ii:PEER_GROUP + ii + 1, lanes],
                                        (PEER_ROWS, LANES)).astype(BF16) for h in range(PEER_HEADS)]
            for r0 in range(0, PEER_NKEYS, PEER_ROWS):
                words = slice(r0 // 2, (r0 + PEER_ROWS) // 2)
                gate = zero
                for h in range(PEER_HEADS):
                    rank = pltpu.bitcast(r1_ref[h, words, lanes], BF16)
                    e1 = pltpu.bitcast(e1_ref[h, words, lanes], BF16)
                    gate = gate + jnp.where(rank < counts[h], e1 * weights[h], zero)
                e0 = ii * PEER_NKEYS + r0
                at_ref[e0:e0 + PEER_ROWS, lanes] = at_ref[e0:e0 + PEER_ROWS, lanes] * gate
    o_ref[...] += lax.dot_general(at_ref[...], v_ref[...], (((0,), (0,)), ((), ())), preferred_element_type=F32)


def _peer_dense(xn, u, v, nw, r1, e1, tt=512):
    t, d = xn.shape
    n = u.shape[0]
    tt = min(tt, t)
    hspec = pl.BlockSpec((PEER_HEADS, PEER_NKEYS // 2, tt), lambda ti, e: (0, 0, ti))
    return pl.pallas_call(
        _peer_dense_kernel,
        grid=(t // tt, n // PEER_EB),
        in_specs=[
            pl.BlockSpec((tt, d), lambda ti, e: (ti, 0)),
            pl.BlockSpec((PEER_EB, d), lambda ti, e: (e, 0)),
            pl.BlockSpec((PEER_EB, d), lambda ti, e: (e, 0)),
            pl.BlockSpec((1, PEER_HEADS, 2 * PEER_GROUP, tt), lambda ti, e: (e, 0, 0, ti)),
            hspec,
            hspec,
        ],
        out_specs=pl.BlockSpec((tt, d), lambda ti, e: (ti, 0)),
        out_shape=jax.ShapeDtypeStruct((t, d), F32),
        scratch_shapes=[pltpu.VMEM((PEER_EB, tt), BF16)],
        compiler_params=_cparams("parallel", "arbitrary"),
        name="peer_dense",
    )(xn, u, v, nw, r1, e1)


def _pack_w_in(w_in, d):
    c1 = 3 * SB_W
    c2 = c1 + MLA_Q_RANK
    c3 = c2 + MLA_KV_RANK + MLA_ROPE
    c4 = c3 + MEM_W
    half = MLA_ROPE // 2
    kr = w_in[:, c2 + MLA_KV_RANK:c3]
    z = lambda n: jnp.zeros((d, n), w_in.dtype)
    dkv = jnp.concatenate([
        w_in[:, c2:c2 + MLA_KV_RANK],
        kr, z(LANES - MLA_ROPE),
        kr[:, half:], kr[:, :half], z(LANES - MLA_ROPE),
        z(DKV_PAD - MLA_KV_RANK - 2 * LANES)], axis=1)
    return jnp.concatenate([w_in[:, :c2], dkv, w_in[:, c3:c4], w_in[:, c4:]], axis=1)


def _pack_mla_weights(w_uq, w_ukv):
    half = MLA_ROPE // 2
    rq = w_uq.shape[0]
    wq = w_uq.reshape(rq, MLA_HEADS, MLA_NOPE + MLA_ROPE)
    nope, t1, t2 = wq[:, :, :MLA_NOPE], wq[:, :, MLA_NOPE:MLA_NOPE + half], wq[:, :, MLA_NOPE + half:]
    zq = jnp.zeros((rq, MLA_HEADS, LANES - MLA_ROPE), w_uq.dtype)
    w_a = jnp.concatenate([nope, t1, t2, zq], axis=2).reshape(rq, MLA_HEADS * MLA_QK_PAD)
    w_b = jnp.concatenate([t2, t1, zq], axis=2).reshape(rq, MLA_HEADS * LANES)
    wab = jnp.concatenate([w_a, w_b], axis=1).astype(BF16)
    rkv = w_ukv.shape[0]
    wkv = w_ukv.reshape(rkv, MLA_HEADS, MLA_NOPE + MLA_DV)
    wkv = jnp.concatenate([wkv[:, :, :MLA_NOPE].reshape(rkv, -1), wkv[:, :, MLA_NOPE:].reshape(rkv, -1)], axis=1)
    return wab, wkv.astype(BF16)


def _rope_maps(positions):
    half = MLA_ROPE // 2
    freqs = ROPE_BASE ** (-jnp.arange(half, dtype=F32) / half)
    ang = positions.astype(F32).reshape(-1)[:, None] * freqs
    cos, sin = jnp.cos(ang), jnp.sin(ang)
    z = jnp.zeros((ang.shape[0], LANES - MLA_ROPE), F32)
    return jnp.concatenate([cos, cos, z], axis=1), jnp.concatenate([-sin, sin, z], axis=1)


def _layer(h, mem, cmap, smap, g_mix, w_in, mla_g_q, mla_w_uq, mla_g_kv, mla_w_ukv, g_mem, w_mem_kv,
           w_branch, w_out, g_ffn, peer_w_q, peer_sub_keys, peer_u, peer_v):
    b, s, d = h.shape
    t = b * s
    n_mem = mem.shape[1]
    h2 = h.reshape(t, d)

    xn = _rmsnorm(h2, g_mix, BF16)
    proj = _matmul(xn, _pack_w_in(w_in.astype(BF16), d), BF16, tm=512, tn=1024, name="in_proj")
    proj3 = proj.reshape(b, s, -1)

    y_sb = _sb_attention(proj3).reshape(t, SB_W)

    wab, wkv = _pack_mla_weights(mla_w_uq, mla_w_ukv)
    qcat, kcat, v = _mla_prep(proj, cmap, smap, mla_g_q, mla_g_kv, wab, wkv)
    y_mla = _mla_attention(qcat.reshape(b, s, -1), kcat.reshape(b, s, -1), v.reshape(b, s, -1)).reshape(t, -1)

    mem_n = _rmsnorm(mem.reshape(b * n_mem, d), g_mem, BF16)
    mkv = _matmul(mem_n, w_mem_kv.astype(BF16), BF16, tm=512, tn=1024, name="mem_kv")
    y_mem = _mem_attention(proj3, mkv.reshape(b, n_mem, 2 * MEM_W)).reshape(t, MEM_W)

    merged = _merge(y_sb, y_mla, y_mem, proj, w_branch.astype(BF16), d)
    h1 = _matmul(merged, w_out.astype(BF16), F32, tm=512, tn=1024, residual=h2, name="out_proj")

    xn2 = _rmsnorm(h1, g_ffn, BF16)
    q = _matmul(xn2, peer_w_q.astype(BF16), F32, tm=512, tn=1024, name="peer_q")
    dh = PEER_DK // 2
    q3 = q.reshape(t, 2 * PEER_HEADS, dh).transpose(1, 0, 2)
    n0, w0, r1, e1 = _peer_route(q3, peer_sub_keys.astype(BF16))
    ngrp = PEER_NKEYS // PEER_GROUP
    nw = jnp.concatenate([n0.reshape(PEER_HEADS, ngrp, PEER_GROUP, t), w0.reshape(PEER_HEADS, ngrp, PEER_GROUP, t)],
                         axis=2).transpose(1, 0, 2, 3)
    y = _peer_dense(xn2, peer_u.astype(BF16), peer_v.astype(BF16), nw, r1, e1)
    return h1, y


def kernel(x, mem, positions, g_mix, w_in, mla_g_q, mla_w_uq, mla_g_kv, mla_w_ukv, g_mem, w_mem_kv, w_branch,
           w_out, g_ffn, peer_w_q, peer_sub_keys, peer_u, peer_v, g_final):
    b, s, d = x.shape
    cmap, smap = _rope_maps(positions)
    depth = w_in.shape[0]
    h = x
    for layer in range(depth):
        h1, y = _layer(h, mem, cmap, smap, g_mix[layer], w_in[layer], mla_g_q[layer], mla_w_uq[layer],
                       mla_g_kv[layer], mla_w_ukv[layer], g_mem[layer], w_mem_kv[layer], w_branch[layer],
                       w_out[layer], g_ffn[layer], peer_w_q[layer], peer_sub_keys[layer], peer_u[layer],
                       peer_v[layer])
        if layer + 1 < depth:
            h = (h1 + y).reshape(b, s, d)
    return _add_rmsnorm(h1, y, g_final).reshape(b, s, d)
```

```python
import functools
import math

import jax
import jax.numpy as jnp
from jax import lax
from jax.experimental import pallas as pl
from jax.experimental.pallas import tpu as pltpu

F32 = jnp.float32
BF16 = jnp.bfloat16

EPS = 1e-6
NEG = -1e30
LANES = 128

SB_HEADS = 16
SB_DH = 128
SB_W = SB_HEADS * SB_DH
MLA_HEADS = 16
MLA_Q_RANK = 1024
MLA_KV_RANK = 512
MLA_NOPE = 128
MLA_ROPE = 64
MLA_DV = 128
MLA_QK_PAD = 256
ROPE_BASE = 10000.0
MEM_HEADS = 4
MEM_DH = 512
MEM_W = MEM_HEADS * MEM_DH
N_BRANCH = 3
BRANCH_W = 2048
PEER_HEADS = 8
PEER_NKEYS = 128
PEER_DK = 256
PEER_TOPK = 16

COL_SB = 0
COL_DQ = 3 * SB_W
COL_DKV = COL_DQ + MLA_Q_RANK
DKV_PAD = 1024
COL_MEMQ = COL_DKV + DKV_PAD
COL_GATE = COL_MEMQ + MEM_W

VMEM_LIMIT_BYTES = 56 * 1024 * 1024


def _cparams(*sem):
    return pltpu.CompilerParams(dimension_semantics=sem, vmem_limit_bytes=VMEM_LIMIT_BYTES)


def _nt_dot(a, b):
    return lax.dot_general(a, b, (((1,), (1,)), ((), ())), preferred_element_type=F32)


def _rmsnorm_kernel(x_ref, g_ref, o_ref):
    x = x_ref[...].astype(F32)
    ms = jnp.mean(x * x, axis=-1, keepdims=True)
    o_ref[...] = (x * lax.rsqrt(ms + EPS) * g_ref[...]).astype(o_ref.dtype)


def _rmsnorm(x, g, out_dtype, tm=256):
    m, d = x.shape
    tm = min(tm, m)
    return pl.pallas_call(
        _rmsnorm_kernel,
        grid=(m // tm,),
        in_specs=[pl.BlockSpec((tm, d), lambda i: (i, 0)), pl.BlockSpec((1, d), lambda i: (0, 0))],
        out_specs=pl.BlockSpec((tm, d), lambda i: (i, 0)),
        out_shape=jax.ShapeDtypeStruct((m, d), out_dtype),
        compiler_params=_cparams("parallel"),
        name="rmsnorm",
    )(x, g.reshape(1, d).astype(F32))


def _add_rmsnorm_kernel(a_ref, b_ref, g_ref, o_ref):
    x = a_ref[...] + b_ref[...]
    ms = jnp.mean(x * x, axis=-1, keepdims=True)
    o_ref[...] = (x * lax.rsqrt(ms + EPS) * g_ref[...]).astype(o_ref.dtype)


def _add_rmsnorm(a, b, g, tm=256):
    m, d = a.shape
    tm = min(tm, m)
    return pl.pallas_call(
        _add_rmsnorm_kernel,
        grid=(m // tm,),
        in_specs=[pl.BlockSpec((tm, d), lambda i: (i, 0)), pl.BlockSpec((tm, d), lambda i: (i, 0)),
                  pl.BlockSpec((1, d), lambda i: (0, 0))],
        out_specs=pl.BlockSpec((tm, d), lambda i: (i, 0)),
        out_shape=jax.ShapeDtypeStruct((m, d), F32),
        compiler_params=_cparams("parallel"),
        name="add_rmsnorm",
    )(a, b, g.reshape(1, d).astype(F32))


def _mm_kernel(a_ref, b_ref, o_ref):
    o_ref[...] = jnp.dot(a_ref[...], b_ref[...], preferred_element_type=F32).astype(o_ref.dtype)


def _mm_res_kernel(a_ref, b_ref, r_ref, o_ref):
    acc = jnp.dot(a_ref[...], b_ref[...], preferred_element_type=F32)
    o_ref[...] = (acc + r_ref[...]).astype(o_ref.dtype)


def _mm_split_kernel(a_ref, b_ref, o_ref):
    acc = jnp.dot(a_ref[...], b_ref[...], preferred_element_type=F32)
    for c in range(o_ref.shape[0]):
        o_ref[c] = acc[:, c * LANES:(c + 1) * LANES].astype(o_ref.dtype)


def _matmul_split(a, b, out_dtype, tm, tn, name):
    m, k = a.shape
    _, n = b.shape
    tm, tn = min(tm, m), min(tn, n)
    return pl.pallas_call(
        _mm_split_kernel,
        grid=(n // tn, m // tm),
        in_specs=[pl.BlockSpec((tm, k), lambda j, i: (i, 0)), pl.BlockSpec((k, tn), lambda j, i: (0, j))],
        out_specs=pl.BlockSpec((tn // LANES, tm, LANES), lambda j, i: (j, i, 0)),
        out_shape=jax.ShapeDtypeStruct((n // LANES, m, LANES), out_dtype),
        compiler_params=_cparams("parallel", "parallel"),
        name=name,
    )(a, b)


def _matmul(a, b, out_dtype, tm, tn, residual=None, name="matmul"):
    m, k = a.shape
    _, n = b.shape
    tm, tn = min(tm, m), min(tn, n)
    in_specs = [pl.BlockSpec((tm, k), lambda j, i: (i, 0)), pl.BlockSpec((k, tn), lambda j, i: (0, j))]
    args = [a, b]
    body = _mm_kernel
    if residual is not None:
        in_specs.append(pl.BlockSpec((tm, tn), lambda j, i: (i, j)))
        args.append(residual)
        body = _mm_res_kernel
    return pl.pallas_call(
        body,
        grid=(n // tn, m // tm),
        in_specs=in_specs,
        out_specs=pl.BlockSpec((tm, tn), lambda j, i: (i, j)),
        out_shape=jax.ShapeDtypeStruct((m, n), out_dtype),
        compiler_params=_cparams("parallel", "parallel"),
        name=name,
    )(*args)


SB_BQ = 256
SB_BK = 128
SB_HG = 4
SB_STOP = -105.0


def _sb_kernel(q_ref, k_ref, v_ref, o_ref, *, scale):
    qi = pl.program_id(2)
    bq = q_ref.shape[1]
    qs = [(q_ref[0, :, h * SB_DH:(h + 1) * SB_DH].astype(F32) * scale).astype(BF16) for h in range(SB_HG)]
    r = lax.broadcasted_iota(jnp.int32, (SB_BK, 2 * SB_BK), 0)
    c = lax.broadcasted_iota(jnp.int32, (SB_BK, 2 * SB_BK), 1)
    u_ext = jnp.where((c >= SB_BK) | (r > c), 1.0, 0.0).astype(BF16)
    qpos = qi * bq + lax.broadcasted_iota(jnp.int32, (bq, SB_BK), 0)
    kofs = lax.broadcasted_iota(jnp.int32, (bq, SB_BK), 1)

    def block(kb, runs, accs, masked):
        start = pl.multiple_of(kb * SB_BK, SB_BK)
        if masked:
            strict = (kb * SB_BK + kofs) < qpos
        heads = range(SB_HG)
        zs = [_nt_dot(qs[h], k_ref[0, pl.ds(start, SB_BK), h * SB_DH:(h + 1) * SB_DH]) for h in heads]
        lks = [-(jnp.maximum(z, 0.0) + jnp.log(1.0 + jnp.exp(-jnp.abs(z)))) for z in zs]
        lkms = [jnp.where(strict, lk, 0.0) for lk in lks] if masked else lks
        his = [lkm.astype(BF16) for lkm in lkms]
        los = [(lkm - hi.astype(F32)).astype(BF16) for lkm, hi in zip(lkms, his)]
        css = [jnp.dot(hi, u_ext, preferred_element_type=F32) + jnp.dot(lo, u_ext, preferred_element_type=F32)
               for hi, lo in zip(his, los)]
        ws = [jnp.exp(zs[h] + lks[h] + runs[h] + css[h][:, :SB_BK]) for h in heads]
        if masked:
            ws = [jnp.where(strict, w, 0.0) for w in ws]
        new_accs = [accs[h] + jnp.dot(ws[h].astype(BF16), v_ref[0, pl.ds(start, SB_BK), h * SB_DH:(h + 1) * SB_DH],
                                      preferred_element_type=F32) for h in heads]
        new_runs = [runs[h] + css[h][:, SB_BK:] for h in heads]
        return tuple(new_runs), tuple(new_accs)

    def least_decayed(runs):
        mx = jnp.max(runs[0])
        for h in range(1, SB_HG):
            mx = jnp.maximum(mx, jnp.max(runs[h]))
        return mx

    runs = tuple(jnp.zeros((bq, SB_BK), F32) for _ in range(SB_HG))
    accs = tuple(jnp.zeros((bq, SB_DH), F32) for _ in range(SB_HG))
    nd = bq // SB_BK
    for d in range(nd):
        runs, accs = block(qi * nd + (nd - 1 - d), runs, accs, True)

    def cond(state):
        kb, mx, _, _ = state
        return jnp.logical_and(kb >= 0, mx > SB_STOP)

    def body(state):
        kb, _, rs, ac = state
        rs, ac = block(kb, rs, ac, False)
        return kb - 1, least_decayed(rs), rs, ac

    _, _, _, accs = lax.while_loop(cond, body, (qi * nd - 1, least_decayed(runs), runs, accs))
    for h in range(SB_HG):
        o_ref[0, :, h * SB_DH:(h + 1) * SB_DH] = accs[h].astype(o_ref.dtype)


def _sb_attention(proj3):
    b, s, _ = proj3.shape
    bq = min(SB_BQ, s)
    ng = SB_HEADS // SB_HG
    wg = SB_HG * SB_DH
    return pl.pallas_call(
        functools.partial(_sb_kernel, scale=1.0 / math.sqrt(SB_DH)),
        grid=(b, ng, s // bq),
        in_specs=[
            pl.BlockSpec((1, bq, wg), lambda bi, g, qi: (bi, qi, g)),
            pl.BlockSpec((1, s, wg), lambda bi, g, qi: (bi, 0, ng + g), pipeline_mode=pl.Buffered(1)),
            pl.BlockSpec((1, s, wg), lambda bi, g, qi: (bi, 0, 2 * ng + g), pipeline_mode=pl.Buffered(1)),
        ],
        out_specs=pl.BlockSpec((1, bq, wg), lambda bi, g, qi: (bi, qi, g)),
        out_shape=jax.ShapeDtypeStruct((b, s, SB_W), BF16),
        compiler_params=_cparams("parallel", "parallel", "arbitrary"),
        name="sb_attention",
    )(proj3, proj3, proj3)


def _mla_prep_kernel(dq_ref, dkv_ref, c_ref, s_ref, gq_ref, gkv_ref, wab_ref, wkv_ref,
                     qcat_ref, kcat_ref, v_ref, *, scale):
    cmap = c_ref[...]
    smap = s_ref[...]
    dq = dq_ref[...].astype(F32)
    cq = dq * lax.rsqrt(jnp.mean(dq * dq, axis=-1, keepdims=True) + EPS) * gq_ref[...]
    qab = jnp.dot(cq.astype(BF16), wab_ref[...], preferred_element_type=F32)
    swap0 = MLA_HEADS * MLA_QK_PAD
    for h in range(MLA_HEADS):
        lo = h * MLA_QK_PAD
        qcat_ref[:, lo:lo + LANES] = (qab[:, lo:lo + LANES] * scale).astype(BF16)
        rope = qab[:, lo + LANES:lo + 2 * LANES] * cmap + qab[:, swap0 + h * LANES:swap0 + (h + 1) * LANES] * smap
        qcat_ref[:, lo + LANES:lo + 2 * LANES] = (rope * scale).astype(BF16)
    dkv = dkv_ref[...].astype(F32)
    ckv_in = dkv[:, :MLA_KV_RANK]
    ckv = ckv_in * lax.rsqrt(jnp.mean(ckv_in * ckv_in, axis=-1, keepdims=True) + EPS) * gkv_ref[...]
    kv = jnp.dot(ckv.astype(BF16), wkv_ref[...], preferred_element_type=F32)
    kr = (dkv[:, MLA_KV_RANK:MLA_KV_RANK + LANES] * cmap
          + dkv[:, MLA_KV_RANK + LANES:MLA_KV_RANK + 2 * LANES] * smap).astype(BF16)
    for h in range(MLA_HEADS):
        lo = h * MLA_QK_PAD
        kcat_ref[:, lo:lo + LANES] = kv[:, h * LANES:(h + 1) * LANES].astype(BF16)
        kcat_ref[:, lo + LANES:lo + 2 * LANES] = kr
    v_ref[...] = kv[:, MLA_HEADS * MLA_NOPE:].astype(BF16)


def _mla_prep(proj, cmap, smap, gq, gkv, wab, wkv, tt=256):
    t = proj.shape[0]
    tt = min(tt, t)
    wq = MLA_HEADS * MLA_QK_PAD
    wv = MLA_HEADS * MLA_DV
    return pl.pallas_call(
        functools.partial(_mla_prep_kernel, scale=math.log2(math.e) / math.sqrt(MLA_NOPE + MLA_ROPE)),
        grid=(t // tt,),
        in_specs=[
            pl.BlockSpec((tt, MLA_Q_RANK), lambda i: (i, COL_DQ // MLA_Q_RANK)),
            pl.BlockSpec((tt, DKV_PAD), lambda i: (i, COL_DKV // DKV_PAD)),
            pl.BlockSpec((tt, LANES), lambda i: (i, 0)),
            pl.BlockSpec((tt, LANES), lambda i: (i, 0)),
            pl.BlockSpec((1, MLA_Q_RANK), lambda i: (0, 0)),
            pl.BlockSpec((1, MLA_KV_RANK), lambda i: (0, 0)),
            pl.BlockSpec(wab.shape, lambda i: (0, 0)),
            pl.BlockSpec(wkv.shape, lambda i: (0, 0)),
        ],
        out_specs=[
            pl.BlockSpec((tt, wq), lambda i: (i, 0)),
            pl.BlockSpec((tt, wq), lambda i: (i, 0)),
            pl.BlockSpec((tt, wv), lambda i: (i, 0)),
        ],
        out_shape=[
            jax.ShapeDtypeStruct((t, wq), BF16),
            jax.ShapeDtypeStruct((t, wq), BF16),
            jax.ShapeDtypeStruct((t, wv), BF16),
        ],
        compiler_params=_cparams("parallel"),
        name="mla_prep",
    )(proj, proj, cmap, smap, gq.reshape(1, -1).astype(F32), gkv.reshape(1, -1).astype(F32), wab, wkv)


MLA_BQ = 512
MLA_BK = 512
MLA_HG = 2


def _mla_kernel(q_ref, k_ref, v_ref, o_ref):
    qi = pl.program_id(2)
    bq = q_ref.shape[1]
    bk = min(MLA_BK, bq)
    heads = range(MLA_HG)
    qs = [q_ref[0, :, h * MLA_QK_PAD:(h + 1) * MLA_QK_PAD] for h in heads]
    kpos = lax.broadcasted_iota(jnp.int32, (bk, bq), 0)
    qpos = qi * bq + lax.broadcasted_iota(jnp.int32, (bk, bq), 1)

    def block(kb, ms, ls, accs, masked):
        start = pl.multiple_of(kb * bk, bk)
        ss = [_nt_dot(k_ref[0, pl.ds(start, bk), h * MLA_QK_PAD:(h + 1) * MLA_QK_PAD], qs[h]) for h in heads]
        if masked:
            allowed = (kb * bk + kpos) <= qpos
            ss = [jnp.where(allowed, s, NEG) for s in ss]
        new_ms = [jnp.maximum(ms[h], jnp.max(ss[h], axis=0, keepdims=True)) for h in heads]
        ps = [jnp.exp2(ss[h] - new_ms[h]) for h in heads]
        alphas = [jnp.exp2(ms[h] - new_ms[h]) for h in heads]
        new_ls = [alphas[h] * ls[h] + jnp.sum(ps[h], axis=0, keepdims=True) for h in heads]
        pvs = [lax.dot_general(v_ref[0, pl.ds(start, bk), h * MLA_DV:(h + 1) * MLA_DV], ps[h].astype(BF16),
                               (((0,), (0,)), ((), ())), preferred_element_type=F32) for h in heads]
        new_accs = [alphas[h] * accs[h] + pvs[h] for h in heads]
        return tuple(new_ms), tuple(new_ls), tuple(new_accs)

    ms = tuple(jnp.full((1, bq), NEG, F32) for _ in heads)
    ls = tuple(jnp.zeros((1, bq), F32) for _ in heads)
    accs = tuple(jnp.zeros((MLA_DV, bq), F32) for _ in heads)
    nd = bq // bk
    ms, ls, accs = lax.fori_loop(0, qi * nd, lambda kb, cr: block(kb, cr[0], cr[1], cr[2], False), (ms, ls, accs))
    for d in range(nd):
        ms, ls, accs = block(qi * nd + d, ms, ls, accs, True)
    for h in heads:
        o_ref[0, :, h * MLA_DV:(h + 1) * MLA_DV] = (accs[h] / ls[h]).T.astype(o_ref.dtype)


def _mla_attention(qcat3, kcat3, v3):
    b, s, _ = qcat3.shape
    bq = min(MLA_BQ, s)
    wq = MLA_HG * MLA_QK_PAD
    wv = MLA_HG * MLA_DV
    return pl.pallas_call(
        _mla_kernel,
        grid=(b, MLA_HEADS // MLA_HG, s // bq),
        in_specs=[
            pl.BlockSpec((1, bq, wq), lambda bi, g, qi: (bi, qi, g)),
            pl.BlockSpec((1, s, wq), lambda bi, g, qi: (bi, 0, g), pipeline_mode=pl.Buffered(1)),
            pl.BlockSpec((1, s, wv), lambda bi, g, qi: (bi, 0, g), pipeline_mode=pl.Buffered(1)),
        ],
        out_specs=pl.BlockSpec((1, bq, wv), lambda bi, g, qi: (bi, qi, g)),
        out_shape=jax.ShapeDtypeStruct((b, s, MLA_HEADS * MLA_DV), BF16),
        compiler_params=_cparams("parallel", "parallel", "arbitrary"),
        name="mla_attention",
    )(qcat3, kcat3, v3)


def _mem_kernel(q_ref, k_ref, v_ref, o_ref, *, scale):
    s = _nt_dot(q_ref[0], k_ref[0]) * scale
    m = jnp.max(s, axis=-1, keepdims=True)
    p = jnp.exp(s - m)
    p = p / jnp.sum(p, axis=-1, keepdims=True)
    o_ref[0] = jnp.dot(p.astype(BF16), v_ref[0], preferred_element_type=F32).astype(o_ref.dtype)


def _mem_attention(proj3, mkv3, tt=512):
    b, s, _ = proj3.shape
    n_mem = mkv3.shape[1]
    tt = min(tt, s)
    q0 = COL_MEMQ // MEM_DH
    return pl.pallas_call(
        functools.partial(_mem_kernel, scale=1.0 / math.sqrt(MEM_DH)),
        grid=(b, s // tt, MEM_HEADS),
        in_specs=[
            pl.BlockSpec((1, tt, MEM_DH), lambda bi, ti, h: (bi, ti, q0 + h)),
            pl.BlockSpec((1, n_mem, MEM_DH), lambda bi, ti, h: (bi, 0, h)),
            pl.BlockSpec((1, n_mem, MEM_DH), lambda bi, ti, h: (bi, 0, MEM_HEADS + h)),
        ],
        out_specs=pl.BlockSpec((1, tt, MEM_DH), lambda bi, ti, h: (bi, ti, h)),
        out_shape=jax.ShapeDtypeStruct((b, s, MEM_W), BF16),
        compiler_params=_cparams("parallel", "parallel", "parallel"),
        name="mem_attention",
    )(proj3, mkv3, mkv3)


def _merge_kernel(ya_ref, yb_ref, yc_ref, ga_ref, gb_ref, gc_ref, w_ref, o_ref):
    acc = jax.nn.sigmoid(ga_ref[...].astype(F32)) * jnp.dot(ya_ref[...], w_ref[0], preferred_element_type=F32)
    acc += jax.nn.sigmoid(gb_ref[...].astype(F32)) * jnp.dot(yb_ref[...], w_ref[1], preferred_element_type=F32)
    acc += jax.nn.sigmoid(gc_ref[...].astype(F32)) * jnp.dot(yc_ref[...], w_ref[2], preferred_element_type=F32)
    o_ref[...] = acc.astype(o_ref.dtype)


def _merge(y_sb, y_mla, y_mem, proj, wb, d, tm=1024, tn=512):
    t = y_sb.shape[0]
    tm, tn = min(tm, t), min(tn, d)
    g0 = COL_GATE // tn
    gstep = d // tn

    def gate_spec(br):
        return pl.BlockSpec((tm, tn), lambda j, i: (i, g0 + br * gstep + j))

    yspec = pl.BlockSpec((tm, BRANCH_W), lambda j, i: (i, 0))
    return pl.pallas_call(
        _merge_kernel,
        grid=(d // tn, t // tm),
        in_specs=[yspec, yspec, yspec, gate_spec(0), gate_spec(1), gate_spec(2),
                  pl.BlockSpec((N_BRANCH, BRANCH_W, tn), lambda j, i: (0, 0, j))],
        out_specs=pl.BlockSpec((tm, tn), lambda j, i: (i, j)),
        out_shape=jax.ShapeDtypeStruct((t, d), BF16),
        compiler_params=_cparams("parallel", "parallel"),
        name="branch_merge",
    )(y_sb, y_mla, y_mem, proj, proj, proj, wb)


PEER_EB = 512
PEER_GROUP = PEER_EB // PEER_NKEYS
PEER_RANK_NONE = 255.0
PEER_CAND_WIDTH = [PEER_TOPK // (a + 1) for a in range(PEER_TOPK)]


def _take_top(vals, n, with_rank=False):
    out = []
    rem = vals
    rank = jnp.full(vals.shape, PEER_RANK_NONE, F32) if with_rank else None
    for k in range(n):
        mx = jnp.max(rem, axis=0, keepdims=True)
        out.append(mx)
        hit = rem == mx
        if with_rank:
            rank = jnp.where(hit, float(k), rank)
        rem = jnp.where(hit, -jnp.inf, rem)
    return (out, rank) if with_rank else out


def _peer_route_kernel(q_ref, keys_ref, nw_ref, r1_ref, e1_ref):
    def head(h, carry):
        s0 = _nt_dot(keys_ref[h, 0], q_ref[2 * h].astype(BF16))
        s1 = _nt_dot(keys_ref[h, 1], q_ref[2 * h + 1].astype(BF16))
        top0 = _take_top(s0, PEER_TOPK)
        top1, rank1 = _take_top(s1, PEER_TOPK, with_rank=True)
        cand = [[top0[a] + top1[b] for b in range(PEER_CAND_WIDTH[a])] for a in range(PEER_TOPK)]
        flat = [c for row in cand for c in row]
        pad = -len(flat) % 8
        stacked = jnp.concatenate(flat + [jnp.full_like(flat[0], -jnp.inf)] * pad, axis=0)
        ctop = _take_top(stacked, PEER_TOPK)
        cmax, tau = ctop[0], ctop[PEER_TOPK - 1]
        zsum = jnp.zeros_like(cmax)
        for cv in ctop:
            zsum = zsum + jnp.exp(cv - cmax)
        n0 = jnp.zeros_like(s0)
        for a in range(PEER_TOPK):
            count = jnp.zeros_like(tau)
            for c in cand[a]:
                count = count + jnp.where(c >= tau, 1.0, 0.0)
            n0 = jnp.where(s0 == top0[a], count, n0)
        w0 = jnp.exp(s0 - top0[0]) / zsum
        for g in range(PEER_NKEYS // PEER_GROUP):
            keys = slice(g * PEER_GROUP, (g + 1) * PEER_GROUP)
            nw_ref[g, h, :PEER_GROUP, :] = n0[keys]
            nw_ref[g, h, PEER_GROUP:, :] = w0[keys]
        r1_ref[h] = pltpu.bitcast(rank1.astype(BF16), jnp.uint32)
        e1_ref[h] = pltpu.bitcast(jnp.exp(s1 - top1[0]).astype(BF16), jnp.uint32)
        return carry

    lax.fori_loop(0, PEER_HEADS, head, 0)


def _peer_route(q3, keys, tt=256):
    _, t, dh = q3.shape
    tt = min(tt, t)
    ngrp = PEER_NKEYS // PEER_GROUP
    nwspec = pl.BlockSpec((ngrp, PEER_HEADS, 2 * PEER_GROUP, tt), lambda i: (0, 0, 0, i))
    pspec = pl.BlockSpec((PEER_HEADS, PEER_NKEYS // 2, tt), lambda i: (0, 0, i))
    packed = jax.ShapeDtypeStruct((PEER_HEADS, PEER_NKEYS // 2, t), jnp.uint32)
    return pl.pallas_call(
        _peer_route_kernel,
        grid=(t // tt,),
        in_specs=[pl.BlockSpec((2 * PEER_HEADS, tt, dh), lambda i: (0, i, 0)),
                  pl.BlockSpec(keys.shape, lambda i: (0, 0, 0, 0))],
        out_specs=[nwspec, pspec, pspec],
        out_shape=[jax.ShapeDtypeStruct((ngrp, PEER_HEADS, 2 * PEER_GROUP, t), F32), packed, packed],
        compiler_params=_cparams("parallel"),
        name="peer_route",
    )(q3, keys)


PEER_ROWS = 16


def _peer_dense_kernel(xn_ref, u_ref, v_ref, nw_ref, r1_ref, e1_ref, o_ref, at_ref):
    e = pl.program_id(1)

    @pl.when(e == 0)
    def _():
        o_ref[...] = jnp.zeros_like(o_ref)

    pre = _nt_dot(u_ref[...], xn_ref[...])
    at_ref[...] = (0.5 * pre * (1.0 + lax.erf(pre * (1.0 / math.sqrt(2.0))))).astype(BF16)
    tt = at_ref.shape[1]
    zero = jnp.zeros((PEER_ROWS, LANES), BF16)
    for ii in range(PEER_GROUP):
        for l0 in range(0, tt, LANES):
            lanes = slice(l0, l0 + LANES)
            counts = [jnp.broadcast_to(nw_ref[0, h, ii:ii + 1, lanes], (PEER_ROWS, LANES)).astype(BF16)
                      for h in range(PEER_HEADS)]
            weights = [jnp.broadcast_to(nw_ref[0, h, PEER_GROUP + ii:PEER_GROUP + ii + 1, lanes],
                                        (PEER_ROWS, LANES)).astype(BF16) for h in range(PEER_HEADS)]
            for r0 in range(0, PEER_NKEYS, PEER_ROWS):
                words = slice(r0 // 2, (r0 + PEER_ROWS) // 2)
                gate = zero
                for h in range(PEER_HEADS):
                    rank = pltpu.bitcast(r1_ref[h, words, lanes], BF16)
                    e1 = pltpu.bitcast(e1_ref[h, words, lanes], BF16)
                    gate = gate + jnp.where(rank < counts[h], e1 * weights[h], zero)
                e0 = ii * PEER_NKEYS + r0
                at_ref[e0:e0 + PEER_ROWS, lanes] = at_ref[e0:e0 + PEER_ROWS, lanes] * gate
    o_ref[...] += lax.dot_general(at_ref[...], v_ref[...], (((0,), (0,)), ((), ())), preferred_element_type=F32)


def _peer_dense(xn, u, v, nw, r1, e1, tt=512):
    t, d = xn.shape
    n = u.shape[0]
    tt = min(tt, t)
    hspec = pl.BlockSpec((PEER_HEADS, PEER_NKEYS // 2, tt), lambda ti, e: (0, 0, ti))
    return pl.pallas_call(
        _peer_dense_kernel,
        grid=(t // tt, n // PEER_EB),
        in_specs=[
            pl.BlockSpec((tt, d), lambda ti, e: (ti, 0)),
            pl.BlockSpec((PEER_EB, d), lambda ti, e: (e, 0)),
            pl.BlockSpec((PEER_EB, d), lambda ti, e: (e, 0)),
            pl.BlockSpec((1, PEER_HEADS, 2 * PEER_GROUP, tt), lambda ti, e: (e, 0, 0, ti)),
            hspec,
            hspec,
        ],
        out_specs=pl.BlockSpec((tt, d), lambda ti, e: (ti, 0)),
        out_shape=jax.ShapeDtypeStruct((t, d), F32),
        scratch_shapes=[pltpu.VMEM((PEER_EB, tt), BF16)],
        compiler_params=_cparams("parallel", "arbitrary"),
        name="peer_dense",
    )(xn, u, v, nw, r1, e1)


PACK_TN = 1024
PACK_HALF = PACK_TN // 2


def _pack_kernel(a1_ref, a2_ref, a3_ref, base_ref, o_ref, *, n_aligned, lane_off):
    del base_ref
    g = pl.program_id(1)

    @pl.when(g < n_aligned)
    def _():
        o_ref[:, :PACK_HALF] = a1_ref[...].astype(o_ref.dtype)
        o_ref[:, PACK_HALF:] = a2_ref[...].astype(o_ref.dtype)

    @pl.when(g >= n_aligned)
    def _():
        x = jnp.concatenate([a1_ref[...], a2_ref[...], a3_ref[...]], axis=1)
        o_ref[...] = x[:, lane_off:lane_off + PACK_TN].astype(o_ref.dtype)


def _pack_w_in(w_in, d, tr=512):
    c2 = 3 * SB_W + MLA_Q_RANK
    c3 = c2 + MLA_KV_RANK + MLA_ROPE
    half = MLA_ROPE // 2
    n_cols = COL_GATE + N_BRANCH * d
    kr = w_in[:, c2 + MLA_KV_RANK:c3]
    z = lambda n: jnp.zeros((d, n), w_in.dtype)
    dkv = jnp.concatenate([
        w_in[:, c2:c2 + MLA_KV_RANK],
        kr, z(LANES - MLA_ROPE),
        kr[:, half:], kr[:, :half], z(LANES - MLA_ROPE),
        z(DKV_PAD - MLA_KV_RANK - 2 * LANES)], axis=1).astype(BF16)
    base = lax.dynamic_update_slice(jnp.zeros((d, n_cols), BF16), dkv, (0, COL_DKV))
    lane_off = c3 % LANES
    n_aligned = COL_DKV // PACK_TN
    first_shifted = (c3 - lane_off) // PACK_HALF
    assert COL_DKV % PACK_TN == 0 and (c3 - lane_off) % PACK_HALF == 0 and (n_cols - COL_MEMQ) % PACK_TN == 0
    n_blocks = n_aligned + (n_cols - COL_MEMQ) // PACK_TN

    def src(g):
        return jnp.where(g < n_aligned, 2 * g, first_shifted + 2 * (g - n_aligned))

    tr = min(tr, d)
    return pl.pallas_call(
        functools.partial(_pack_kernel, n_aligned=n_aligned, lane_off=lane_off),
        grid=(d // tr, n_blocks),
        in_specs=[
            pl.BlockSpec((tr, PACK_HALF), lambda i, g: (i, src(g))),
            pl.BlockSpec((tr, PACK_HALF), lambda i, g: (i, src(g) + 1)),
            pl.BlockSpec((tr, LANES), lambda i, g: (i, (src(g) + 2) * (PACK_HALF // LANES))),
            pl.BlockSpec(memory_space=pl.ANY),
        ],
        out_specs=pl.BlockSpec((tr, PACK_TN), lambda i, g: (i, jnp.where(g < n_aligned, g, g + 1))),
        out_shape=jax.ShapeDtypeStruct((d, n_cols), BF16),
        input_output_aliases={3: 0},
        compiler_params=_cparams("parallel", "parallel"),
        name="pack_w_in",
    )(w_in, w_in, w_in, base)


def _pack_mla_weights(w_uq, w_ukv):
    half = MLA_ROPE // 2
    rq = w_uq.shape[0]
    wq = w_uq.reshape(rq, MLA_HEADS, MLA_NOPE + MLA_ROPE)
    nope, t1, t2 = wq[:, :, :MLA_NOPE], wq[:, :, MLA_NOPE:MLA_NOPE + half], wq[:, :, MLA_NOPE + half:]
    zq = jnp.zeros((rq, MLA_HEADS, LANES - MLA_ROPE), w_uq.dtype)
    w_a = jnp.concatenate([nope, t1, t2, zq], axis=2).reshape(rq, MLA_HEADS * MLA_QK_PAD)
    w_b = jnp.concatenate([t2, t1, zq], axis=2).reshape(rq, MLA_HEADS * LANES)
    wab = jnp.concatenate([w_a, w_b], axis=1).astype(BF16)
    rkv = w_ukv.shape[0]
    wkv = w_ukv.reshape(rkv, MLA_HEADS, MLA_NOPE + MLA_DV)
    wkv = jnp.concatenate([wkv[:, :, :MLA_NOPE].reshape(rkv, -1), wkv[:, :, MLA_NOPE:].reshape(rkv, -1)], axis=1)
    return wab, wkv.astype(BF16)


def _rope_maps(positions):
    half = MLA_ROPE // 2
    freqs = ROPE_BASE ** (-jnp.arange(half, dtype=F32) / half)
    ang = positions.astype(F32).reshape(-1)[:, None] * freqs
    cos, sin = jnp.cos(ang), jnp.sin(ang)
    z = jnp.zeros((ang.shape[0], LANES - MLA_ROPE), F32)
    return jnp.concatenate([cos, cos, z], axis=1), jnp.concatenate([-sin, sin, z], axis=1)


def _layer(h, mem, cmap, smap, g_mix, w_in, mla_g_q, mla_w_uq, mla_g_kv, mla_w_ukv, g_mem, w_mem_kv,
           w_branch, w_out, g_ffn, peer_w_q, peer_sub_keys, peer_u, peer_v):
    b, s, d = h.shape
    t = b * s
    n_mem = mem.shape[1]
    h2 = h.reshape(t, d)

    xn = _rmsnorm(h2, g_mix, BF16)
    proj = _matmul(xn, _pack_w_in(w_in, d), BF16, tm=512, tn=1024, name="in_proj")
    proj3 = proj.reshape(b, s, -1)

    y_sb = _sb_attention(proj3).reshape(t, SB_W)

    wab, wkv = _pack_mla_weights(mla_w_uq, mla_w_ukv)
    qcat, kcat, v = _mla_prep(proj, cmap, smap, mla_g_q, mla_g_kv, wab, wkv)
    y_mla = _mla_attention(qcat.reshape(b, s, -1), kcat.reshape(b, s, -1), v.reshape(b, s, -1)).reshape(t, -1)

    mem_n = _rmsnorm(mem.reshape(b * n_mem, d), g_mem, BF16)
    mkv = _matmul(mem_n, w_mem_kv.astype(BF16), BF16, tm=512, tn=1024, name="mem_kv")
    y_mem = _mem_attention(proj3, mkv.reshape(b, n_mem, 2 * MEM_W)).reshape(t, MEM_W)

    merged = _merge(y_sb, y_mla, y_mem, proj, w_branch.astype(BF16), d)
    h1 = _matmul(merged, w_out.astype(BF16), F32, tm=512, tn=1024, residual=h2, name="out_proj")

    xn2 = _rmsnorm(h1, g_ffn, BF16)
    q3 = _matmul_split(xn2, peer_w_q.astype(BF16), F32, tm=512, tn=1024, name="peer_q")
    nw, r1, e1 = _peer_route(q3, peer_sub_keys.astype(BF16))
    y = _peer_dense(xn2, peer_u.astype(BF16), peer_v.astype(BF16), nw, r1, e1)
    return h1, y


def kernel(x, mem, positions, g_mix, w_in, mla_g_q, mla_w_uq, mla_g_kv, mla_w_ukv, g_mem, w_mem_kv, w_branch,
           w_out, g_ffn, peer_w_q, peer_sub_keys, peer_u, peer_v, g_final):
    b, s, d = x.shape
    cmap, smap = _rope_maps(positions)
    depth = w_in.shape[0]
    h = x
    for layer in range(depth):
        h1, y = _layer(h, mem, cmap, smap, g_mix[layer], w_in[layer], mla_g_q[layer], mla_w_uq[layer],
                       mla_g_kv[layer], mla_w_ukv[layer], g_mem[layer], w_mem_kv[layer], w_branch[layer],
                       w_out[layer], g_ffn[layer], peer_w_q[layer], peer_sub_keys[layer], peer_u[layer],
                       peer_v[layer])
        if layer + 1 < depth:
            h = (h1 + y).reshape(b, s, d)
    return _add_rmsnorm(h1, y, g_final).reshape(b, s, d)
```

```python
import functools
import math

import jax
import jax.numpy as jnp
from jax import lax
from jax.experimental import pallas as pl
from jax.experimental.pallas import tpu as pltpu

F32 = jnp.float32
BF16 = jnp.bfloat16

EPS = 1e-6
NEG = -1e30
LANES = 128

SB_HEADS = 16
SB_DH = 128
SB_W = SB_HEADS * SB_DH
MLA_HEADS = 16
MLA_Q_RANK = 1024
MLA_KV_RANK = 512
MLA_NOPE = 128
MLA_ROPE = 64
MLA_DV = 128
MLA_QK_PAD = 256
ROPE_BASE = 10000.0
MEM_HEADS = 4
MEM_DH = 512
MEM_W = MEM_HEADS * MEM_DH
N_BRANCH = 3
BRANCH_W = 2048
PEER_HEADS = 8
PEER_NKEYS = 128
PEER_DK = 256
PEER_TOPK = 16

COL_SB = 0
COL_DQ = 3 * SB_W
COL_DKV = COL_DQ + MLA_Q_RANK
DKV_PAD = 1024
COL_MEMQ = COL_DKV + DKV_PAD
COL_GATE = COL_MEMQ + MEM_W

VMEM_LIMIT_BYTES = 56 * 1024 * 1024


def _cparams(*sem):
    return pltpu.CompilerParams(dimension_semantics=sem, vmem_limit_bytes=VMEM_LIMIT_BYTES)


def _nt_dot(a, b):
    return lax.dot_general(a, b, (((1,), (1,)), ((), ())), preferred_element_type=F32)


def _rmsnorm_kernel(x_ref, g_ref, o_ref):
    x = x_ref[...].astype(F32)
    ms = jnp.mean(x * x, axis=-1, keepdims=True)
    o_ref[...] = (x * lax.rsqrt(ms + EPS) * g_ref[...]).astype(o_ref.dtype)


def _rmsnorm(x, g, out_dtype, tm=256):
    m, d = x.shape
    tm = min(tm, m)
    return pl.pallas_call(
        _rmsnorm_kernel,
        grid=(m // tm,),
        in_specs=[pl.BlockSpec((tm, d), lambda i: (i, 0)), pl.BlockSpec((1, d), lambda i: (0, 0))],
        out_specs=pl.BlockSpec((tm, d), lambda i: (i, 0)),
        out_shape=jax.ShapeDtypeStruct((m, d), out_dtype),
        compiler_params=_cparams("parallel"),
        name="rmsnorm",
    )(x, g.reshape(1, d).astype(F32))


def _add_rmsnorm_kernel(a_ref, b_ref, g_ref, o_ref):
    x = a_ref[...] + b_ref[...]
    ms = jnp.mean(x * x, axis=-1, keepdims=True)
    o_ref[...] = (x * lax.rsqrt(ms + EPS) * g_ref[...]).astype(o_ref.dtype)


def _add_rmsnorm(a, b, g, tm=256):
    m, d = a.shape
    tm = min(tm, m)
    return pl.pallas_call(
        _add_rmsnorm_kernel,
        grid=(m // tm,),
        in_specs=[pl.BlockSpec((tm, d), lambda i: (i, 0)), pl.BlockSpec((tm, d), lambda i: (i, 0)),
                  pl.BlockSpec((1, d), lambda i: (0, 0))],
        out_specs=pl.BlockSpec((tm, d), lambda i: (i, 0)),
        out_shape=jax.ShapeDtypeStruct((m, d), F32),
        compiler_params=_cparams("parallel"),
        name="add_rmsnorm",
    )(a, b, g.reshape(1, d).astype(F32))


def _mm_kernel(a_ref, b_ref, o_ref):
    o_ref[...] = jnp.dot(a_ref[...], b_ref[...], preferred_element_type=F32).astype(o_ref.dtype)


def _mm_res_kernel(a_ref, b_ref, r_ref, o_ref):
    acc = jnp.dot(a_ref[...], b_ref[...], preferred_element_type=F32)
    o_ref[...] = (acc + r_ref[...]).astype(o_ref.dtype)


def _mm_split_kernel(a_ref, b_ref, o_ref):
    acc = jnp.dot(a_ref[...], b_ref[...], preferred_element_type=F32)
    for c in range(o_ref.shape[0]):
        o_ref[c] = acc[:, c * LANES:(c + 1) * LANES].astype(o_ref.dtype)


def _matmul_split(a, b, out_dtype, tm, tn, name):
    m, k = a.shape
    _, n = b.shape
    tm, tn = min(tm, m), min(tn, n)
    return pl.pallas_call(
        _mm_split_kernel,
        grid=(n // tn, m // tm),
        in_specs=[pl.BlockSpec((tm, k), lambda j, i: (i, 0)), pl.BlockSpec((k, tn), lambda j, i: (0, j))],
        out_specs=pl.BlockSpec((tn // LANES, tm, LANES), lambda j, i: (j, i, 0)),
        out_shape=jax.ShapeDtypeStruct((n // LANES, m, LANES), out_dtype),
        compiler_params=_cparams("parallel", "parallel"),
        name=name,
    )(a, b)


def _matmul(a, b, out_dtype, tm, tn, residual=None, name="matmul"):
    m, k = a.shape
    _, n = b.shape
    tm, tn = min(tm, m), min(tn, n)
    in_specs = [pl.BlockSpec((tm, k), lambda j, i: (i, 0)), pl.BlockSpec((k, tn), lambda j, i: (0, j))]
    args = [a, b]
    body = _mm_kernel
    if residual is not None:
        in_specs.append(pl.BlockSpec((tm, tn), lambda j, i: (i, j)))
        args.append(residual)
        body = _mm_res_kernel
    return pl.pallas_call(
        body,
        grid=(n // tn, m // tm),
        in_specs=in_specs,
        out_specs=pl.BlockSpec((tm, tn), lambda j, i: (i, j)),
        out_shape=jax.ShapeDtypeStruct((m, n), out_dtype),
        compiler_params=_cparams("parallel", "parallel"),
        name=name,
    )(*args)


SB_BQ = 256
SB_BK = 128
SB_HG = 8
SB_STOP = -105.0


def _sb_kernel(q_ref, k_ref, v_ref, o_ref, *, scale):
    qi = pl.program_id(2)
    bq = q_ref.shape[1]
    qs = [(q_ref[0, :, h * SB_DH:(h + 1) * SB_DH].astype(F32) * scale).astype(BF16) for h in range(SB_HG)]
    r = lax.broadcasted_iota(jnp.int32, (SB_BK, 2 * SB_BK), 0)
    c = lax.broadcasted_iota(jnp.int32, (SB_BK, 2 * SB_BK), 1)
    u_ext = jnp.where((c >= SB_BK) | (r > c), 1.0, 0.0).astype(BF16)
    qpos = qi * bq + lax.broadcasted_iota(jnp.int32, (bq, SB_BK), 0)
    kofs = lax.broadcasted_iota(jnp.int32, (bq, SB_BK), 1)

    def block(kb, runs, accs, masked):
        start = pl.multiple_of(kb * SB_BK, SB_BK)
        if masked:
            strict = (kb * SB_BK + kofs) < qpos
        heads = range(SB_HG)
        zs = [_nt_dot(qs[h], k_ref[0, pl.ds(start, SB_BK), h * SB_DH:(h + 1) * SB_DH]) for h in heads]
        lks = [-(jnp.maximum(z, 0.0) + jnp.log(1.0 + jnp.exp(-jnp.abs(z)))) for z in zs]
        lkms = [jnp.where(strict, lk, 0.0) for lk in lks] if masked else lks
        his = [lkm.astype(BF16) for lkm in lkms]
        los = [(lkm - hi.astype(F32)).astype(BF16) for lkm, hi in zip(lkms, his)]
        css = [jnp.dot(hi, u_ext, preferred_element_type=F32) + jnp.dot(lo, u_ext, preferred_element_type=F32)
               for hi, lo in zip(his, los)]
        ws = [jnp.exp(zs[h] + lks[h] + runs[h] + css[h][:, :SB_BK]) for h in heads]
        if masked:
            ws = [jnp.where(strict, w, 0.0) for w in ws]
        new_accs = [accs[h] + jnp.dot(ws[h].astype(BF16), v_ref[0, pl.ds(start, SB_BK), h * SB_DH:(h + 1) * SB_DH],
                                      preferred_element_type=F32) for h in heads]
        new_runs = [runs[h] + css[h][:, SB_BK:] for h in heads]
        return tuple(new_runs), tuple(new_accs)

    def least_decayed(runs):
        mx = jnp.max(runs[0])
        for h in range(1, SB_HG):
            mx = jnp.maximum(mx, jnp.max(runs[h]))
        return mx

    runs = tuple(jnp.zeros((bq, SB_BK), F32) for _ in range(SB_HG))
    accs = tuple(jnp.zeros((bq, SB_DH), F32) for _ in range(SB_HG))
    nd = bq // SB_BK
    for d in range(nd):
        runs, accs = block(qi * nd + (nd - 1 - d), runs, accs, True)

    def cond(state):
        kb, mx, _, _ = state
        return jnp.logical_and(kb >= 0, mx > SB_STOP)

    def body(state):
        kb, _, rs, ac = state
        rs, ac = block(kb, rs, ac, False)
        return kb - 1, least_decayed(rs), rs, ac

    _, _, _, accs = lax.while_loop(cond, body, (qi * nd - 1, least_decayed(runs), runs, accs))
    for h in range(SB_HG):
        o_ref[0, :, h * SB_DH:(h + 1) * SB_DH] = accs[h].astype(o_ref.dtype)


def _sb_attention(proj3):
    b, s, _ = proj3.shape
    bq = min(SB_BQ, s)
    ng = SB_HEADS // SB_HG
    wg = SB_HG * SB_DH
    return pl.pallas_call(
        functools.partial(_sb_kernel, scale=1.0 / math.sqrt(SB_DH)),
        grid=(b, ng, s // bq),
        in_specs=[
            pl.BlockSpec((1, bq, wg), lambda bi, g, qi: (bi, qi, g)),
            pl.BlockSpec((1, s, wg), lambda bi, g, qi: (bi, 0, ng + g), pipeline_mode=pl.Buffered(1)),
            pl.BlockSpec((1, s, wg), lambda bi, g, qi: (bi, 0, 2 * ng + g), pipeline_mode=pl.Buffered(1)),
        ],
        out_specs=pl.BlockSpec((1, bq, wg), lambda bi, g, qi: (bi, qi, g)),
        out_shape=jax.ShapeDtypeStruct((b, s, SB_W), BF16),
        compiler_params=_cparams("parallel", "parallel", "arbitrary"),
        name="sb_attention",
    )(proj3, proj3, proj3)


def _mla_prep_kernel(dq_ref, dkv_ref, c_ref, s_ref, gq_ref, gkv_ref, wab_ref, wkv_ref,
                     qcat_ref, kcat_ref, v_ref, *, scale):
    cmap = c_ref[...]
    smap = s_ref[...]
    dq = dq_ref[...].astype(F32)
    cq = dq * lax.rsqrt(jnp.mean(dq * dq, axis=-1, keepdims=True) + EPS) * gq_ref[...]
    qab = jnp.dot(cq.astype(BF16), wab_ref[...], preferred_element_type=F32)
    swap0 = MLA_HEADS * MLA_QK_PAD
    for h in range(MLA_HEADS):
        lo = h * MLA_QK_PAD
        qcat_ref[:, lo:lo + LANES] = (qab[:, lo:lo + LANES] * scale).astype(BF16)
        rope = qab[:, lo + LANES:lo + 2 * LANES] * cmap + qab[:, swap0 + h * LANES:swap0 + (h + 1) * LANES] * smap
        qcat_ref[:, lo + LANES:lo + 2 * LANES] = (rope * scale).astype(BF16)
    dkv = dkv_ref[...].astype(F32)
    ckv_in = dkv[:, :MLA_KV_RANK]
    ckv = ckv_in * lax.rsqrt(jnp.mean(ckv_in * ckv_in, axis=-1, keepdims=True) + EPS) * gkv_ref[...]
    kv = jnp.dot(ckv.astype(BF16), wkv_ref[...], preferred_element_type=F32)
    kr = (dkv[:, MLA_KV_RANK:MLA_KV_RANK + LANES] * cmap
          + dkv[:, MLA_KV_RANK + LANES:MLA_KV_RANK + 2 * LANES] * smap).astype(BF16)
    for h in range(MLA_HEADS):
        lo = h * MLA_QK_PAD
        kcat_ref[:, lo:lo + LANES] = kv[:, h * LANES:(h + 1) * LANES].astype(BF16)
        kcat_ref[:, lo + LANES:lo + 2 * LANES] = kr
    v_ref[...] = kv[:, MLA_HEADS * MLA_NOPE:].astype(BF16)


def _mla_prep(proj, cmap, smap, gq, gkv, wab, wkv, tt=256):
    t = proj.shape[0]
    tt = min(tt, t)
    wq = MLA_HEADS * MLA_QK_PAD
    wv = MLA_HEADS * MLA_DV
    return pl.pallas_call(
        functools.partial(_mla_prep_kernel, scale=math.log2(math.e) / math.sqrt(MLA_NOPE + MLA_ROPE)),
        grid=(t // tt,),
        in_specs=[
            pl.BlockSpec((tt, MLA_Q_RANK), lambda i: (i, COL_DQ // MLA_Q_RANK)),
            pl.BlockSpec((tt, DKV_PAD), lambda i: (i, COL_DKV // DKV_PAD)),
            pl.BlockSpec((tt, LANES), lambda i: (i, 0)),
            pl.BlockSpec((tt, LANES), lambda i: (i, 0)),
            pl.BlockSpec((1, MLA_Q_RANK), lambda i: (0, 0)),
            pl.BlockSpec((1, MLA_KV_RANK), lambda i: (0, 0)),
            pl.BlockSpec(wab.shape, lambda i: (0, 0)),
            pl.BlockSpec(wkv.shape, lambda i: (0, 0)),
        ],
        out_specs=[
            pl.BlockSpec((tt, wq), lambda i: (i, 0)),
            pl.BlockSpec((tt, wq), lambda i: (i, 0)),
            pl.BlockSpec((tt, wv), lambda i: (i, 0)),
        ],
        out_shape=[
            jax.ShapeDtypeStruct((t, wq), BF16),
            jax.ShapeDtypeStruct((t, wq), BF16),
            jax.ShapeDtypeStruct((t, wv), BF16),
        ],
        compiler_params=_cparams("parallel"),
        name="mla_prep",
    )(proj, proj, cmap, smap, gq.reshape(1, -1).astype(F32), gkv.reshape(1, -1).astype(F32), wab, wkv)


MLA_BQ = 512
MLA_BK = 512
MLA_HG = 4


def _mla_kernel(q_ref, k_ref, v_ref, o_ref):
    qi = pl.program_id(2)
    bq = q_ref.shape[1]
    bk = min(MLA_BK, bq)
    heads = range(MLA_HG)
    qs = [q_ref[0, :, h * MLA_QK_PAD:(h + 1) * MLA_QK_PAD] for h in heads]
    kpos = lax.broadcasted_iota(jnp.int32, (bk, bq), 0)
    qpos = qi * bq + lax.broadcasted_iota(jnp.int32, (bk, bq), 1)

    def block(kb, ms, ls, accs, masked):
        start = pl.multiple_of(kb * bk, bk)
        ss = [_nt_dot(k_ref[0, pl.ds(start, bk), h * MLA_QK_PAD:(h + 1) * MLA_QK_PAD], qs[h]) for h in heads]
        if masked:
            allowed = (kb * bk + kpos) <= qpos
            ss = [jnp.where(allowed, s, NEG) for s in ss]
        new_ms = [jnp.maximum(ms[h], jnp.max(ss[h], axis=0, keepdims=True)) for h in heads]
        ps = [jnp.exp2(ss[h] - new_ms[h]) for h in heads]
        alphas = [jnp.exp2(ms[h] - new_ms[h]) for h in heads]
        new_ls = [alphas[h] * ls[h] + jnp.sum(ps[h], axis=0, keepdims=True) for h in heads]
        pvs = [lax.dot_general(v_ref[0, pl.ds(start, bk), h * MLA_DV:(h + 1) * MLA_DV], ps[h].astype(BF16),
                               (((0,), (0,)), ((), ())), preferred_element_type=F32) for h in heads]
        new_accs = [alphas[h] * accs[h] + pvs[h] for h in heads]
        return tuple(new_ms), tuple(new_ls), tuple(new_accs)

    ms = tuple(jnp.full((1, bq), NEG, F32) for _ in heads)
    ls = tuple(jnp.zeros((1, bq), F32) for _ in heads)
    accs = tuple(jnp.zeros((MLA_DV, bq), F32) for _ in heads)
    nd = bq // bk
    ms, ls, accs = lax.fori_loop(0, qi * nd, lambda kb, cr: block(kb, cr[0], cr[1], cr[2], False), (ms, ls, accs))
    for d in range(nd):
        ms, ls, accs = block(qi * nd + d, ms, ls, accs, True)
    for h in heads:
        o_ref[0, :, h * MLA_DV:(h + 1) * MLA_DV] = (accs[h] / ls[h]).T.astype(o_ref.dtype)


def _mla_attention(qcat3, kcat3, v3):
    b, s, _ = qcat3.shape
    bq = min(MLA_BQ, s)
    wq = MLA_HG * MLA_QK_PAD
    wv = MLA_HG * MLA_DV
    return pl.pallas_call(
        _mla_kernel,
        grid=(b, MLA_HEADS // MLA_HG, s // bq),
        in_specs=[
            pl.BlockSpec((1, bq, wq), lambda bi, g, qi: (bi, qi, g)),
            pl.BlockSpec((1, s, wq), lambda bi, g, qi: (bi, 0, g), pipeline_mode=pl.Buffered(1)),
            pl.BlockSpec((1, s, wv), lambda bi, g, qi: (bi, 0, g), pipeline_mode=pl.Buffered(1)),
        ],
        out_specs=pl.BlockSpec((1, bq, wv), lambda bi, g, qi: (bi, qi, g)),
        out_shape=jax.ShapeDtypeStruct((b, s, MLA_HEADS * MLA_DV), BF16),
        compiler_params=_cparams("parallel", "parallel", "arbitrary"),
        name="mla_attention",
    )(qcat3, kcat3, v3)


def _mem_kernel(q_ref, k_ref, v_ref, o_ref, *, scale):
    s = _nt_dot(q_ref[0], k_ref[0]) * scale
    m = jnp.max(s, axis=-1, keepdims=True)
    p = jnp.exp(s - m)
    p = p / jnp.sum(p, axis=-1, keepdims=True)
    o_ref[0] = jnp.dot(p.astype(BF16), v_ref[0], preferred_element_type=F32).astype(o_ref.dtype)


def _mem_attention(proj3, mkv3, tt=512):
    b, s, _ = proj3.shape
    n_mem = mkv3.shape[1]
    tt = min(tt, s)
    q0 = COL_MEMQ // MEM_DH
    return pl.pallas_call(
        functools.partial(_mem_kernel, scale=1.0 / math.sqrt(MEM_DH)),
        grid=(b, s // tt, MEM_HEADS),
        in_specs=[
            pl.BlockSpec((1, tt, MEM_DH), lambda bi, ti, h: (bi, ti, q0 + h)),
            pl.BlockSpec((1, n_mem, MEM_DH), lambda bi, ti, h: (bi, 0, h)),
            pl.BlockSpec((1, n_mem, MEM_DH), lambda bi, ti, h: (bi, 0, MEM_HEADS + h)),
        ],
        out_specs=pl.BlockSpec((1, tt, MEM_DH), lambda bi, ti, h: (bi, ti, h)),
        out_shape=jax.ShapeDtypeStruct((b, s, MEM_W), BF16),
        compiler_params=_cparams("parallel", "parallel", "parallel"),
        name="mem_attention",
    )(proj3, mkv3, mkv3)


def _merge_kernel(ya_ref, yb_ref, yc_ref, ga_ref, gb_ref, gc_ref, w_ref, o_ref):
    acc = jax.nn.sigmoid(ga_ref[...].astype(F32)) * jnp.dot(ya_ref[...], w_ref[0], preferred_element_type=F32)
    acc += jax.nn.sigmoid(gb_ref[...].astype(F32)) * jnp.dot(yb_ref[...], w_ref[1], preferred_element_type=F32)
    acc += jax.nn.sigmoid(gc_ref[...].astype(F32)) * jnp.dot(yc_ref[...], w_ref[2], preferred_element_type=F32)
    o_ref[...] = acc.astype(o_ref.dtype)


def _merge(y_sb, y_mla, y_mem, proj, wb, d, tm=1024, tn=512):
    t = y_sb.shape[0]
    tm, tn = min(tm, t), min(tn, d)
    g0 = COL_GATE // tn
    gstep = d // tn

    def gate_spec(br):
        return pl.BlockSpec((tm, tn), lambda j, i: (i, g0 + br * gstep + j))

    yspec = pl.BlockSpec((tm, BRANCH_W), lambda j, i: (i, 0))
    return pl.pallas_call(
        _merge_kernel,
        grid=(d // tn, t // tm),
        in_specs=[yspec, yspec, yspec, gate_spec(0), gate_spec(1), gate_spec(2),
                  pl.BlockSpec((N_BRANCH, BRANCH_W, tn), lambda j, i: (0, 0, j))],
        out_specs=pl.BlockSpec((tm, tn), lambda j, i: (i, j)),
        out_shape=jax.ShapeDtypeStruct((t, d), BF16),
        compiler_params=_cparams("parallel", "parallel"),
        name="branch_merge",
    )(y_sb, y_mla, y_mem, proj, proj, proj, wb)


PEER_EB = 512
PEER_GROUP = PEER_EB // PEER_NKEYS
PEER_RANK_NONE = 255.0
PEER_CAND_WIDTH = [PEER_TOPK // (a + 1) for a in range(PEER_TOPK)]


def _take_top(vals, n, with_rank=False):
    out = []
    rem = vals
    rank = jnp.full(vals.shape, PEER_RANK_NONE, F32) if with_rank else None
    for k in range(n):
        mx = jnp.max(rem, axis=0, keepdims=True)
        out.append(mx)
        hit = rem == mx
        if with_rank:
            rank = jnp.where(hit, float(k), rank)
        rem = jnp.where(hit, -jnp.inf, rem)
    return (out, rank) if with_rank else out


def _peer_route_kernel(q_ref, keys_ref, nw_ref, r1_ref, e1_ref):
    def head(h, carry):
        s0 = _nt_dot(keys_ref[h, 0], q_ref[2 * h].astype(BF16))
        s1 = _nt_dot(keys_ref[h, 1], q_ref[2 * h + 1].astype(BF16))
        top0 = _take_top(s0, PEER_TOPK)
        top1, rank1 = _take_top(s1, PEER_TOPK, with_rank=True)
        cand = [[top0[a] + top1[b] for b in range(PEER_CAND_WIDTH[a])] for a in range(PEER_TOPK)]
        flat = [c for row in cand for c in row]
        pad = -len(flat) % 8
        stacked = jnp.concatenate(flat + [jnp.full_like(flat[0], -jnp.inf)] * pad, axis=0)
        ctop = _take_top(stacked, PEER_TOPK)
        cmax, tau = ctop[0], ctop[PEER_TOPK - 1]
        zsum = jnp.zeros_like(cmax)
        for cv in ctop:
            zsum = zsum + jnp.exp(cv - cmax)
        n0 = jnp.zeros_like(s0)
        for a in range(PEER_TOPK):
            count = jnp.zeros_like(tau)
            for c in cand[a]:
                count = count + jnp.where(c >= tau, 1.0, 0.0)
            n0 = jnp.where(s0 == top0[a], count, n0)
        w0 = jnp.exp(s0 - top0[0]) / zsum
        for g in range(PEER_NKEYS // PEER_GROUP):
            keys = slice(g * PEER_GROUP, (g + 1) * PEER_GROUP)
            nw_ref[g, h, :PEER_GROUP, :] = n0[keys]
            nw_ref[g, h, PEER_GROUP:, :] = w0[keys]
        r1_ref[h] = pltpu.bitcast(rank1.astype(BF16), jnp.uint32)
        e1_ref[h] = pltpu.bitcast(jnp.exp(s1 - top1[0]).astype(BF16), jnp.uint32)
        return carry

    lax.fori_loop(0, PEER_HEADS, head, 0)


def _peer_route(q3, keys, tt=256):
    _, t, dh = q3.shape
    tt = min(tt, t)
    ngrp = PEER_NKEYS // PEER_GROUP
    nwspec = pl.BlockSpec((ngrp, PEER_HEADS, 2 * PEER_GROUP, tt), lambda i: (0, 0, 0, i))
    pspec = pl.BlockSpec((PEER_HEADS, PEER_NKEYS // 2, tt), lambda i: (0, 0, i))
    packed = jax.ShapeDtypeStruct((PEER_HEADS, PEER_NKEYS // 2, t), jnp.uint32)
    return pl.pallas_call(
        _peer_route_kernel,
        grid=(t // tt,),
        in_specs=[pl.BlockSpec((2 * PEER_HEADS, tt, dh), lambda i: (0, i, 0)),
                  pl.BlockSpec(keys.shape, lambda i: (0, 0, 0, 0))],
        out_specs=[nwspec, pspec, pspec],
        out_shape=[jax.ShapeDtypeStruct((ngrp, PEER_HEADS, 2 * PEER_GROUP, t), F32), packed, packed],
        compiler_params=_cparams("parallel"),
        name="peer_route",
    )(q3, keys)


PEER_ROWS = 16


def _peer_dense_kernel(xn_ref, u_ref, v_ref, nw_ref, r1_ref, e1_ref, o_ref, at_ref):
    e = pl.program_id(1)

    @pl.when(e == 0)
    def _():
        o_ref[...] = jnp.zeros_like(o_ref)

    pre = _nt_dot(u_ref[...], xn_ref[...])
    at_ref[...] = (0.5 * pre * (1.0 + lax.erf(pre * (1.0 / math.sqrt(2.0))))).astype(BF16)
    tt = at_ref.shape[1]
    zero = jnp.zeros((PEER_ROWS, LANES), BF16)
    for ii in range(PEER_GROUP):
        for l0 in range(0, tt, LANES):
            lanes = slice(l0, l0 + LANES)
            counts = [jnp.broadcast_to(nw_ref[0, h, ii:ii + 1, lanes], (PEER_ROWS, LANES)).astype(BF16)
                      for h in range(PEER_HEADS)]
            weights = [jnp.broadcast_to(nw_ref[0, h, PEER_GROUP + ii:PEER_GROUP + ii + 1, lanes],
                                        (PEER_ROWS, LANES)).astype(BF16) for h in range(PEER_HEADS)]
            for r0 in range(0, PEER_NKEYS, PEER_ROWS):
                words = slice(r0 // 2, (r0 + PEER_ROWS) // 2)
                gate = zero
                for h in range(PEER_HEADS):
                    rank = pltpu.bitcast(r1_ref[h, words, lanes], BF16)
                    e1 = pltpu.bitcast(e1_ref[h, words, lanes], BF16)
                    gate = gate + jnp.where(rank < counts[h], e1 * weights[h], zero)
                e0 = ii * PEER_NKEYS + r0
                at_ref[e0:e0 + PEER_ROWS, lanes] = at_ref[e0:e0 + PEER_ROWS, lanes] * gate
    o_ref[...] += lax.dot_general(at_ref[...], v_ref[...], (((0,), (0,)), ((), ())), preferred_element_type=F32)


def _peer_dense(xn, u, v, nw, r1, e1, tt=512):
    t, d = xn.shape
    n = u.shape[0]
    tt = min(tt, t)
    hspec = pl.BlockSpec((PEER_HEADS, PEER_NKEYS // 2, tt), lambda ti, e: (0, 0, ti))
    return pl.pallas_call(
        _peer_dense_kernel,
        grid=(t // tt, n // PEER_EB),
        in_specs=[
            pl.BlockSpec((tt, d), lambda ti, e: (ti, 0)),
            pl.BlockSpec((PEER_EB, d), lambda ti, e: (e, 0)),
            pl.BlockSpec((PEER_EB, d), lambda ti, e: (e, 0)),
            pl.BlockSpec((1, PEER_HEADS, 2 * PEER_GROUP, tt), lambda ti, e: (e, 0, 0, ti)),
            hspec,
            hspec,
        ],
        out_specs=pl.BlockSpec((tt, d), lambda ti, e: (ti, 0)),
        out_shape=jax.ShapeDtypeStruct((t, d), F32),
        scratch_shapes=[pltpu.VMEM((PEER_EB, tt), BF16)],
        compiler_params=_cparams("parallel", "arbitrary"),
        name="peer_dense",
    )(xn, u, v, nw, r1, e1)


PACK_TN = 1024
PACK_HALF = PACK_TN // 2


def _pack_kernel(a1_ref, a2_ref, a3_ref, dkv_ref, o_ref, *, n_aligned, lane_off):
    g = pl.program_id(1)

    @pl.when(g < n_aligned)
    def _():
        o_ref[:, :PACK_HALF] = a1_ref[...].astype(o_ref.dtype)
        o_ref[:, PACK_HALF:] = a2_ref[...].astype(o_ref.dtype)

    @pl.when(g == n_aligned)
    def _():
        o_ref[...] = dkv_ref[...]

    @pl.when(g > n_aligned)
    def _():
        x = jnp.concatenate([a1_ref[...], a2_ref[...], a3_ref[...]], axis=1)
        o_ref[...] = x[:, lane_off:lane_off + PACK_TN].astype(o_ref.dtype)


def _pack_w_in(w_in, d, tr=512):
    c2 = 3 * SB_W + MLA_Q_RANK
    c3 = c2 + MLA_KV_RANK + MLA_ROPE
    half = MLA_ROPE // 2
    n_cols = COL_GATE + N_BRANCH * d
    kr = w_in[:, c2 + MLA_KV_RANK:c3]
    z = lambda n: jnp.zeros((d, n), w_in.dtype)
    dkv = jnp.concatenate([
        w_in[:, c2:c2 + MLA_KV_RANK],
        kr, z(LANES - MLA_ROPE),
        kr[:, half:], kr[:, :half], z(LANES - MLA_ROPE),
        z(DKV_PAD - MLA_KV_RANK - 2 * LANES)], axis=1).astype(BF16)
    lane_off = c3 % LANES
    n_aligned = COL_DKV // PACK_TN
    first_shifted = (c3 - lane_off) // PACK_HALF
    assert DKV_PAD == PACK_TN and COL_DKV % PACK_TN == 0 and (c3 - lane_off) % PACK_HALF == 0
    assert (n_cols - COL_MEMQ) % PACK_TN == 0

    def src(g):
        return jnp.where(g <= n_aligned, 2 * jnp.minimum(g, n_aligned - 1), first_shifted + 2 * (g - n_aligned - 1))

    tr = min(tr, d)
    return pl.pallas_call(
        functools.partial(_pack_kernel, n_aligned=n_aligned, lane_off=lane_off),
        grid=(d // tr, n_cols // PACK_TN),
        in_specs=[
            pl.BlockSpec((tr, PACK_HALF), lambda i, g: (i, src(g))),
            pl.BlockSpec((tr, PACK_HALF), lambda i, g: (i, src(g) + 1)),
            pl.BlockSpec((tr, LANES), lambda i, g: (i, (src(g) + 2) * (PACK_HALF // LANES))),
            pl.BlockSpec((tr, PACK_TN), lambda i, g: (i, 0)),
        ],
        out_specs=pl.BlockSpec((tr, PACK_TN), lambda i, g: (i, g)),
        out_shape=jax.ShapeDtypeStruct((d, n_cols), BF16),
        compiler_params=_cparams("parallel", "parallel"),
        name="pack_w_in",
    )(w_in, w_in, w_in, dkv)


def _pack_mla_weights(w_uq, w_ukv):
    half = MLA_ROPE // 2
    rq = w_uq.shape[0]
    wq = w_uq.reshape(rq, MLA_HEADS, MLA_NOPE + MLA_ROPE)
    nope, t1, t2 = wq[:, :, :MLA_NOPE], wq[:, :, MLA_NOPE:MLA_NOPE + half], wq[:, :, MLA_NOPE + half:]
    zq = jnp.zeros((rq, MLA_HEADS, LANES - MLA_ROPE), w_uq.dtype)
    w_a = jnp.concatenate([nope, t1, t2, zq], axis=2).reshape(rq, MLA_HEADS * MLA_QK_PAD)
    w_b = jnp.concatenate([t2, t1, zq], axis=2).reshape(rq, MLA_HEADS * LANES)
    wab = jnp.concatenate([w_a, w_b], axis=1).astype(BF16)
    rkv = w_ukv.shape[0]
    wkv = w_ukv.reshape(rkv, MLA_HEADS, MLA_NOPE + MLA_DV)
    wkv = jnp.concatenate([wkv[:, :, :MLA_NOPE].reshape(rkv, -1), wkv[:, :, MLA_NOPE:].reshape(rkv, -1)], axis=1)
    return wab, wkv.astype(BF16)


def _rope_maps(positions):
    half = MLA_ROPE // 2
    freqs = ROPE_BASE ** (-jnp.arange(half, dtype=F32) / half)
    ang = positions.astype(F32).reshape(-1)[:, None] * freqs
    cos, sin = jnp.cos(ang), jnp.sin(ang)
    z = jnp.zeros((ang.shape[0], LANES - MLA_ROPE), F32)
    return jnp.concatenate([cos, cos, z], axis=1), jnp.concatenate([-sin, sin, z], axis=1)


def _layer(h, mem, cmap, smap, g_mix, w_in, mla_g_q, mla_w_uq, mla_g_kv, mla_w_ukv, g_mem, w_mem_kv,
           w_branch, w_out, g_ffn, peer_w_q, peer_sub_keys, peer_u, peer_v):
    b, s, d = h.shape
    t = b * s
    n_mem = mem.shape[1]
    h2 = h.reshape(t, d)

    xn = _rmsnorm(h2, g_mix, BF16)
    proj = _matmul(xn, _pack_w_in(w_in, d), BF16, tm=512, tn=1024, name="in_proj")
    proj3 = proj.reshape(b, s, -1)

    y_sb = _sb_attention(proj3).reshape(t, SB_W)

    wab, wkv = _pack_mla_weights(mla_w_uq, mla_w_ukv)
    qcat, kcat, v = _mla_prep(proj, cmap, smap, mla_g_q, mla_g_kv, wab, wkv)
    y_mla = _mla_attention(qcat.reshape(b, s, -1), kcat.reshape(b, s, -1), v.reshape(b, s, -1)).reshape(t, -1)

    mem_n = _rmsnorm(mem.reshape(b * n_mem, d), g_mem, BF16)
    mkv = _matmul(mem_n, w_mem_kv.astype(BF16), BF16, tm=512, tn=1024, name="mem_kv")
    y_mem = _mem_attention(proj3, mkv.reshape(b, n_mem, 2 * MEM_W)).reshape(t, MEM_W)

    merged = _merge(y_sb, y_mla, y_mem, proj, w_branch.astype(BF16), d)
    h1 = _matmul(merged, w_out.astype(BF16), F32, tm=512, tn=1024, residual=h2, name="out_proj")

    xn2 = _rmsnorm(h1, g_ffn, BF16)
    q3 = _matmul_split(xn2, peer_w_q.astype(BF16), F32, tm=512, tn=1024, name="peer_q")
    nw, r1, e1 = _peer_route(q3, peer_sub_keys.astype(BF16))
    y = _peer_dense(xn2, peer_u.astype(BF16), peer_v.astype(BF16), nw, r1, e1)
    return h1, y


def kernel(x, mem, positions, g_mix, w_in, mla_g_q, mla_w_uq, mla_g_kv, mla_w_ukv, g_mem, w_mem_kv, w_branch,
           w_out, g_ffn, peer_w_q, peer_sub_keys, peer_u, peer_v, g_final):
    b, s, d = x.shape
    cmap, smap = _rope_maps(positions)
    depth = w_in.shape[0]
    h = x
    for layer in range(depth):
        h1, y = _layer(h, mem, cmap, smap, g_mix[layer], w_in[layer], mla_g_q[layer], mla_w_uq[layer],
                       mla_g_kv[layer], mla_w_ukv[layer], g_mem[layer], w_mem_kv[layer], w_branch[layer],
                       w_out[layer], g_ffn[layer], peer_w_q[layer], peer_sub_keys[layer], peer_u[layer],
                       peer_v[layer])
        if layer + 1 < depth:
            h = (h1 + y).reshape(b, s, d)
    return _add_rmsnorm(h1, y, g_final).reshape(b, s, d)
```

```python
import functools
import math

import jax
import jax.numpy as jnp
from jax import lax
from jax.experimental import pallas as pl
from jax.experimental.pallas import tpu as pltpu

F32 = jnp.float32
BF16 = jnp.bfloat16

EPS = 1e-6
NEG = -1e30
LANES = 128

SB_HEADS = 16
SB_DH = 128
SB_W = SB_HEADS * SB_DH
MLA_HEADS = 16
MLA_Q_RANK = 1024
MLA_KV_RANK = 512
MLA_NOPE = 128
MLA_ROPE = 64
MLA_DV = 128
MLA_QK_PAD = 256
ROPE_BASE = 10000.0
MEM_HEADS = 4
MEM_DH = 512
MEM_W = MEM_HEADS * MEM_DH
N_BRANCH = 3
BRANCH_W = 2048
PEER_HEADS = 8
PEER_NKEYS = 128
PEER_DK = 256
PEER_TOPK = 16

COL_SB = 0
COL_DQ = 3 * SB_W
COL_DKV = COL_DQ + MLA_Q_RANK
DKV_PAD = 1024
COL_MEMQ = COL_DKV + DKV_PAD
COL_GATE = COL_MEMQ + MEM_W

VMEM_LIMIT_BYTES = 56 * 1024 * 1024


def _cparams(*sem):
    return pltpu.CompilerParams(dimension_semantics=sem, vmem_limit_bytes=VMEM_LIMIT_BYTES)


def _nt_dot(a, b):
    return lax.dot_general(a, b, (((1,), (1,)), ((), ())), preferred_element_type=F32)


def _rmsnorm_kernel(x_ref, g_ref, o_ref):
    x = x_ref[...].astype(F32)
    ms = jnp.mean(x * x, axis=-1, keepdims=True)
    o_ref[...] = (x * lax.rsqrt(ms + EPS) * g_ref[...]).astype(o_ref.dtype)


def _rmsnorm(x, g, out_dtype, tm=256):
    m, d = x.shape
    tm = min(tm, m)
    return pl.pallas_call(
        _rmsnorm_kernel,
        grid=(m // tm,),
        in_specs=[pl.BlockSpec((tm, d), lambda i: (i, 0)), pl.BlockSpec((1, d), lambda i: (0, 0))],
        out_specs=pl.BlockSpec((tm, d), lambda i: (i, 0)),
        out_shape=jax.ShapeDtypeStruct((m, d), out_dtype),
        compiler_params=_cparams("parallel"),
        name="rmsnorm",
    )(x, g.reshape(1, d).astype(F32))


def _add_rmsnorm_kernel(a_ref, b_ref, g_ref, o_ref):
    x = a_ref[...] + b_ref[...]
    ms = jnp.mean(x * x, axis=-1, keepdims=True)
    o_ref[...] = (x * lax.rsqrt(ms + EPS) * g_ref[...]).astype(o_ref.dtype)


def _add_rmsnorm(a, b, g, tm=256):
    m, d = a.shape
    tm = min(tm, m)
    return pl.pallas_call(
        _add_rmsnorm_kernel,
        grid=(m // tm,),
        in_specs=[pl.BlockSpec((tm, d), lambda i: (i, 0)), pl.BlockSpec((tm, d), lambda i: (i, 0)),
                  pl.BlockSpec((1, d), lambda i: (0, 0))],
        out_specs=pl.BlockSpec((tm, d), lambda i: (i, 0)),
        out_shape=jax.ShapeDtypeStruct((m, d), F32),
        compiler_params=_cparams("parallel"),
        name="add_rmsnorm",
    )(a, b, g.reshape(1, d).astype(F32))


def _mm_kernel(a_ref, b_ref, o_ref):
    o_ref[...] = jnp.dot(a_ref[...], b_ref[...], preferred_element_type=F32).astype(o_ref.dtype)


def _mm_res_kernel(a_ref, b_ref, r_ref, o_ref):
    acc = jnp.dot(a_ref[...], b_ref[...], preferred_element_type=F32)
    o_ref[...] = (acc + r_ref[...]).astype(o_ref.dtype)


def _mm_split_kernel(a_ref, b_ref, o_ref):
    acc = jnp.dot(a_ref[...], b_ref[...], preferred_element_type=F32)
    for c in range(o_ref.shape[0]):
        o_ref[c] = acc[:, c * LANES:(c + 1) * LANES].astype(o_ref.dtype)


def _matmul_split(a, b, out_dtype, tm, tn, name):
    m, k = a.shape
    _, n = b.shape
    tm, tn = min(tm, m), min(tn, n)
    return pl.pallas_call(
        _mm_split_kernel,
        grid=(n // tn, m // tm),
        in_specs=[pl.BlockSpec((tm, k), lambda j, i: (i, 0)), pl.BlockSpec((k, tn), lambda j, i: (0, j))],
        out_specs=pl.BlockSpec((tn // LANES, tm, LANES), lambda j, i: (j, i, 0)),
        out_shape=jax.ShapeDtypeStruct((n // LANES, m, LANES), out_dtype),
        compiler_params=_cparams("parallel", "parallel"),
        name=name,
    )(a, b)


def _matmul(a, b, out_dtype, tm, tn, residual=None, name="matmul"):
    m, k = a.shape
    _, n = b.shape
    tm, tn = min(tm, m), min(tn, n)
    in_specs = [pl.BlockSpec((tm, k), lambda j, i: (i, 0)), pl.BlockSpec((k, tn), lambda j, i: (0, j))]
    args = [a, b]
    body = _mm_kernel
    if residual is not None:
        in_specs.append(pl.BlockSpec((tm, tn), lambda j, i: (i, j)))
        args.append(residual)
        body = _mm_res_kernel
    return pl.pallas_call(
        body,
        grid=(n // tn, m // tm),
        in_specs=in_specs,
        out_specs=pl.BlockSpec((tm, tn), lambda j, i: (i, j)),
        out_shape=jax.ShapeDtypeStruct((m, n), out_dtype),
        compiler_params=_cparams("parallel", "parallel"),
        name=name,
    )(*args)


SB_BQ = 256
SB_BK = 128
SB_HG = 8
SB_STOP = -105.0


def _sb_kernel(q_ref, k_ref, v_ref, o_ref, *, scale):
    qi = pl.program_id(2)
    bq = q_ref.shape[1]
    qs = [(q_ref[0, :, h * SB_DH:(h + 1) * SB_DH].astype(F32) * scale).astype(BF16) for h in range(SB_HG)]
    r = lax.broadcasted_iota(jnp.int32, (SB_BK, 2 * SB_BK), 0)
    c = lax.broadcasted_iota(jnp.int32, (SB_BK, 2 * SB_BK), 1)
    u_ext = jnp.where((c >= SB_BK) | (r > c), 1.0, 0.0).astype(BF16)
    qpos = qi * bq + lax.broadcasted_iota(jnp.int32, (bq, SB_BK), 0)
    kofs = lax.broadcasted_iota(jnp.int32, (bq, SB_BK), 1)

    def block(kb, runs, accs, masked):
        start = pl.multiple_of(kb * SB_BK, SB_BK)
        if masked:
            strict = (kb * SB_BK + kofs) < qpos
        heads = range(SB_HG)
        zs = [_nt_dot(qs[h], k_ref[0, pl.ds(start, SB_BK), h * SB_DH:(h + 1) * SB_DH]) for h in heads]
        lks = [-(jnp.maximum(z, 0.0) + jnp.log(1.0 + jnp.exp(-jnp.abs(z)))) for z in zs]
        lkms = [jnp.where(strict, lk, 0.0) for lk in lks] if masked else lks
        his = [lkm.astype(BF16) for lkm in lkms]
        los = [(lkm - hi.astype(F32)).astype(BF16) for lkm, hi in zip(lkms, his)]
        css = [jnp.dot(hi, u_ext, preferred_element_type=F32) + jnp.dot(lo, u_ext, preferred_element_type=F32)
               for hi, lo in zip(his, los)]
        ws = [jnp.exp(zs[h] + lks[h] + runs[h] + css[h][:, :SB_BK]) for h in heads]
        if masked:
            ws = [jnp.where(strict, w, 0.0) for w in ws]
        new_accs = [accs[h] + jnp.dot(ws[h].astype(BF16), v_ref[0, pl.ds(start, SB_BK), h * SB_DH:(h + 1) * SB_DH],
                                      preferred_element_type=F32) for h in heads]
        new_runs = [runs[h] + css[h][:, SB_BK:] for h in heads]
        return tuple(new_runs), tuple(new_accs)

    def least_decayed(runs):
        mx = jnp.max(runs[0])
        for h in range(1, SB_HG):
            mx = jnp.maximum(mx, jnp.max(runs[h]))
        return mx

    runs = tuple(jnp.zeros((bq, SB_BK), F32) for _ in range(SB_HG))
    accs = tuple(jnp.zeros((bq, SB_DH), F32) for _ in range(SB_HG))
    nd = bq // SB_BK
    for d in range(nd):
        runs, accs = block(qi * nd + (nd - 1 - d), runs, accs, True)

    def cond(state):
        kb, mx, _, _ = state
        return jnp.logical_and(kb >= 0, mx > SB_STOP)

    def body(state):
        kb, _, rs, ac = state
        rs, ac = block(kb, rs, ac, False)
        return kb - 1, least_decayed(rs), rs, ac

    _, _, _, accs = lax.while_loop(cond, body, (qi * nd - 1, least_decayed(runs), runs, accs))
    for h in range(SB_HG):
        o_ref[0, :, h * SB_DH:(h + 1) * SB_DH] = accs[h].astype(o_ref.dtype)


def _sb_attention(proj3):
    b, s, _ = proj3.shape
    bq = min(SB_BQ, s)
    ng = SB_HEADS // SB_HG
    wg = SB_HG * SB_DH
    return pl.pallas_call(
        functools.partial(_sb_kernel, scale=1.0 / math.sqrt(SB_DH)),
        grid=(b, ng, s // bq),
        in_specs=[
            pl.BlockSpec((1, bq, wg), lambda bi, g, qi: (bi, qi, g)),
            pl.BlockSpec((1, s, wg), lambda bi, g, qi: (bi, 0, ng + g), pipeline_mode=pl.Buffered(1)),
            pl.BlockSpec((1, s, wg), lambda bi, g, qi: (bi, 0, 2 * ng + g), pipeline_mode=pl.Buffered(1)),
        ],
        out_specs=pl.BlockSpec((1, bq, wg), lambda bi, g, qi: (bi, qi, g)),
        out_shape=jax.ShapeDtypeStruct((b, s, SB_W), BF16),
        compiler_params=_cparams("parallel", "parallel", "arbitrary"),
        name="sb_attention",
    )(proj3, proj3, proj3)


def _mla_prep_kernel(dq_ref, dkv_ref, c_ref, s_ref, gq_ref, gkv_ref, wab_ref, wkv_ref,
                     qcat_ref, kcat_ref, v_ref, *, scale):
    cmap = c_ref[...]
    smap = s_ref[...]
    dq = dq_ref[...].astype(F32)
    cq = dq * lax.rsqrt(jnp.mean(dq * dq, axis=-1, keepdims=True) + EPS) * gq_ref[...]
    qab = jnp.dot(cq.astype(BF16), wab_ref[...], preferred_element_type=F32)
    swap0 = MLA_HEADS * MLA_QK_PAD
    for h in range(MLA_HEADS):
        lo = h * MLA_QK_PAD
        qcat_ref[:, lo:lo + LANES] = (qab[:, lo:lo + LANES] * scale).astype(BF16)
        rope = qab[:, lo + LANES:lo + 2 * LANES] * cmap + qab[:, swap0 + h * LANES:swap0 + (h + 1) * LANES] * smap
        qcat_ref[:, lo + LANES:lo + 2 * LANES] = (rope * scale).astype(BF16)
    dkv = dkv_ref[...].astype(F32)
    ckv_in = dkv[:, :MLA_KV_RANK]
    ckv = ckv_in * lax.rsqrt(jnp.mean(ckv_in * ckv_in, axis=-1, keepdims=True) + EPS) * gkv_ref[...]
    kv = jnp.dot(ckv.astype(BF16), wkv_ref[...], preferred_element_type=F32)
    kr = (dkv[:, MLA_KV_RANK:MLA_KV_RANK + LANES] * cmap
          + dkv[:, MLA_KV_RANK + LANES:MLA_KV_RANK + 2 * LANES] * smap).astype(BF16)
    for h in range(MLA_HEADS):
        lo = h * MLA_QK_PAD
        kcat_ref[:, lo:lo + LANES] = kv[:, h * LANES:(h + 1) * LANES].astype(BF16)
        kcat_ref[:, lo + LANES:lo + 2 * LANES] = kr
    v_ref[...] = kv[:, MLA_HEADS * MLA_NOPE:].astype(BF16)


def _mla_prep(proj, cmap, smap, gq, gkv, wab, wkv, tt=256):
    t = proj.shape[0]
    tt = min(tt, t)
    wq = MLA_HEADS * MLA_QK_PAD
    wv = MLA_HEADS * MLA_DV
    return pl.pallas_call(
        functools.partial(_mla_prep_kernel, scale=math.log2(math.e) / math.sqrt(MLA_NOPE + MLA_ROPE)),
        grid=(t // tt,),
        in_specs=[
            pl.BlockSpec((tt, MLA_Q_RANK), lambda i: (i, COL_DQ // MLA_Q_RANK)),
            pl.BlockSpec((tt, DKV_PAD), lambda i: (i, COL_DKV // DKV_PAD)),
            pl.BlockSpec((tt, LANES), lambda i: (i, 0)),
            pl.BlockSpec((tt, LANES), lambda i: (i, 0)),
            pl.BlockSpec((1, MLA_Q_RANK), lambda i: (0, 0)),
            pl.BlockSpec((1, MLA_KV_RANK), lambda i: (0, 0)),
            pl.BlockSpec(wab.shape, lambda i: (0, 0)),
            pl.BlockSpec(wkv.shape, lambda i: (0, 0)),
        ],
        out_specs=[
            pl.BlockSpec((tt, wq), lambda i: (i, 0)),
            pl.BlockSpec((tt, wq), lambda i: (i, 0)),
            pl.BlockSpec((tt, wv), lambda i: (i, 0)),
        ],
        out_shape=[
            jax.ShapeDtypeStruct((t, wq), BF16),
            jax.ShapeDtypeStruct((t, wq), BF16),
            jax.ShapeDtypeStruct((t, wv), BF16),
        ],
        compiler_params=_cparams("parallel"),
        name="mla_prep",
    )(proj, proj, cmap, smap, gq.reshape(1, -1).astype(F32), gkv.reshape(1, -1).astype(F32), wab, wkv)


MLA_BQ = 512
MLA_BK = 512
MLA_HG = 4


def _mla_kernel(q_ref, k_ref, v_ref, o_ref):
    qi = pl.program_id(2)
    bq = q_ref.shape[1]
    bk = min(MLA_BK, bq)
    heads = range(MLA_HG)
    qs = [q_ref[0, :, h * MLA_QK_PAD:(h + 1) * MLA_QK_PAD] for h in heads]
    kpos = lax.broadcasted_iota(jnp.int32, (bk, bq), 0)
    qpos = qi * bq + lax.broadcasted_iota(jnp.int32, (bk, bq), 1)

    def block(kb, ms, ls, accs, masked):
        start = pl.multiple_of(kb * bk, bk)
        ss = [_nt_dot(k_ref[0, pl.ds(start, bk), h * MLA_QK_PAD:(h + 1) * MLA_QK_PAD], qs[h]) for h in heads]
        if masked:
            allowed = (kb * bk + kpos) <= qpos
            ss = [jnp.where(allowed, s, NEG) for s in ss]
        new_ms = [jnp.maximum(ms[h], jnp.max(ss[h], axis=0, keepdims=True)) for h in heads]
        ps = [jnp.exp2(ss[h] - new_ms[h]) for h in heads]
        alphas = [jnp.exp2(ms[h] - new_ms[h]) for h in heads]
        new_ls = [alphas[h] * ls[h] + jnp.sum(ps[h], axis=0, keepdims=True) for h in heads]
        pvs = [lax.dot_general(v_ref[0, pl.ds(start, bk), h * MLA_DV:(h + 1) * MLA_DV], ps[h].astype(BF16),
                               (((0,), (0,)), ((), ())), preferred_element_type=F32) for h in heads]
        new_accs = [alphas[h] * accs[h] + pvs[h] for h in heads]
        return tuple(new_ms), tuple(new_ls), tuple(new_accs)

    ms = tuple(jnp.full((1, bq), NEG, F32) for _ in heads)
    ls = tuple(jnp.zeros((1, bq), F32) for _ in heads)
    accs = tuple(jnp.zeros((MLA_DV, bq), F32) for _ in heads)
    nd = bq // bk
    ms, ls, accs = lax.fori_loop(0, qi * nd, lambda kb, cr: block(kb, cr[0], cr[1], cr[2], False), (ms, ls, accs))
    for d in range(nd):
        ms, ls, accs = block(qi * nd + d, ms, ls, accs, True)
    for h in heads:
        o_ref[0, :, h * MLA_DV:(h + 1) * MLA_DV] = (accs[h] / ls[h]).T.astype(o_ref.dtype)


def _mla_attention(qcat3, kcat3, v3):
    b, s, _ = qcat3.shape
    bq = min(MLA_BQ, s)
    wq = MLA_HG * MLA_QK_PAD
    wv = MLA_HG * MLA_DV
    return pl.pallas_call(
        _mla_kernel,
        grid=(b, MLA_HEADS // MLA_HG, s // bq),
        in_specs=[
            pl.BlockSpec((1, bq, wq), lambda bi, g, qi: (bi, qi, g)),
            pl.BlockSpec((1, s, wq), lambda bi, g, qi: (bi, 0, g), pipeline_mode=pl.Buffered(1)),
            pl.BlockSpec((1, s, wv), lambda bi, g, qi: (bi, 0, g), pipeline_mode=pl.Buffered(1)),
        ],
        out_specs=pl.BlockSpec((1, bq, wv), lambda bi, g, qi: (bi, qi, g)),
        out_shape=jax.ShapeDtypeStruct((b, s, MLA_HEADS * MLA_DV), BF16),
        compiler_params=_cparams("parallel", "parallel", "arbitrary"),
        name="mla_attention",
    )(qcat3, kcat3, v3)


def _mem_kernel(q_ref, k_ref, v_ref, o_ref, *, scale):
    s = _nt_dot(q_ref[0], k_ref[0]) * scale
    m = jnp.max(s, axis=-1, keepdims=True)
    p = jnp.exp(s - m)
    p = p / jnp.sum(p, axis=-1, keepdims=True)
    o_ref[0] = jnp.dot(p.astype(BF16), v_ref[0], preferred_element_type=F32).astype(o_ref.dtype)


def _mem_attention(proj3, mkv3, tt=512):
    b, s, _ = proj3.shape
    n_mem = mkv3.shape[1]
    tt = min(tt, s)
    q0 = COL_MEMQ // MEM_DH
    return pl.pallas_call(
        functools.partial(_mem_kernel, scale=1.0 / math.sqrt(MEM_DH)),
        grid=(b, s // tt, MEM_HEADS),
        in_specs=[
            pl.BlockSpec((1, tt, MEM_DH), lambda bi, ti, h: (bi, ti, q0 + h)),
            pl.BlockSpec((1, n_mem, MEM_DH), lambda bi, ti, h: (bi, 0, h)),
            pl.BlockSpec((1, n_mem, MEM_DH), lambda bi, ti, h: (bi, 0, MEM_HEADS + h)),
        ],
        out_specs=pl.BlockSpec((1, tt, MEM_DH), lambda bi, ti, h: (bi, ti, h)),
        out_shape=jax.ShapeDtypeStruct((b, s, MEM_W), BF16),
        compiler_params=_cparams("parallel", "parallel", "parallel"),
        name="mem_attention",
    )(proj3, mkv3, mkv3)


def _merge_kernel(ya_ref, yb_ref, yc_ref, ga_ref, gb_ref, gc_ref, w_ref, o_ref):
    acc = jax.nn.sigmoid(ga_ref[...].astype(F32)) * jnp.dot(ya_ref[...], w_ref[0], preferred_element_type=F32)
    acc += jax.nn.sigmoid(gb_ref[...].astype(F32)) * jnp.dot(yb_ref[...], w_ref[1], preferred_element_type=F32)
    acc += jax.nn.sigmoid(gc_ref[...].astype(F32)) * jnp.dot(yc_ref[...], w_ref[2], preferred_element_type=F32)
    o_ref[...] = acc.astype(o_ref.dtype)


def _merge(y_sb, y_mla, y_mem, proj, wb, d, tm=1024, tn=512):
    t = y_sb.shape[0]
    tm, tn = min(tm, t), min(tn, d)
    g0 = COL_GATE // tn
    gstep = d // tn

    def gate_spec(br):
        return pl.BlockSpec((tm, tn), lambda j, i: (i, g0 + br * gstep + j))

    yspec = pl.BlockSpec((tm, BRANCH_W), lambda j, i: (i, 0))
    return pl.pallas_call(
        _merge_kernel,
        grid=(d // tn, t // tm),
        in_specs=[yspec, yspec, yspec, gate_spec(0), gate_spec(1), gate_spec(2),
                  pl.BlockSpec((N_BRANCH, BRANCH_W, tn), lambda j, i: (0, 0, j))],
        out_specs=pl.BlockSpec((tm, tn), lambda j, i: (i, j)),
        out_shape=jax.ShapeDtypeStruct((t, d), BF16),
        compiler_params=_cparams("parallel", "parallel"),
        name="branch_merge",
    )(y_sb, y_mla, y_mem, proj, proj, proj, wb)


PEER_EB = 512
PEER_GROUP = PEER_EB // PEER_NKEYS
PEER_RANK_NONE = 255.0
PEER_CAND_WIDTH = [PEER_TOPK // (a + 1) for a in range(PEER_TOPK)]


def _take_top(vals, n, with_rank=False):
    out = []
    rem = vals
    rank = jnp.full(vals.shape, PEER_RANK_NONE, F32) if with_rank else None
    for k in range(n):
        mx = jnp.max(rem, axis=0, keepdims=True)
        out.append(mx)
        hit = rem == mx
        if with_rank:
            rank = jnp.where(hit, float(k), rank)
        rem = jnp.where(hit, -jnp.inf, rem)
    return (out, rank) if with_rank else out


def _peer_route_kernel(q_ref, keys_ref, nw_ref, r1_ref, e1_ref):
    def head(h, carry):
        s0 = _nt_dot(keys_ref[h, 0], q_ref[2 * h].astype(BF16))
        s1 = _nt_dot(keys_ref[h, 1], q_ref[2 * h + 1].astype(BF16))
        top0 = _take_top(s0, PEER_TOPK)
        top1, rank1 = _take_top(s1, PEER_TOPK, with_rank=True)
        cand = [[top0[a] + top1[b] for b in range(PEER_CAND_WIDTH[a])] for a in range(PEER_TOPK)]
        flat = [c for row in cand for c in row]
        pad = -len(flat) % 8
        stacked = jnp.concatenate(flat + [jnp.full_like(flat[0], -jnp.inf)] * pad, axis=0)
        ctop = _take_top(stacked, PEER_TOPK)
        cmax, tau = ctop[0], ctop[PEER_TOPK - 1]
        zsum = jnp.zeros_like(cmax)
        for cv in ctop:
            zsum = zsum + jnp.exp(cv - cmax)
        n0 = jnp.zeros_like(s0)
        for a in range(PEER_TOPK):
            count = jnp.zeros_like(tau)
            for c in cand[a]:
                count = count + jnp.where(c >= tau, 1.0, 0.0)
            n0 = jnp.where(s0 == top0[a], count, n0)
        w0 = jnp.exp(s0 - top0[0]) / zsum
        for g in range(PEER_NKEYS // PEER_GROUP):
            keys = slice(g * PEER_GROUP, (g + 1) * PEER_GROUP)
            nw_ref[g, h, :PEER_GROUP, :] = n0[keys]
            nw_ref[g, h, PEER_GROUP:, :] = w0[keys]
        r1_ref[h] = pltpu.bitcast(rank1.astype(BF16), jnp.uint32)
        e1_ref[h] = pltpu.bitcast(jnp.exp(s1 - top1[0]).astype(BF16), jnp.uint32)
        return carry

    lax.fori_loop(0, PEER_HEADS, head, 0)


def _peer_route(q3, keys, tt=256):
    _, t, dh = q3.shape
    tt = min(tt, t)
    ngrp = PEER_NKEYS // PEER_GROUP
    nwspec = pl.BlockSpec((ngrp, PEER_HEADS, 2 * PEER_GROUP, tt), lambda i: (0, 0, 0, i))
    pspec = pl.BlockSpec((PEER_HEADS, PEER_NKEYS // 2, tt), lambda i: (0, 0, i))
    packed = jax.ShapeDtypeStruct((PEER_HEADS, PEER_NKEYS // 2, t), jnp.uint32)
    return pl.pallas_call(
        _peer_route_kernel,
        grid=(t // tt,),
        in_specs=[pl.BlockSpec((2 * PEER_HEADS, tt, dh), lambda i: (0, i, 0)),
                  pl.BlockSpec(keys.shape, lambda i: (0, 0, 0, 0))],
        out_specs=[nwspec, pspec, pspec],
        out_shape=[jax.ShapeDtypeStruct((ngrp, PEER_HEADS, 2 * PEER_GROUP, t), F32), packed, packed],
        compiler_params=_cparams("parallel"),
        name="peer_route",
    )(q3, keys)


PEER_ROWS = 16


def _peer_dense_kernel(xn_ref, u_ref, v_ref, nw_ref, r1_ref, e1_ref, o_ref, at_ref):
    e = pl.program_id(1)

    @pl.when(e == 0)
    def _():
        o_ref[...] = jnp.zeros_like(o_ref)

    pre = _nt_dot(u_ref[...], xn_ref[...])
    at_ref[...] = (0.5 * pre * (1.0 + lax.erf(pre * (1.0 / math.sqrt(2.0))))).astype(BF16)
    tt = at_ref.shape[1]
    zero = jnp.zeros((PEER_ROWS, LANES), BF16)
    for ii in range(PEER_GROUP):
        for l0 in range(0, tt, LANES):
            lanes = slice(l0, l0 + LANES)
            counts = [jnp.broadcast_to(nw_ref[0, h, ii:ii + 1, lanes], (PEER_ROWS, LANES)).astype(BF16)
                      for h in range(PEER_HEADS)]
            weights = [jnp.broadcast_to(nw_ref[0, h, PEER_GROUP + ii:PEER_GROUP + ii + 1, lanes],
                                        (PEER_ROWS, LANES)).astype(BF16) for h in range(PEER_HEADS)]
            for r0 in range(0, PEER_NKEYS, PEER_ROWS):
                words = slice(r0 // 2, (r0 + PEER_ROWS) // 2)
                gate = zero
                for h in range(PEER_HEADS):
                    rank = pltpu.bitcast(r1_ref[h, words, lanes], BF16)
                    e1 = pltpu.bitcast(e1_ref[h, words, lanes], BF16)
                    gate = gate + jnp.where(rank < counts[h], e1 * weights[h], zero)
                e0 = ii * PEER_NKEYS + r0
                at_ref[e0:e0 + PEER_ROWS, lanes] = at_ref[e0:e0 + PEER_ROWS, lanes] * gate
    o_ref[...] += lax.dot_general(at_ref[...], v_ref[...], (((0,), (0,)), ((), ())), preferred_element_type=F32)


def _peer_dense(xn, u, v, nw, r1, e1, tt=512):
    t, d = xn.shape
    n = u.shape[0]
    tt = min(tt, t)
    hspec = pl.BlockSpec((PEER_HEADS, PEER_NKEYS // 2, tt), lambda ti, e: (0, 0, ti))
    return pl.pallas_call(
        _peer_dense_kernel,
        grid=(t // tt, n // PEER_EB),
        in_specs=[
            pl.BlockSpec((tt, d), lambda ti, e: (ti, 0)),
            pl.BlockSpec((PEER_EB, d), lambda ti, e: (e, 0)),
            pl.BlockSpec((PEER_EB, d), lambda ti, e: (e, 0)),
            pl.BlockSpec((1, PEER_HEADS, 2 * PEER_GROUP, tt), lambda ti, e: (e, 0, 0, ti)),
            hspec,
            hspec,
        ],
        out_specs=pl.BlockSpec((tt, d), lambda ti, e: (ti, 0)),
        out_shape=jax.ShapeDtypeStruct((t, d), F32),
        scratch_shapes=[pltpu.VMEM((PEER_EB, tt), BF16)],
        compiler_params=_cparams("parallel", "arbitrary"),
        name="peer_dense",
    )(xn, u, v, nw, r1, e1)


IN_TN = 1024
IN_ROW_ALIGN = 64


def _in_proj_kernel(a_ref, w_ref, o_ref, wb_ref, *, dkv_block):
    j = pl.program_id(0)
    i = pl.program_id(1)
    half = MLA_ROPE // 2
    kr0 = MLA_KV_RANK

    @pl.when(jnp.logical_and(i == 0, j != dkv_block))
    def _():
        wb_ref[...] = w_ref[...].astype(BF16)

    @pl.when(jnp.logical_and(i == 0, j == dkv_block))
    def _():
        wb_ref[...] = jnp.zeros_like(wb_ref)
        wb_ref[:kr0 + MLA_ROPE, :] = w_ref[:kr0 + MLA_ROPE, :].astype(BF16)
        wb_ref[kr0 + LANES:kr0 + LANES + half, :] = w_ref[kr0 + half:kr0 + MLA_ROPE, :].astype(BF16)
        wb_ref[kr0 + LANES + half:kr0 + LANES + MLA_ROPE, :] = w_ref[kr0:kr0 + half, :].astype(BF16)

    o_ref[...] = _nt_dot(a_ref[...], wb_ref[...]).astype(o_ref.dtype)


def _in_proj(xn, w_in_t, d, tm=512):
    t = xn.shape[0]
    c3 = 3 * SB_W + MLA_Q_RANK + MLA_KV_RANK + MLA_ROPE
    n_cols = COL_GATE + N_BRANCH * d
    dkv_block = COL_DKV // IN_TN
    assert DKV_PAD == IN_TN and COL_DKV % IN_TN == 0 and (n_cols - COL_MEMQ) % IN_TN == 0
    assert w_in_t.shape[0] - c3 == n_cols - COL_MEMQ and c3 % IN_ROW_ALIGN == 0
    tm = min(tm, t)

    def row0(j):
        return pl.multiple_of(jnp.where(j <= dkv_block, j * IN_TN, c3 + (j - dkv_block - 1) * IN_TN), IN_ROW_ALIGN)

    return pl.pallas_call(
        functools.partial(_in_proj_kernel, dkv_block=dkv_block),
        grid=(n_cols // IN_TN, t // tm),
        in_specs=[pl.BlockSpec((tm, d), lambda j, i: (i, 0)),
                  pl.BlockSpec((pl.Element(IN_TN), pl.Element(d)), lambda j, i: (row0(j), 0))],
        out_specs=pl.BlockSpec((tm, IN_TN), lambda j, i: (i, j)),
        out_shape=jax.ShapeDtypeStruct((t, n_cols), BF16),
        scratch_shapes=[pltpu.VMEM((IN_TN, d), BF16)],
        compiler_params=_cparams("arbitrary", "arbitrary"),
        name="in_proj",
    )(xn, w_in_t)


def _pack_mla_weights(w_uq, w_ukv):
    half = MLA_ROPE // 2
    rq = w_uq.shape[0]
    wq = w_uq.reshape(rq, MLA_HEADS, MLA_NOPE + MLA_ROPE)
    nope, t1, t2 = wq[:, :, :MLA_NOPE], wq[:, :, MLA_NOPE:MLA_NOPE + half], wq[:, :, MLA_NOPE + half:]
    zq = jnp.zeros((rq, MLA_HEADS, LANES - MLA_ROPE), w_uq.dtype)
    w_a = jnp.concatenate([nope, t1, t2, zq], axis=2).reshape(rq, MLA_HEADS * MLA_QK_PAD)
    w_b = jnp.concatenate([t2, t1, zq], axis=2).reshape(rq, MLA_HEADS * LANES)
    wab = jnp.concatenate([w_a, w_b], axis=1).astype(BF16)
    rkv = w_ukv.shape[0]
    wkv = w_ukv.reshape(rkv, MLA_HEADS, MLA_NOPE + MLA_DV)
    wkv = jnp.concatenate([wkv[:, :, :MLA_NOPE].reshape(rkv, -1), wkv[:, :, MLA_NOPE:].reshape(rkv, -1)], axis=1)
    return wab, wkv.astype(BF16)


def _rope_maps(positions):
    half = MLA_ROPE // 2
    freqs = ROPE_BASE ** (-jnp.arange(half, dtype=F32) / half)
    ang = positions.astype(F32).reshape(-1)[:, None] * freqs
    cos, sin = jnp.cos(ang), jnp.sin(ang)
    z = jnp.zeros((ang.shape[0], LANES - MLA_ROPE), F32)
    return jnp.concatenate([cos, cos, z], axis=1), jnp.concatenate([-sin, sin, z], axis=1)


def _layer(h, mem, cmap, smap, g_mix, w_in, mla_g_q, mla_w_uq, mla_g_kv, mla_w_ukv, g_mem, w_mem_kv,
           w_branch, w_out, g_ffn, peer_w_q, peer_sub_keys, peer_u, peer_v):
    b, s, d = h.shape
    t = b * s
    n_mem = mem.shape[1]
    h2 = h.reshape(t, d)

    xn = _rmsnorm(h2, g_mix, BF16)
    proj = _in_proj(xn, w_in.T, d)
    proj3 = proj.reshape(b, s, -1)

    y_sb = _sb_attention(proj3).reshape(t, SB_W)

    wab, wkv = _pack_mla_weights(mla_w_uq, mla_w_ukv)
    qcat, kcat, v = _mla_prep(proj, cmap, smap, mla_g_q, mla_g_kv, wab, wkv)
    y_mla = _mla_attention(qcat.reshape(b, s, -1), kcat.reshape(b, s, -1), v.reshape(b, s, -1)).reshape(t, -1)

    mem_n = _rmsnorm(mem.reshape(b * n_mem, d), g_mem, BF16)
    mkv = _matmul(mem_n, w_mem_kv.astype(BF16), BF16, tm=512, tn=1024, name="mem_kv")
    y_mem = _mem_attention(proj3, mkv.reshape(b, n_mem, 2 * MEM_W)).reshape(t, MEM_W)

    merged = _merge(y_sb, y_mla, y_mem, proj, w_branch.astype(BF16), d)
    h1 = _matmul(merged, w_out.astype(BF16), F32, tm=512, tn=1024, residual=h2, name="out_proj")

    xn2 = _rmsnorm(h1, g_ffn, BF16)
    q3 = _matmul_split(xn2, peer_w_q.astype(BF16), F32, tm=512, tn=1024, name="peer_q")
    nw, r1, e1 = _peer_route(q3, peer_sub_keys.astype(BF16))
    y = _peer_dense(xn2, peer_u.astype(BF16), peer_v.astype(BF16), nw, r1, e1)
    return h1, y


def kernel(x, mem, positions, g_mix, w_in, mla_g_q, mla_w_uq, mla_g_kv, mla_w_ukv, g_mem, w_mem_kv, w_branch,
           w_out, g_ffn, peer_w_q, peer_sub_keys, peer_u, peer_v, g_final):
    b, s, d = x.shape
    cmap, smap = _rope_maps(positions)
    depth = w_in.shape[0]
    h = x
    for layer in range(depth):
        h1, y = _layer(h, mem, cmap, smap, g_mix[layer], w_in[layer], mla_g_q[layer], mla_w_uq[layer],
                       mla_g_kv[layer], mla_w_ukv[layer], g_mem[layer], w_mem_kv[layer], w_branch[layer],
                       w_out[layer], g_ffn[layer], peer_w_q[layer], peer_sub_keys[layer], peer_u[layer],
                       peer_v[layer])
        if layer + 1 < depth:
            h = (h1 + y).reshape(b, s, d)
    return _add_rmsnorm(h1, y, g_final).reshape(b, s, d)
```

```python
import functools
import math

import jax
import jax.numpy as jnp
from jax import lax
from jax.experimental import pallas as pl
from jax.experimental.pallas import tpu as pltpu

F32 = jnp.float32
BF16 = jnp.bfloat16

EPS = 1e-6
NEG = -1e30
LANES = 128

SB_HEADS = 16
SB_DH = 128
SB_W = SB_HEADS * SB_DH
MLA_HEADS = 16
MLA_Q_RANK = 1024
MLA_KV_RANK = 512
MLA_NOPE = 128
MLA_ROPE = 64
MLA_DV = 128
MLA_QK_PAD = 256
ROPE_BASE = 10000.0
MEM_HEADS = 4
MEM_DH = 512
MEM_W = MEM_HEADS * MEM_DH
N_BRANCH = 3
BRANCH_W = 2048
PEER_HEADS = 8
PEER_NKEYS = 128
PEER_DK = 256
PEER_TOPK = 16

COL_SB = 0
COL_DQ = 3 * SB_W
COL_DKV = COL_DQ + MLA_Q_RANK
DKV_PAD = 1024
COL_MEMQ = COL_DKV + DKV_PAD
COL_GATE = COL_MEMQ + MEM_W

VMEM_LIMIT_BYTES = 56 * 1024 * 1024


def _cparams(*sem):
    return pltpu.CompilerParams(dimension_semantics=sem, vmem_limit_bytes=VMEM_LIMIT_BYTES)


def _nt_dot(a, b):
    return lax.dot_general(a, b, (((1,), (1,)), ((), ())), preferred_element_type=F32)


def _rmsnorm_kernel(x_ref, g_ref, o_ref):
    x = x_ref[...].astype(F32)
    ms = jnp.mean(x * x, axis=-1, keepdims=True)
    o_ref[...] = (x * lax.rsqrt(ms + EPS) * g_ref[...]).astype(o_ref.dtype)


def _rmsnorm(x, g, out_dtype, tm=256):
    m, d = x.shape
    tm = min(tm, m)
    return pl.pallas_call(
        _rmsnorm_kernel,
        grid=(m // tm,),
        in_specs=[pl.BlockSpec((tm, d), lambda i: (i, 0)), pl.BlockSpec((1, d), lambda i: (0, 0))],
        out_specs=pl.BlockSpec((tm, d), lambda i: (i, 0)),
        out_shape=jax.ShapeDtypeStruct((m, d), out_dtype),
        compiler_params=_cparams("parallel"),
        name="rmsnorm",
    )(x, g.reshape(1, d).astype(F32))


def _add_rmsnorm_kernel(a_ref, b_ref, g_ref, o_ref):
    x = a_ref[...] + b_ref[...]
    ms = jnp.mean(x * x, axis=-1, keepdims=True)
    o_ref[...] = (x * lax.rsqrt(ms + EPS) * g_ref[...]).astype(o_ref.dtype)


def _add_rmsnorm(a, b, g, tm=256):
    m, d = a.shape
    tm = min(tm, m)
    return pl.pallas_call(
        _add_rmsnorm_kernel,
        grid=(m // tm,),
        in_specs=[pl.BlockSpec((tm, d), lambda i: (i, 0)), pl.BlockSpec((tm, d), lambda i: (i, 0)),
                  pl.BlockSpec((1, d), lambda i: (0, 0))],
        out_specs=pl.BlockSpec((tm, d), lambda i: (i, 0)),
        out_shape=jax.ShapeDtypeStruct((m, d), F32),
        compiler_params=_cparams("parallel"),
        name="add_rmsnorm",
    )(a, b, g.reshape(1, d).astype(F32))


def _mm_kernel(a_ref, b_ref, o_ref):
    o_ref[...] = jnp.dot(a_ref[...], b_ref[...], preferred_element_type=F32).astype(o_ref.dtype)


def _mm_res_kernel(a_ref, b_ref, r_ref, o_ref):
    acc = jnp.dot(a_ref[...], b_ref[...], preferred_element_type=F32)
    o_ref[...] = (acc + r_ref[...]).astype(o_ref.dtype)


def _mm_split_kernel(a_ref, b_ref, o_ref):
    acc = jnp.dot(a_ref[...], b_ref[...], preferred_element_type=F32)
    for c in range(o_ref.shape[0]):
        o_ref[c] = acc[:, c * LANES:(c + 1) * LANES].astype(o_ref.dtype)


def _matmul_split(a, b, out_dtype, tm, tn, name):
    m, k = a.shape
    _, n = b.shape
    tm, tn = min(tm, m), min(tn, n)
    return pl.pallas_call(
        _mm_split_kernel,
        grid=(n // tn, m // tm),
        in_specs=[pl.BlockSpec((tm, k), lambda j, i: (i, 0)), pl.BlockSpec((k, tn), lambda j, i: (0, j))],
        out_specs=pl.BlockSpec((tn // LANES, tm, LANES), lambda j, i: (j, i, 0)),
        out_shape=jax.ShapeDtypeStruct((n // LANES, m, LANES), out_dtype),
        compiler_params=_cparams("parallel", "parallel"),
        name=name,
    )(a, b)


def _matmul(a, b, out_dtype, tm, tn, residual=None, name="matmul"):
    m, k = a.shape
    _, n = b.shape
    tm, tn = min(tm, m), min(tn, n)
    in_specs = [pl.BlockSpec((tm, k), lambda j, i: (i, 0)), pl.BlockSpec((k, tn), lambda j, i: (0, j))]
    args = [a, b]
    body = _mm_kernel
    if residual is not None:
        in_specs.append(pl.BlockSpec((tm, tn), lambda j, i: (i, j)))
        args.append(residual)
        body = _mm_res_kernel
    return pl.pallas_call(
        body,
        grid=(n // tn, m // tm),
        in_specs=in_specs,
        out_specs=pl.BlockSpec((tm, tn), lambda j, i: (i, j)),
        out_shape=jax.ShapeDtypeStruct((m, n), out_dtype),
        compiler_params=_cparams("parallel", "parallel"),
        name=name,
    )(*args)


SB_BQ = 256
SB_BK = 128
SB_HG = 8
SB_PAIR = 2 * SB_DH
SB_STOP = -151.5


def _sb_kernel(q_ref, k_ref, v_ref, o_ref, *, scale):
    qi = pl.program_id(2)
    bq = q_ref.shape[1]
    pairs = range(SB_HG // 2)
    lo_half, hi_half = slice(0, SB_DH), slice(SB_DH, SB_PAIR)
    qs = [(q_ref[0, :, p * SB_PAIR:(p + 1) * SB_PAIR].astype(F32) * scale).astype(BF16) for p in pairs]
    r = lax.broadcasted_iota(jnp.int32, (2 * SB_BK, 2 * SB_BK), 0) & (SB_BK - 1)
    c = lax.broadcasted_iota(jnp.int32, (2 * SB_BK, 2 * SB_BK), 1)
    u2 = jnp.where((c >= SB_BK) | (r > c), 1.0, 0.0).astype(BF16)
    qpos = qi * bq + lax.broadcasted_iota(jnp.int32, (bq, SB_PAIR), 0)
    kofs = lax.broadcasted_iota(jnp.int32, (bq, SB_PAIR), 1) & (SB_BK - 1)
    zero_blk = jnp.zeros((SB_BK, SB_DH), BF16)

    def blockdiag(x):
        return jnp.concatenate([jnp.concatenate([x[:, lo_half], zero_blk], axis=1),
                                jnp.concatenate([zero_blk, x[:, hi_half]], axis=1)], axis=0)

    def block(kb, runs, accs, masked):
        start = pl.multiple_of(kb * SB_BK, SB_BK)
        if masked:
            strict = (kb * SB_BK + kofs) < qpos
        zs = [_nt_dot(qs[p], blockdiag(k_ref[0, pl.ds(start, SB_BK), p * SB_PAIR:(p + 1) * SB_PAIR])) for p in pairs]
        lks = [-(jnp.maximum(z, 0.0) + jnp.log2(1.0 + jnp.exp2(-jnp.abs(z)))) for z in zs]
        lkms = [jnp.where(strict, lk, 0.0) for lk in lks] if masked else lks
        his = [lkm.astype(BF16) for lkm in lkms]
        los = [(lkm - hi.astype(F32)).astype(BF16) for lkm, hi in zip(lkms, his)]
        css = [[jnp.dot(jnp.concatenate([his[p][:, half], los[p][:, half]], axis=1), u2, preferred_element_type=F32)
                for half in (lo_half, hi_half)] for p in pairs]
        sufs = [jnp.concatenate([css[p][0][:, :SB_BK], css[p][1][:, :SB_BK]], axis=1) for p in pairs]
        tots = [jnp.concatenate([css[p][0][:, SB_BK:], css[p][1][:, SB_BK:]], axis=1) for p in pairs]
        ws = [jnp.exp2(zs[p] + lks[p] + runs[p] + sufs[p]) for p in pairs]
        if masked:
            ws = [jnp.where(strict, w, 0.0) for w in ws]
        new_accs = [accs[p] + jnp.dot(ws[p].astype(BF16),
                                      blockdiag(v_ref[0, pl.ds(start, SB_BK), p * SB_PAIR:(p + 1) * SB_PAIR]),
                                      preferred_element_type=F32) for p in pairs]
        new_runs = [runs[p] + tots[p] for p in pairs]
        return tuple(new_runs), tuple(new_accs)

    def least_decayed(runs):
        mx = jnp.max(runs[0])
        for p in pairs[1:]:
            mx = jnp.maximum(mx, jnp.max(runs[p]))
        return mx

    runs = tuple(jnp.zeros((bq, SB_PAIR), F32) for _ in pairs)
    accs = tuple(jnp.zeros((bq, SB_PAIR), F32) for _ in pairs)
    nd = bq // SB_BK
    for d in range(nd):
        runs, accs = block(qi * nd + (nd - 1 - d), runs, accs, True)

    def cond(state):
        kb, mx, _, _ = state
        return jnp.logical_and(kb >= 0, mx > SB_STOP)

    def body(state):
        kb, _, rs, ac = state
        rs, ac = block(kb, rs, ac, False)
        return kb - 1, least_decayed(rs), rs, ac

    _, _, _, accs = lax.while_loop(cond, body, (qi * nd - 1, least_decayed(runs), runs, accs))
    for p in pairs:
        o_ref[0, :, p * SB_PAIR:(p + 1) * SB_PAIR] = accs[p].astype(o_ref.dtype)


def _sb_attention(proj3):
    b, s, _ = proj3.shape
    bq = min(SB_BQ, s)
    ng = SB_HEADS // SB_HG
    wg = SB_HG * SB_DH
    return pl.pallas_call(
        functools.partial(_sb_kernel, scale=math.log2(math.e) / math.sqrt(SB_DH)),
        grid=(b, ng, s // bq),
        in_specs=[
            pl.BlockSpec((1, bq, wg), lambda bi, g, qi: (bi, qi, g)),
            pl.BlockSpec((1, s, wg), lambda bi, g, qi: (bi, 0, ng + g), pipeline_mode=pl.Buffered(1)),
            pl.BlockSpec((1, s, wg), lambda bi, g, qi: (bi, 0, 2 * ng + g), pipeline_mode=pl.Buffered(1)),
        ],
        out_specs=pl.BlockSpec((1, bq, wg), lambda bi, g, qi: (bi, qi, g)),
        out_shape=jax.ShapeDtypeStruct((b, s, SB_W), BF16),
        compiler_params=_cparams("parallel", "parallel", "arbitrary"),
        name="sb_attention",
    )(proj3, proj3, proj3)


def _mla_prep_kernel(dq_ref, dkv_ref, c_ref, s_ref, gq_ref, gkv_ref, wab_ref, wkv_ref,
                     qcat_ref, kcat_ref, v_ref, *, scale):
    cmap = c_ref[...]
    smap = s_ref[...]
    dq = dq_ref[...].astype(F32)
    cq = dq * lax.rsqrt(jnp.mean(dq * dq, axis=-1, keepdims=True) + EPS) * gq_ref[...]
    qab = jnp.dot(cq.astype(BF16), wab_ref[...], preferred_element_type=F32)
    swap0 = MLA_HEADS * MLA_QK_PAD
    for h in range(MLA_HEADS):
        lo = h * MLA_QK_PAD
        qcat_ref[:, lo:lo + LANES] = (qab[:, lo:lo + LANES] * scale).astype(BF16)
        rope = qab[:, lo + LANES:lo + 2 * LANES] * cmap + qab[:, swap0 + h * LANES:swap0 + (h + 1) * LANES] * smap
        qcat_ref[:, lo + LANES:lo + 2 * LANES] = (rope * scale).astype(BF16)
    dkv = dkv_ref[...].astype(F32)
    ckv_in = dkv[:, :MLA_KV_RANK]
    ckv = ckv_in * lax.rsqrt(jnp.mean(ckv_in * ckv_in, axis=-1, keepdims=True) + EPS) * gkv_ref[...]
    kv = jnp.dot(ckv.astype(BF16), wkv_ref[...], preferred_element_type=F32)
    kr = (dkv[:, MLA_KV_RANK:MLA_KV_RANK + LANES] * cmap
          + dkv[:, MLA_KV_RANK + LANES:MLA_KV_RANK + 2 * LANES] * smap).astype(BF16)
    for h in range(MLA_HEADS):
        lo = h * MLA_QK_PAD
        kcat_ref[:, lo:lo + LANES] = kv[:, h * LANES:(h + 1) * LANES].astype(BF16)
        kcat_ref[:, lo + LANES:lo + 2 * LANES] = kr
    v_ref[...] = kv[:, MLA_HEADS * MLA_NOPE:].astype(BF16)


def _mla_prep(proj, cmap, smap, gq, gkv, wab, wkv, tt=256):
    t = proj.shape[0]
    tt = min(tt, t)
    wq = MLA_HEADS * MLA_QK_PAD
    wv = MLA_HEADS * MLA_DV
    return pl.pallas_call(
        functools.partial(_mla_prep_kernel, scale=math.log2(math.e) / math.sqrt(MLA_NOPE + MLA_ROPE)),
        grid=(t // tt,),
        in_specs=[
            pl.BlockSpec((tt, MLA_Q_RANK), lambda i: (i, COL_DQ // MLA_Q_RANK)),
            pl.BlockSpec((tt, DKV_PAD), lambda i: (i, COL_DKV // DKV_PAD)),
            pl.BlockSpec((tt, LANES), lambda i: (i, 0)),
            pl.BlockSpec((tt, LANES), lambda i: (i, 0)),
            pl.BlockSpec((1, MLA_Q_RANK), lambda i: (0, 0)),
            pl.BlockSpec((1, MLA_KV_RANK), lambda i: (0, 0)),
            pl.BlockSpec(wab.shape, lambda i: (0, 0)),
            pl.BlockSpec(wkv.shape, lambda i: (0, 0)),
        ],
        out_specs=[
            pl.BlockSpec((tt, wq), lambda i: (i, 0)),
            pl.BlockSpec((tt, wq), lambda i: (i, 0)),
            pl.BlockSpec((tt, wv), lambda i: (i, 0)),
        ],
        out_shape=[
            jax.ShapeDtypeStruct((t, wq), BF16),
            jax.ShapeDtypeStruct((t, wq), BF16),
            jax.ShapeDtypeStruct((t, wv), BF16),
        ],
        compiler_params=_cparams("parallel"),
        name="mla_prep",
    )(proj, proj, cmap, smap, gq.reshape(1, -1).astype(F32), gkv.reshape(1, -1).astype(F32), wab, wkv)


MLA_BQ = 512
MLA_BK = 512
MLA_HG = 4


def _mla_kernel(q_ref, k_ref, v_ref, o_ref):
    qi = pl.program_id(2)
    bq = q_ref.shape[1]
    bk = min(MLA_BK, bq)
    heads = range(MLA_HG)
    qs = [q_ref[0, :, h * MLA_QK_PAD:(h + 1) * MLA_QK_PAD] for h in heads]
    kpos = lax.broadcasted_iota(jnp.int32, (bk, bq), 0)
    qpos = qi * bq + lax.broadcasted_iota(jnp.int32, (bk, bq), 1)

    def block(kb, ms, ls, accs, masked):
        start = pl.multiple_of(kb * bk, bk)
        ss = [_nt_dot(k_ref[0, pl.ds(start, bk), h * MLA_QK_PAD:(h + 1) * MLA_QK_PAD], qs[h]) for h in heads]
        if masked:
            allowed = (kb * bk + kpos) <= qpos
            ss = [jnp.where(allowed, s, NEG) for s in ss]
        new_ms = [jnp.maximum(ms[h], jnp.max(ss[h], axis=0, keepdims=True)) for h in heads]
        ps = [jnp.exp2(ss[h] - new_ms[h]) for h in heads]
        alphas = [jnp.exp2(ms[h] - new_ms[h]) for h in heads]
        new_ls = [alphas[h] * ls[h] + jnp.sum(ps[h], axis=0, keepdims=True) for h in heads]
        pvs = [lax.dot_general(v_ref[0, pl.ds(start, bk), h * MLA_DV:(h + 1) * MLA_DV], ps[h].astype(BF16),
                               (((0,), (0,)), ((), ())), preferred_element_type=F32) for h in heads]
        new_accs = [alphas[h] * accs[h] + pvs[h] for h in heads]
        return tuple(new_ms), tuple(new_ls), tuple(new_accs)

    ms = tuple(jnp.full((1, bq), NEG, F32) for _ in heads)
    ls = tuple(jnp.zeros((1, bq), F32) for _ in heads)
    accs = tuple(jnp.zeros((MLA_DV, bq), F32) for _ in heads)
    nd = bq // bk
    ms, ls, accs = lax.fori_loop(0, qi * nd, lambda kb, cr: block(kb, cr[0], cr[1], cr[2], False), (ms, ls, accs))
    for d in range(nd):
        ms, ls, accs = block(qi * nd + d, ms, ls, accs, True)
    for h in heads:
        o_ref[0, :, h * MLA_DV:(h + 1) * MLA_DV] = (accs[h] / ls[h]).T.astype(o_ref.dtype)


def _mla_attention(qcat3, kcat3, v3):
    b, s, _ = qcat3.shape
    bq = min(MLA_BQ, s)
    wq = MLA_HG * MLA_QK_PAD
    wv = MLA_HG * MLA_DV
    return pl.pallas_call(
        _mla_kernel,
        grid=(b, MLA_HEADS // MLA_HG, s // bq),
        in_specs=[
            pl.BlockSpec((1, bq, wq), lambda bi, g, qi: (bi, qi, g)),
            pl.BlockSpec((1, s, wq), lambda bi, g, qi: (bi, 0, g), pipeline_mode=pl.Buffered(1)),
            pl.BlockSpec((1, s, wv), lambda bi, g, qi: (bi, 0, g), pipeline_mode=pl.Buffered(1)),
        ],
        out_specs=pl.BlockSpec((1, bq, wv), lambda bi, g, qi: (bi, qi, g)),
        out_shape=jax.ShapeDtypeStruct((b, s, MLA_HEADS * MLA_DV), BF16),
        compiler_params=_cparams("parallel", "parallel", "arbitrary"),
        name="mla_attention",
    )(qcat3, kcat3, v3)


def _mem_kernel(q_ref, k_ref, v_ref, o_ref, *, scale):
    s = _nt_dot(q_ref[0], k_ref[0]) * scale
    m = jnp.max(s, axis=-1, keepdims=True)
    p = jnp.exp(s - m)
    p = p / jnp.sum(p, axis=-1, keepdims=True)
    o_ref[0] = jnp.dot(p.astype(BF16), v_ref[0], preferred_element_type=F32).astype(o_ref.dtype)


def _mem_attention(proj3, mkv3, tt=512):
    b, s, _ = proj3.shape
    n_mem = mkv3.shape[1]
    tt = min(tt, s)
    q0 = COL_MEMQ // MEM_DH
    return pl.pallas_call(
        functools.partial(_mem_kernel, scale=1.0 / math.sqrt(MEM_DH)),
        grid=(b, s // tt, MEM_HEADS),
        in_specs=[
            pl.BlockSpec((1, tt, MEM_DH), lambda bi, ti, h: (bi, ti, q0 + h)),
            pl.BlockSpec((1, n_mem, MEM_DH), lambda bi, ti, h: (bi, 0, h)),
            pl.BlockSpec((1, n_mem, MEM_DH), lambda bi, ti, h: (bi, 0, MEM_HEADS + h)),
        ],
        out_specs=pl.BlockSpec((1, tt, MEM_DH), lambda bi, ti, h: (bi, ti, h)),
        out_shape=jax.ShapeDtypeStruct((b, s, MEM_W), BF16),
        compiler_params=_cparams("parallel", "parallel", "parallel"),
        name="mem_attention",
    )(proj3, mkv3, mkv3)


def _merge_kernel(ya_ref, yb_ref, yc_ref, ga_ref, gb_ref, gc_ref, w_ref, o_ref):
    acc = jax.nn.sigmoid(ga_ref[...].astype(F32)) * jnp.dot(ya_ref[...], w_ref[0], preferred_element_type=F32)
    acc += jax.nn.sigmoid(gb_ref[...].astype(F32)) * jnp.dot(yb_ref[...], w_ref[1], preferred_element_type=F32)
    acc += jax.nn.sigmoid(gc_ref[...].astype(F32)) * jnp.dot(yc_ref[...], w_ref[2], preferred_element_type=F32)
    o_ref[...] = acc.astype(o_ref.dtype)


def _merge(y_sb, y_mla, y_mem, proj, wb, d, tm=1024, tn=512):
    t = y_sb.shape[0]
    tm, tn = min(tm, t), min(tn, d)
    g0 = COL_GATE // tn
    gstep = d // tn

    def gate_spec(br):
        return pl.BlockSpec((tm, tn), lambda j, i: (i, g0 + br * gstep + j))

    yspec = pl.BlockSpec((tm, BRANCH_W), lambda j, i: (i, 0))
    return pl.pallas_call(
        _merge_kernel,
        grid=(d // tn, t // tm),
        in_specs=[yspec, yspec, yspec, gate_spec(0), gate_spec(1), gate_spec(2),
                  pl.BlockSpec((N_BRANCH, BRANCH_W, tn), lambda j, i: (0, 0, j))],
        out_specs=pl.BlockSpec((tm, tn), lambda j, i: (i, j)),
        out_shape=jax.ShapeDtypeStruct((t, d), BF16),
        compiler_params=_cparams("parallel", "parallel"),
        name="branch_merge",
    )(y_sb, y_mla, y_mem, proj, proj, proj, wb)


PEER_EB = 512
PEER_GROUP = PEER_EB // PEER_NKEYS
PEER_RANK_NONE = 255.0
PEER_CAND_WIDTH = [PEER_TOPK // (a + 1) for a in range(PEER_TOPK)]


def _take_top(vals, n, with_rank=False):
    out = []
    rem = vals
    rank = jnp.full(vals.shape, PEER_RANK_NONE, F32) if with_rank else None
    for k in range(n):
        mx = jnp.max(rem, axis=0, keepdims=True)
        out.append(mx)
        hit = rem == mx
        if with_rank:
            rank = jnp.where(hit, float(k), rank)
        rem = jnp.where(hit, -jnp.inf, rem)
    return (out, rank) if with_rank else out


def _peer_route_kernel(q_ref, keys_ref, nw_ref, r1_ref, e1_ref):
    def head(h, carry):
        s0 = _nt_dot(keys_ref[h, 0], q_ref[2 * h].astype(BF16))
        s1 = _nt_dot(keys_ref[h, 1], q_ref[2 * h + 1].astype(BF16))
        top0 = _take_top(s0, PEER_TOPK)
        top1, rank1 = _take_top(s1, PEER_TOPK, with_rank=True)
        cand = [[top0[a] + top1[b] for b in range(PEER_CAND_WIDTH[a])] for a in range(PEER_TOPK)]
        flat = [c for row in cand for c in row]
        pad = -len(flat) % 8
        stacked = jnp.concatenate(flat + [jnp.full_like(flat[0], -jnp.inf)] * pad, axis=0)
        ctop = _take_top(stacked, PEER_TOPK)
        cmax, tau = ctop[0], ctop[PEER_TOPK - 1]
        zsum = jnp.zeros_like(cmax)
        for cv in ctop:
            zsum = zsum + jnp.exp(cv - cmax)
        n0 = jnp.zeros_like(s0)
        for a in range(PEER_TOPK):
            count = jnp.zeros_like(tau)
            for c in cand[a]:
                count = count + jnp.where(c >= tau, 1.0, 0.0)
            n0 = jnp.where(s0 == top0[a], count, n0)
        w0 = jnp.exp(s0 - top0[0]) / zsum
        for g in range(PEER_NKEYS // PEER_GROUP):
            keys = slice(g * PEER_GROUP, (g + 1) * PEER_GROUP)
            nw_ref[g, h, :PEER_GROUP, :] = n0[keys]
            nw_ref[g, h, PEER_GROUP:, :] = w0[keys]
        r1_ref[h] = pltpu.bitcast(rank1.astype(BF16), jnp.uint32)
        e1_ref[h] = pltpu.bitcast(jnp.exp(s1 - top1[0]).astype(BF16), jnp.uint32)
        return carry

    lax.fori_loop(0, PEER_HEADS, head, 0)


def _peer_route(q3, keys, tt=256):
    _, t, dh = q3.shape
    tt = min(tt, t)
    ngrp = PEER_NKEYS // PEER_GROUP
    nwspec = pl.BlockSpec((ngrp, PEER_HEADS, 2 * PEER_GROUP, tt), lambda i: (0, 0, 0, i))
    pspec = pl.BlockSpec((PEER_HEADS, PEER_NKEYS // 2, tt), lambda i: (0, 0, i))
    packed = jax.ShapeDtypeStruct((PEER_HEADS, PEER_NKEYS // 2, t), jnp.uint32)
    return pl.pallas_call(
        _peer_route_kernel,
        grid=(t // tt,),
        in_specs=[pl.BlockSpec((2 * PEER_HEADS, tt, dh), lambda i: (0, i, 0)),
                  pl.BlockSpec(keys.shape, lambda i: (0, 0, 0, 0))],
        out_specs=[nwspec, pspec, pspec],
        out_shape=[jax.ShapeDtypeStruct((ngrp, PEER_HEADS, 2 * PEER_GROUP, t), F32), packed, packed],
        compiler_params=_cparams("parallel"),
        name="peer_route",
    )(q3, keys)


PEER_ROWS = 16


def _peer_dense_kernel(xn_ref, u_ref, v_ref, nw_ref, r1_ref, e1_ref, o_ref, at_ref):
    e = pl.program_id(1)

    @pl.when(e == 0)
    def _():
        o_ref[...] = jnp.zeros_like(o_ref)

    pre = _nt_dot(u_ref[...], xn_ref[...])
    at_ref[...] = (0.5 * pre * (1.0 + lax.erf(pre * (1.0 / math.sqrt(2.0))))).astype(BF16)
    tt = at_ref.shape[1]
    zero = jnp.zeros((PEER_ROWS, LANES), BF16)
    for ii in range(PEER_GROUP):
        for l0 in range(0, tt, LANES):
            lanes = slice(l0, l0 + LANES)
            counts = [jnp.broadcast_to(nw_ref[0, h, ii:ii + 1, lanes], (PEER_ROWS, LANES)).astype(BF16)
                      for h in range(PEER_HEADS)]
            weights = [jnp.broadcast_to(nw_ref[0, h, PEER_GROUP + ii:PEER_GROUP + ii + 1, lanes],
                                        (PEER_ROWS, LANES)).astype(BF16) for h in range(PEER_HEADS)]
            for r0 in range(0, PEER_NKEYS, PEER_ROWS):
                words = slice(r0 // 2, (r0 + PEER_ROWS) // 2)
                gate = zero
                for h in range(PEER_HEADS):
                    rank = pltpu.bitcast(r1_ref[h, words, lanes], BF16)
                    e1 = pltpu.bitcast(e1_ref[h, words, lanes], BF16)
                    gate = gate + jnp.where(rank < counts[h], e1 * weights[h], zero)
                e0 = ii * PEER_NKEYS + r0
                at_ref[e0:e0 + PEER_ROWS, lanes] = at_ref[e0:e0 + PEER_ROWS, lanes] * gate
    o_ref[...] += lax.dot_general(at_ref[...], v_ref[...], (((0,), (0,)), ((), ())), preferred_element_type=F32)


def _peer_dense(xn, u, v, nw, r1, e1, tt=512):
    t, d = xn.shape
    n = u.shape[0]
    tt = min(tt, t)
    hspec = pl.BlockSpec((PEER_HEADS, PEER_NKEYS // 2, tt), lambda ti, e: (0, 0, ti))
    return pl.pallas_call(
        _peer_dense_kernel,
        grid=(t // tt, n // PEER_EB),
        in_specs=[
            pl.BlockSpec((tt, d), lambda ti, e: (ti, 0)),
            pl.BlockSpec((PEER_EB, d), lambda ti, e: (e, 0)),
            pl.BlockSpec((PEER_EB, d), lambda ti, e: (e, 0)),
            pl.BlockSpec((1, PEER_HEADS, 2 * PEER_GROUP, tt), lambda ti, e: (e, 0, 0, ti)),
            hspec,
            hspec,
        ],
        out_specs=pl.BlockSpec((tt, d), lambda ti, e: (ti, 0)),
        out_shape=jax.ShapeDtypeStruct((t, d), F32),
        scratch_shapes=[pltpu.VMEM((PEER_EB, tt), BF16)],
        compiler_params=_cparams("parallel", "arbitrary"),
        name="peer_dense",
    )(xn, u, v, nw, r1, e1)


IN_TN = 512
IN_ROW_ALIGN = 64


def _in_proj_kernel(a_ref, w_ref, o_ref, wb_ref, *, kr_block):
    j = pl.program_id(0)
    i = pl.program_id(1)
    half = MLA_ROPE // 2
    kr0 = (COL_DKV + MLA_KV_RANK) % IN_TN

    @pl.when(jnp.logical_and(i == 0, j != kr_block))
    def _():
        wb_ref[...] = w_ref[...].astype(BF16)

    @pl.when(jnp.logical_and(i == 0, j == kr_block))
    def _():
        wb_ref[...] = jnp.zeros_like(wb_ref)
        wb_ref[:kr0 + MLA_ROPE, :] = w_ref[:kr0 + MLA_ROPE, :].astype(BF16)
        wb_ref[kr0 + LANES:kr0 + LANES + half, :] = w_ref[kr0 + half:kr0 + MLA_ROPE, :].astype(BF16)
        wb_ref[kr0 + LANES + half:kr0 + LANES + MLA_ROPE, :] = w_ref[kr0:kr0 + half, :].astype(BF16)

    o_ref[...] = _nt_dot(a_ref[...], wb_ref[...]).astype(o_ref.dtype)


def _in_proj(xn, w_in_t, d, tm=1024):
    t = xn.shape[0]
    c3 = 3 * SB_W + MLA_Q_RANK + MLA_KV_RANK + MLA_ROPE
    n_cols = COL_GATE + N_BRANCH * d
    kr_block = (COL_DKV + MLA_KV_RANK) // IN_TN
    assert (kr_block + 1) * IN_TN == COL_MEMQ and (COL_DKV + MLA_KV_RANK) % IN_TN + 2 * LANES <= IN_TN
    assert (n_cols - COL_MEMQ) % IN_TN == 0 and w_in_t.shape[0] - c3 == n_cols - COL_MEMQ and c3 % IN_ROW_ALIGN == 0
    tm = min(tm, t)

    def row0(j):
        return pl.multiple_of(jnp.where(j <= kr_block, j * IN_TN, c3 + (j - kr_block - 1) * IN_TN), IN_ROW_ALIGN)

    return pl.pallas_call(
        functools.partial(_in_proj_kernel, kr_block=kr_block),
        grid=(n_cols // IN_TN, t // tm),
        in_specs=[pl.BlockSpec((tm, d), lambda j, i: (i, 0)),
                  pl.BlockSpec((pl.Element(IN_TN), pl.Element(d)), lambda j, i: (row0(j), 0))],
        out_specs=pl.BlockSpec((tm, IN_TN), lambda j, i: (i, j)),
        out_shape=jax.ShapeDtypeStruct((t, n_cols), BF16),
        scratch_shapes=[pltpu.VMEM((IN_TN, d), BF16)],
        compiler_params=_cparams("arbitrary", "arbitrary"),
        name="in_proj",
    )(xn, w_in_t)


def _pack_mla_weights(w_uq, w_ukv):
    half = MLA_ROPE // 2
    rq = w_uq.shape[0]
    wq = w_uq.reshape(rq, MLA_HEADS, MLA_NOPE + MLA_ROPE)
    nope, t1, t2 = wq[:, :, :MLA_NOPE], wq[:, :, MLA_NOPE:MLA_NOPE + half], wq[:, :, MLA_NOPE + half:]
    zq = jnp.zeros((rq, MLA_HEADS, LANES - MLA_ROPE), w_uq.dtype)
    w_a = jnp.concatenate([nope, t1, t2, zq], axis=2).reshape(rq, MLA_HEADS * MLA_QK_PAD)
    w_b = jnp.concatenate([t2, t1, zq], axis=2).reshape(rq, MLA_HEADS * LANES)
    wab = jnp.concatenate([w_a, w_b], axis=1).astype(BF16)
    rkv = w_ukv.shape[0]
    wkv = w_ukv.reshape(rkv, MLA_HEADS, MLA_NOPE + MLA_DV)
    wkv = jnp.concatenate([wkv[:, :, :MLA_NOPE].reshape(rkv, -1), wkv[:, :, MLA_NOPE:].reshape(rkv, -1)], axis=1)
    return wab, wkv.astype(BF16)


def _rope_maps(positions):
    half = MLA_ROPE // 2
    freqs = ROPE_BASE ** (-jnp.arange(half, dtype=F32) / half)
    ang = positions.astype(F32).reshape(-1)[:, None] * freqs
    cos, sin = jnp.cos(ang), jnp.sin(ang)
    z = jnp.zeros((ang.shape[0], LANES - MLA_ROPE), F32)
    return jnp.concatenate([cos, cos, z], axis=1), jnp.concatenate([-sin, sin, z], axis=1)


def _layer(h, mem, cmap, smap, g_mix, w_in, mla_g_q, mla_w_uq, mla_g_kv, mla_w_ukv, g_mem, w_mem_kv,
           w_branch, w_out, g_ffn, peer_w_q, peer_sub_keys, peer_u, peer_v):
    b, s, d = h.shape
    t = b * s
    n_mem = mem.shape[1]
    h2 = h.reshape(t, d)

    xn = _rmsnorm(h2, g_mix, BF16)
    proj = _in_proj(xn, w_in.T, d)
    proj3 = proj.reshape(b, s, -1)

    y_sb = _sb_attention(proj3).reshape(t, SB_W)

    wab, wkv = _pack_mla_weights(mla_w_uq, mla_w_ukv)
    qcat, kcat, v = _mla_prep(proj, cmap, smap, mla_g_q, mla_g_kv, wab, wkv)
    y_mla = _mla_attention(qcat.reshape(b, s, -1), kcat.reshape(b, s, -1), v.reshape(b, s, -1)).reshape(t, -1)

    mem_n = _rmsnorm(mem.reshape(b * n_mem, d), g_mem, BF16)
    mkv = _matmul(mem_n, w_mem_kv.astype(BF16), BF16, tm=512, tn=1024, name="mem_kv")
    y_mem = _mem_attention(proj3, mkv.reshape(b, n_mem, 2 * MEM_W)).reshape(t, MEM_W)

    merged = _merge(y_sb, y_mla, y_mem, proj, w_branch.astype(BF16), d)
    h1 = _matmul(merged, w_out.astype(BF16), F32, tm=512, tn=1024, residual=h2, name="out_proj")

    xn2 = _rmsnorm(h1, g_ffn, BF16)
    q3 = _matmul_split(xn2, peer_w_q.astype(BF16), F32, tm=512, tn=1024, name="peer_q")
    nw, r1, e1 = _peer_route(q3, peer_sub_keys.astype(BF16))
    y = _peer_dense(xn2, peer_u.astype(BF16), peer_v.astype(BF16), nw, r1, e1)
    return h1, y


def kernel(x, mem, positions, g_mix, w_in, mla_g_q, mla_w_uq, mla_g_kv, mla_w_ukv, g_mem, w_mem_kv, w_branch,
           w_out, g_ffn, peer_w_q, peer_sub_keys, peer_u, peer_v, g_final):
    b, s, d = x.shape
    cmap, smap = _rope_maps(positions)
    depth = w_in.shape[0]
    h = x
    for layer in range(depth):
        h1, y = _layer(h, mem, cmap, smap, g_mix[layer], w_in[layer], mla_g_q[layer], mla_w_uq[layer],
                       mla_g_kv[layer], mla_w_ukv[layer], g_mem[layer], w_mem_kv[layer], w_branch[layer],
                       w_out[layer], g_ffn[layer], peer_w_q[layer], peer_sub_keys[layer], peer_u[layer],
                       peer_v[layer])
        if layer + 1 < depth:
            h = (h1 + y).reshape(b, s, d)
    return _add_rmsnorm(h1, y, g_final).reshape(b, s, d)
```

```python
import functools
import math

import jax
import jax.numpy as jnp
from jax import lax
from jax.experimental import pallas as pl
from jax.experimental.pallas import tpu as pltpu

F32 = jnp.float32
BF16 = jnp.bfloat16

EPS = 1e-6
NEG = -1e30
LANES = 128

SB_HEADS = 16
SB_DH = 128
SB_W = SB_HEADS * SB_DH
MLA_HEADS = 16
MLA_Q_RANK = 1024
MLA_KV_RANK = 512
MLA_NOPE = 128
MLA_ROPE = 64
MLA_DV = 128
MLA_QK_PAD = 256
ROPE_BASE = 10000.0
MEM_HEADS = 4
MEM_DH = 512
MEM_W = MEM_HEADS * MEM_DH
N_BRANCH = 3
BRANCH_W = 2048
PEER_HEADS = 8
PEER_NKEYS = 128
PEER_DK = 256
PEER_TOPK = 16

COL_SB = 0
COL_DQ = 3 * SB_W
COL_DKV = COL_DQ + MLA_Q_RANK
DKV_PAD = 1024
COL_MEMQ = COL_DKV + DKV_PAD
COL_GATE = COL_MEMQ + MEM_W

VMEM_LIMIT_BYTES = 56 * 1024 * 1024


def _cparams(*sem):
    return pltpu.CompilerParams(dimension_semantics=sem, vmem_limit_bytes=VMEM_LIMIT_BYTES)


def _nt_dot(a, b):
    return lax.dot_general(a, b, (((1,), (1,)), ((), ())), preferred_element_type=F32)


def _rmsnorm_kernel(x_ref, g_ref, o_ref):
    x = x_ref[...].astype(F32)
    ms = jnp.mean(x * x, axis=-1, keepdims=True)
    o_ref[...] = (x * lax.rsqrt(ms + EPS) * g_ref[...]).astype(o_ref.dtype)


def _rmsnorm(x, g, out_dtype, tm=256):
    m, d = x.shape
    tm = min(tm, m)
    return pl.pallas_call(
        _rmsnorm_kernel,
        grid=(m // tm,),
        in_specs=[pl.BlockSpec((tm, d), lambda i: (i, 0)), pl.BlockSpec((1, d), lambda i: (0, 0))],
        out_specs=pl.BlockSpec((tm, d), lambda i: (i, 0)),
        out_shape=jax.ShapeDtypeStruct((m, d), out_dtype),
        compiler_params=_cparams("parallel"),
        name="rmsnorm",
    )(x, g.reshape(1, d).astype(F32))


def _add_rmsnorm_kernel(a_ref, b_ref, g_ref, o_ref):
    x = a_ref[...] + b_ref[...]
    ms = jnp.mean(x * x, axis=-1, keepdims=True)
    o_ref[...] = (x * lax.rsqrt(ms + EPS) * g_ref[...]).astype(o_ref.dtype)


def _add_rmsnorm(a, b, g, tm=256):
    m, d = a.shape
    tm = min(tm, m)
    return pl.pallas_call(
        _add_rmsnorm_kernel,
        grid=(m // tm,),
        in_specs=[pl.BlockSpec((tm, d), lambda i: (i, 0)), pl.BlockSpec((tm, d), lambda i: (i, 0)),
                  pl.BlockSpec((1, d), lambda i: (0, 0))],
        out_specs=pl.BlockSpec((tm, d), lambda i: (i, 0)),
        out_shape=jax.ShapeDtypeStruct((m, d), F32),
        compiler_params=_cparams("parallel"),
        name="add_rmsnorm",
    )(a, b, g.reshape(1, d).astype(F32))


def _mm_kernel(a_ref, b_ref, o_ref):
    o_ref[...] = jnp.dot(a_ref[...], b_ref[...], preferred_element_type=F32).astype(o_ref.dtype)


def _mm_res_kernel(a_ref, b_ref, r_ref, o_ref):
    acc = jnp.dot(a_ref[...], b_ref[...], preferred_element_type=F32)
    o_ref[...] = (acc + r_ref[...]).astype(o_ref.dtype)


def _mm_split_kernel(a_ref, b_ref, o_ref):
    acc = jnp.dot(a_ref[...], b_ref[...], preferred_element_type=F32)
    for c in range(o_ref.shape[0]):
        o_ref[c] = acc[:, c * LANES:(c + 1) * LANES].astype(o_ref.dtype)


def _matmul_split(a, b, out_dtype, tm, tn, name):
    m, k = a.shape
    _, n = b.shape
    tm, tn = min(tm, m), min(tn, n)
    return pl.pallas_call(
        _mm_split_kernel,
        grid=(n // tn, m // tm),
        in_specs=[pl.BlockSpec((tm, k), lambda j, i: (i, 0)), pl.BlockSpec((k, tn), lambda j, i: (0, j))],
        out_specs=pl.BlockSpec((tn // LANES, tm, LANES), lambda j, i: (j, i, 0)),
        out_shape=jax.ShapeDtypeStruct((n // LANES, m, LANES), out_dtype),
        compiler_params=_cparams("parallel", "parallel"),
        name=name,
    )(a, b)


def _matmul(a, b, out_dtype, tm, tn, residual=None, name="matmul"):
    m, k = a.shape
    _, n = b.shape
    tm, tn = min(tm, m), min(tn, n)
    in_specs = [pl.BlockSpec((tm, k), lambda j, i: (i, 0)), pl.BlockSpec((k, tn), lambda j, i: (0, j))]
    args = [a, b]
    body = _mm_kernel
    if residual is not None:
        in_specs.append(pl.BlockSpec((tm, tn), lambda j, i: (i, j)))
        args.append(residual)
        body = _mm_res_kernel
    return pl.pallas_call(
        body,
        grid=(n // tn, m // tm),
        in_specs=in_specs,
        out_specs=pl.BlockSpec((tm, tn), lambda j, i: (i, j)),
        out_shape=jax.ShapeDtypeStruct((m, n), out_dtype),
        compiler_params=_cparams("parallel", "parallel"),
        name=name,
    )(*args)


SB_BQ = 256
SB_BK = 128
SB_HG = 8
SB_PAIR = 2 * SB_DH
SB_STOP = -151.5


def _sb_kernel(q_ref, k_ref, v_ref, o_ref, *, scale):
    qi = pl.program_id(2)
    bq = q_ref.shape[1]
    pairs = range(SB_HG // 2)
    lo_half, hi_half = slice(0, SB_DH), slice(SB_DH, SB_PAIR)
    qs = [(q_ref[0, :, p * SB_PAIR:(p + 1) * SB_PAIR].astype(F32) * scale).astype(BF16) for p in pairs]
    r = lax.broadcasted_iota(jnp.int32, (2 * SB_BK, 2 * SB_BK), 0) & (SB_BK - 1)
    c = lax.broadcasted_iota(jnp.int32, (2 * SB_BK, 2 * SB_BK), 1)
    u2 = jnp.where((c >= SB_BK) | (r > c), 1.0, 0.0).astype(BF16)
    qpos = qi * bq + lax.broadcasted_iota(jnp.int32, (bq, SB_PAIR), 0)
    kofs = lax.broadcasted_iota(jnp.int32, (bq, SB_PAIR), 1) & (SB_BK - 1)
    zero_blk = jnp.zeros((SB_BK, SB_DH), BF16)

    def blockdiag(x):
        return jnp.concatenate([jnp.concatenate([x[:, lo_half], zero_blk], axis=1),
                                jnp.concatenate([zero_blk, x[:, hi_half]], axis=1)], axis=0)

    def block(kb, runs, accs, masked):
        start = pl.multiple_of(kb * SB_BK, SB_BK)
        if masked:
            strict = (kb * SB_BK + kofs) < qpos
        zs = [_nt_dot(qs[p], blockdiag(k_ref[0, pl.ds(start, SB_BK), p * SB_PAIR:(p + 1) * SB_PAIR])) for p in pairs]
        lks = [-(jnp.maximum(z, 0.0) + jnp.log2(1.0 + jnp.exp2(-jnp.abs(z)))) for z in zs]
        lkms = [jnp.where(strict, lk, 0.0) for lk in lks] if masked else lks
        his = [lkm.astype(BF16) for lkm in lkms]
        los = [(lkm - hi.astype(F32)).astype(BF16) for lkm, hi in zip(lkms, his)]
        css = [[jnp.dot(jnp.concatenate([his[p][:, half], los[p][:, half]], axis=1), u2, preferred_element_type=F32)
                for half in (lo_half, hi_half)] for p in pairs]
        sufs = [jnp.concatenate([css[p][0][:, :SB_BK], css[p][1][:, :SB_BK]], axis=1) for p in pairs]
        tots = [jnp.concatenate([css[p][0][:, SB_BK:], css[p][1][:, SB_BK:]], axis=1) for p in pairs]
        ws = [jnp.exp2(zs[p] + lks[p] + runs[p] + sufs[p]) for p in pairs]
        if masked:
            ws = [jnp.where(strict, w, 0.0) for w in ws]
        new_accs = [accs[p] + jnp.dot(ws[p].astype(BF16),
                                      blockdiag(v_ref[0, pl.ds(start, SB_BK), p * SB_PAIR:(p + 1) * SB_PAIR]),
                                      preferred_element_type=F32) for p in pairs]
        new_runs = [runs[p] + tots[p] for p in pairs]
        return tuple(new_runs), tuple(new_accs)

    def least_decayed(runs):
        mx = jnp.max(runs[0])
        for p in pairs[1:]:
            mx = jnp.maximum(mx, jnp.max(runs[p]))
        return mx

    runs = tuple(jnp.zeros((bq, SB_PAIR), F32) for _ in pairs)
    accs = tuple(jnp.zeros((bq, SB_PAIR), F32) for _ in pairs)
    nd = bq // SB_BK
    for d in range(nd):
        runs, accs = block(qi * nd + (nd - 1 - d), runs, accs, True)

    def cond(state):
        kb, mx, _, _ = state
        return jnp.logical_and(kb >= 0, mx > SB_STOP)

    def body(state):
        kb, _, rs, ac = state
        rs, ac = block(kb, rs, ac, False)
        return kb - 1, least_decayed(rs), rs, ac

    _, _, _, accs = lax.while_loop(cond, body, (qi * nd - 1, least_decayed(runs), runs, accs))
    for p in pairs:
        o_ref[0, :, p * SB_PAIR:(p + 1) * SB_PAIR] = accs[p].astype(o_ref.dtype)


def _sb_attention(proj3):
    b, s, _ = proj3.shape
    bq = min(SB_BQ, s)
    ng = SB_HEADS // SB_HG
    wg = SB_HG * SB_DH
    return pl.pallas_call(
        functools.partial(_sb_kernel, scale=math.log2(math.e) / math.sqrt(SB_DH)),
        grid=(b, ng, s // bq),
        in_specs=[
            pl.BlockSpec((1, bq, wg), lambda bi, g, qi: (bi, qi, g)),
            pl.BlockSpec((1, s, wg), lambda bi, g, qi: (bi, 0, ng + g), pipeline_mode=pl.Buffered(1)),
            pl.BlockSpec((1, s, wg), lambda bi, g, qi: (bi, 0, 2 * ng + g), pipeline_mode=pl.Buffered(1)),
        ],
        out_specs=pl.BlockSpec((1, bq, wg), lambda bi, g, qi: (bi, qi, g)),
        out_shape=jax.ShapeDtypeStruct((b, s, SB_W), BF16),
        compiler_params=_cparams("parallel", "parallel", "arbitrary"),
        name="sb_attention",
    )(proj3, proj3, proj3)


def _mla_prep_kernel(dq_ref, dkv_ref, c_ref, s_ref, gq_ref, gkv_ref, wab_ref, wkv_ref,
                     qcat_ref, kcat_ref, v_ref, *, scale):
    cmap = c_ref[...]
    smap = s_ref[...]
    dq = dq_ref[...].astype(F32)
    cq = dq * lax.rsqrt(jnp.mean(dq * dq, axis=-1, keepdims=True) + EPS) * gq_ref[...]
    qab = jnp.dot(cq.astype(BF16), wab_ref[...], preferred_element_type=F32)
    swap0 = MLA_HEADS * MLA_QK_PAD
    for h in range(MLA_HEADS):
        lo = h * MLA_QK_PAD
        qcat_ref[:, lo:lo + LANES] = (qab[:, lo:lo + LANES] * scale).astype(BF16)
        rope = qab[:, lo + LANES:lo + 2 * LANES] * cmap + qab[:, swap0 + h * LANES:swap0 + (h + 1) * LANES] * smap
        qcat_ref[:, lo + LANES:lo + 2 * LANES] = (rope * scale).astype(BF16)
    dkv = dkv_ref[...].astype(F32)
    ckv_in = dkv[:, :MLA_KV_RANK]
    ckv = ckv_in * lax.rsqrt(jnp.mean(ckv_in * ckv_in, axis=-1, keepdims=True) + EPS) * gkv_ref[...]
    kv = jnp.dot(ckv.astype(BF16), wkv_ref[...], preferred_element_type=F32)
    kr = (dkv[:, MLA_KV_RANK:MLA_KV_RANK + LANES] * cmap
          + dkv[:, MLA_KV_RANK + LANES:MLA_KV_RANK + 2 * LANES] * smap).astype(BF16)
    for h in range(MLA_HEADS):
        lo = h * MLA_QK_PAD
        kcat_ref[:, lo:lo + LANES] = kv[:, h * LANES:(h + 1) * LANES].astype(BF16)
        kcat_ref[:, lo + LANES:lo + 2 * LANES] = kr
    v_ref[...] = kv[:, MLA_HEADS * MLA_NOPE:].astype(BF16)


def _mla_prep(proj, cmap, smap, gq, gkv, wab, wkv, tt=256):
    t = proj.shape[0]
    tt = min(tt, t)
    wq = MLA_HEADS * MLA_QK_PAD
    wv = MLA_HEADS * MLA_DV
    return pl.pallas_call(
        functools.partial(_mla_prep_kernel, scale=math.log2(math.e) / math.sqrt(MLA_NOPE + MLA_ROPE)),
        grid=(t // tt,),
        in_specs=[
            pl.BlockSpec((tt, MLA_Q_RANK), lambda i: (i, COL_DQ // MLA_Q_RANK)),
            pl.BlockSpec((tt, DKV_PAD), lambda i: (i, COL_DKV // DKV_PAD)),
            pl.BlockSpec((tt, LANES), lambda i: (i, 0)),
            pl.BlockSpec((tt, LANES), lambda i: (i, 0)),
            pl.BlockSpec((1, MLA_Q_RANK), lambda i: (0, 0)),
            pl.BlockSpec((1, MLA_KV_RANK), lambda i: (0, 0)),
            pl.BlockSpec(wab.shape, lambda i: (0, 0)),
            pl.BlockSpec(wkv.shape, lambda i: (0, 0)),
        ],
        out_specs=[
            pl.BlockSpec((tt, wq), lambda i: (i, 0)),
            pl.BlockSpec((tt, wq), lambda i: (i, 0)),
            pl.BlockSpec((tt, wv), lambda i: (i, 0)),
        ],
        out_shape=[
            jax.ShapeDtypeStruct((t, wq), BF16),
            jax.ShapeDtypeStruct((t, wq), BF16),
            jax.ShapeDtypeStruct((t, wv), BF16),
        ],
        compiler_params=_cparams("parallel"),
        name="mla_prep",
    )(proj, proj, cmap, smap, gq.reshape(1, -1).astype(F32), gkv.reshape(1, -1).astype(F32), wab, wkv)


MLA_BQ = 512
MLA_BK = 512
MLA_HG = 4


def _mla_kernel(q_ref, k_ref, v_ref, o_ref):
    qi = pl.program_id(2)
    bq = q_ref.shape[1]
    bk = min(MLA_BK, bq)
    heads = range(MLA_HG)
    qs = [q_ref[0, :, h * MLA_QK_PAD:(h + 1) * MLA_QK_PAD] for h in heads]
    kpos = lax.broadcasted_iota(jnp.int32, (bk, bq), 0)
    qpos = qi * bq + lax.broadcasted_iota(jnp.int32, (bk, bq), 1)

    def block(kb, ms, ls, accs, masked):
        start = pl.multiple_of(kb * bk, bk)
        ss = [_nt_dot(k_ref[0, pl.ds(start, bk), h * MLA_QK_PAD:(h + 1) * MLA_QK_PAD], qs[h]) for h in heads]
        if masked:
            allowed = (kb * bk + kpos) <= qpos
            ss = [jnp.where(allowed, s, NEG) for s in ss]
        new_ms = [jnp.maximum(ms[h], jnp.max(ss[h], axis=0, keepdims=True)) for h in heads]
        ps = [jnp.exp2(ss[h] - new_ms[h]) for h in heads]
        alphas = [jnp.exp2(ms[h] - new_ms[h]) for h in heads]
        new_ls = [alphas[h] * ls[h] + jnp.sum(ps[h], axis=0, keepdims=True) for h in heads]
        pvs = [lax.dot_general(v_ref[0, pl.ds(start, bk), h * MLA_DV:(h + 1) * MLA_DV], ps[h].astype(BF16),
                               (((0,), (0,)), ((), ())), preferred_element_type=F32) for h in heads]
        new_accs = [alphas[h] * accs[h] + pvs[h] for h in heads]
        return tuple(new_ms), tuple(new_ls), tuple(new_accs)

    ms = tuple(jnp.full((1, bq), NEG, F32) for _ in heads)
    ls = tuple(jnp.zeros((1, bq), F32) for _ in heads)
    accs = tuple(jnp.zeros((MLA_DV, bq), F32) for _ in heads)
    nd = bq // bk
    ms, ls, accs = lax.fori_loop(0, qi * nd, lambda kb, cr: block(kb, cr[0], cr[1], cr[2], False), (ms, ls, accs))
    for d in range(nd):
        ms, ls, accs = block(qi * nd + d, ms, ls, accs, True)
    for h in heads:
        o_ref[0, :, h * MLA_DV:(h + 1) * MLA_DV] = (accs[h] / ls[h]).T.astype(o_ref.dtype)


def _mla_attention(qcat3, kcat3, v3):
    b, s, _ = qcat3.shape
    bq = min(MLA_BQ, s)
    wq = MLA_HG * MLA_QK_PAD
    wv = MLA_HG * MLA_DV
    return pl.pallas_call(
        _mla_kernel,
        grid=(b, MLA_HEADS // MLA_HG, s // bq),
        in_specs=[
            pl.BlockSpec((1, bq, wq), lambda bi, g, qi: (bi, qi, g)),
            pl.BlockSpec((1, s, wq), lambda bi, g, qi: (bi, 0, g), pipeline_mode=pl.Buffered(1)),
            pl.BlockSpec((1, s, wv), lambda bi, g, qi: (bi, 0, g), pipeline_mode=pl.Buffered(1)),
        ],
        out_specs=pl.BlockSpec((1, bq, wv), lambda bi, g, qi: (bi, qi, g)),
        out_shape=jax.ShapeDtypeStruct((b, s, MLA_HEADS * MLA_DV), BF16),
        compiler_params=_cparams("parallel", "parallel", "arbitrary"),
        name="mla_attention",
    )(qcat3, kcat3, v3)


def _mem_kernel(q_ref, kv_ref, o_ref, *, scale):
    for h in range(MEM_HEADS):
        cols = slice(h * MEM_DH, (h + 1) * MEM_DH)
        s = _nt_dot(q_ref[0, :, cols], kv_ref[0, :, cols]) * scale
        m = jnp.max(s, axis=-1, keepdims=True)
        p = jnp.exp(s - m)
        p = p / jnp.sum(p, axis=-1, keepdims=True)
        vals = kv_ref[0, :, MEM_W + h * MEM_DH:MEM_W + (h + 1) * MEM_DH]
        o_ref[0, :, cols] = jnp.dot(p.astype(BF16), vals, preferred_element_type=F32).astype(o_ref.dtype)


def _mem_attention(proj3, mkv3, tt=512):
    b, s, _ = proj3.shape
    n_mem = mkv3.shape[1]
    tt = min(tt, s)
    return pl.pallas_call(
        functools.partial(_mem_kernel, scale=1.0 / math.sqrt(MEM_DH)),
        grid=(b, s // tt),
        in_specs=[
            pl.BlockSpec((1, tt, MEM_W), lambda bi, ti: (bi, ti, COL_MEMQ // MEM_W)),
            pl.BlockSpec((1, n_mem, 2 * MEM_W), lambda bi, ti: (bi, 0, 0)),
        ],
        out_specs=pl.BlockSpec((1, tt, MEM_W), lambda bi, ti: (bi, ti, 0)),
        out_shape=jax.ShapeDtypeStruct((b, s, MEM_W), BF16),
        compiler_params=_cparams("parallel", "parallel"),
        name="mem_attention",
    )(proj3, mkv3)


def _merge_kernel(ya_ref, yb_ref, yc_ref, ga_ref, gb_ref, gc_ref, w_ref, o_ref):
    acc = jax.nn.sigmoid(ga_ref[...].astype(F32)) * jnp.dot(ya_ref[...], w_ref[0], preferred_element_type=F32)
    acc += jax.nn.sigmoid(gb_ref[...].astype(F32)) * jnp.dot(yb_ref[...], w_ref[1], preferred_element_type=F32)
    acc += jax.nn.sigmoid(gc_ref[...].astype(F32)) * jnp.dot(yc_ref[...], w_ref[2], preferred_element_type=F32)
    o_ref[...] = acc.astype(o_ref.dtype)


def _merge(y_sb, y_mla, y_mem, proj, wb, d, tm=1024, tn=512):
    t = y_sb.shape[0]
    tm, tn = min(tm, t), min(tn, d)
    g0 = COL_GATE // tn
    gstep = d // tn

    def gate_spec(br):
        return pl.BlockSpec((tm, tn), lambda j, i: (i, g0 + br * gstep + j))

    yspec = pl.BlockSpec((tm, BRANCH_W), lambda j, i: (i, 0))
    return pl.pallas_call(
        _merge_kernel,
        grid=(d // tn, t // tm),
        in_specs=[yspec, yspec, yspec, gate_spec(0), gate_spec(1), gate_spec(2),
                  pl.BlockSpec((N_BRANCH, BRANCH_W, tn), lambda j, i: (0, 0, j))],
        out_specs=pl.BlockSpec((tm, tn), lambda j, i: (i, j)),
        out_shape=jax.ShapeDtypeStruct((t, d), BF16),
        compiler_params=_cparams("parallel", "parallel"),
        name="branch_merge",
    )(y_sb, y_mla, y_mem, proj, proj, proj, wb)


PEER_EB = 512
PEER_GROUP = PEER_EB // PEER_NKEYS
PEER_RANK_NONE = 255.0
PEER_CAND_WIDTH = [PEER_TOPK // (a + 1) for a in range(PEER_TOPK)]


def _take_top(vals, n, with_rank=False):
    out = []
    rem = vals
    rank = jnp.full(vals.shape, PEER_RANK_NONE, F32) if with_rank else None
    for k in range(n):
        mx = jnp.max(rem, axis=0, keepdims=True)
        out.append(mx)
        hit = rem == mx
        if with_rank:
            rank = jnp.where(hit, float(k), rank)
        rem = jnp.where(hit, -jnp.inf, rem)
    return (out, rank) if with_rank else out


def _peer_route_kernel(q_ref, keys_ref, nw_ref, r1_ref, e1_ref):
    def head(h):
        s0 = _nt_dot(keys_ref[h, 0], q_ref[2 * h].astype(BF16))
        s1 = _nt_dot(keys_ref[h, 1], q_ref[2 * h + 1].astype(BF16))
        top0 = _take_top(s0, PEER_TOPK)
        top1, rank1 = _take_top(s1, PEER_TOPK, with_rank=True)
        cand = [[top0[a] + top1[b] for b in range(PEER_CAND_WIDTH[a])] for a in range(PEER_TOPK)]
        flat = [c for row in cand for c in row]
        pad = -len(flat) % 8
        stacked = jnp.concatenate(flat + [jnp.full_like(flat[0], -jnp.inf)] * pad, axis=0)
        ctop = _take_top(stacked, PEER_TOPK)
        cmax, tau = ctop[0], ctop[PEER_TOPK - 1]
        zsum = jnp.zeros_like(cmax)
        for cv in ctop:
            zsum = zsum + jnp.exp(cv - cmax)
        n0 = jnp.zeros_like(s0)
        for a in range(PEER_TOPK):
            count = jnp.zeros_like(tau)
            for c in cand[a]:
                count = count + jnp.where(c >= tau, 1.0, 0.0)
            n0 = jnp.where(s0 == top0[a], count, n0)
        w0 = jnp.exp(s0 - top0[0]) / zsum
        for g in range(PEER_NKEYS // PEER_GROUP):
            keys = slice(g * PEER_GROUP, (g + 1) * PEER_GROUP)
            nw_ref[g, h, :PEER_GROUP, :] = n0[keys]
            nw_ref[g, h, PEER_GROUP:, :] = w0[keys]
        r1_ref[h] = pltpu.bitcast(rank1.astype(BF16), jnp.uint32)
        e1_ref[h] = pltpu.bitcast(jnp.exp(s1 - top1[0]).astype(BF16), jnp.uint32)

    def head_pair(hp, carry):
        head(2 * hp)
        head(2 * hp + 1)
        return carry

    lax.fori_loop(0, PEER_HEADS // 2, head_pair, 0)


def _peer_route(q3, keys, tt=256):
    _, t, dh = q3.shape
    tt = min(tt, t)
    ngrp = PEER_NKEYS // PEER_GROUP
    nwspec = pl.BlockSpec((ngrp, PEER_HEADS, 2 * PEER_GROUP, tt), lambda i: (0, 0, 0, i))
    pspec = pl.BlockSpec((PEER_HEADS, PEER_NKEYS // 2, tt), lambda i: (0, 0, i))
    packed = jax.ShapeDtypeStruct((PEER_HEADS, PEER_NKEYS // 2, t), jnp.uint32)
    return pl.pallas_call(
        _peer_route_kernel,
        grid=(t // tt,),
        in_specs=[pl.BlockSpec((2 * PEER_HEADS, tt, dh), lambda i: (0, i, 0)),
                  pl.BlockSpec(keys.shape, lambda i: (0, 0, 0, 0))],
        out_specs=[nwspec, pspec, pspec],
        out_shape=[jax.ShapeDtypeStruct((ngrp, PEER_HEADS, 2 * PEER_GROUP, t), F32), packed, packed],
        compiler_params=_cparams("parallel"),
        name="peer_route",
    )(q3, keys)


PEER_ROWS = 16


def _peer_dense_kernel(xn_ref, u_ref, v_ref, nw_ref, r1_ref, e1_ref, o_ref, at_ref):
    e = pl.program_id(1)

    @pl.when(e == 0)
    def _():
        o_ref[...] = jnp.zeros_like(o_ref)

    pre = _nt_dot(u_ref[...], xn_ref[...])
    at_ref[...] = (0.5 * pre * (1.0 + lax.erf(pre * (1.0 / math.sqrt(2.0))))).astype(BF16)
    tt = at_ref.shape[1]
    zero = jnp.zeros((PEER_ROWS, LANES), BF16)
    for ii in range(PEER_GROUP):
        for l0 in range(0, tt, LANES):
            lanes = slice(l0, l0 + LANES)
            counts = [jnp.broadcast_to(nw_ref[0, h, ii:ii + 1, lanes], (PEER_ROWS, LANES)).astype(BF16)
                      for h in range(PEER_HEADS)]
            weights = [jnp.broadcast_to(nw_ref[0, h, PEER_GROUP + ii:PEER_GROUP + ii + 1, lanes],
                                        (PEER_ROWS, LANES)).astype(BF16) for h in range(PEER_HEADS)]
            for r0 in range(0, PEER_NKEYS, PEER_ROWS):
                words = slice(r0 // 2, (r0 + PEER_ROWS) // 2)
                gate = zero
                for h in range(PEER_HEADS):
                    rank = pltpu.bitcast(r1_ref[h, words, lanes], BF16)
                    e1 = pltpu.bitcast(e1_ref[h, words, lanes], BF16)
                    gate = gate + jnp.where(rank < counts[h], e1 * weights[h], zero)
                e0 = ii * PEER_NKEYS + r0
                at_ref[e0:e0 + PEER_ROWS, lanes] = at_ref[e0:e0 + PEER_ROWS, lanes] * gate
    o_ref[...] += lax.dot_general(at_ref[...], v_ref[...], (((0,), (0,)), ((), ())), preferred_element_type=F32)


def _peer_dense(xn, u, v, nw, r1, e1, tt=512):
    t, d = xn.shape
    n = u.shape[0]
    tt = min(tt, t)
    hspec = pl.BlockSpec((PEER_HEADS, PEER_NKEYS // 2, tt), lambda ti, e: (0, 0, ti))
    return pl.pallas_call(
        _peer_dense_kernel,
        grid=(t // tt, n // PEER_EB),
        in_specs=[
            pl.BlockSpec((tt, d), lambda ti, e: (ti, 0)),
            pl.BlockSpec((PEER_EB, d), lambda ti, e: (e, 0)),
            pl.BlockSpec((PEER_EB, d), lambda ti, e: (e, 0)),
            pl.BlockSpec((1, PEER_HEADS, 2 * PEER_GROUP, tt), lambda ti, e: (e, 0, 0, ti)),
            hspec,
            hspec,
        ],
        out_specs=pl.BlockSpec((tt, d), lambda ti, e: (ti, 0)),
        out_shape=jax.ShapeDtypeStruct((t, d), F32),
        scratch_shapes=[pltpu.VMEM((PEER_EB, tt), BF16)],
        compiler_params=_cparams("parallel", "arbitrary"),
        name="peer_dense",
    )(xn, u, v, nw, r1, e1)


IN_TN = 512
IN_ROW_ALIGN = 64


def _in_proj_kernel(a_ref, w_ref, o_ref, wb_ref, *, kr_block):
    j = pl.program_id(0)
    i = pl.program_id(1)
    half = MLA_ROPE // 2
    kr0 = (COL_DKV + MLA_KV_RANK) % IN_TN

    @pl.when(jnp.logical_and(i == 0, j != kr_block))
    def _():
        wb_ref[...] = w_ref[...].astype(BF16)

    @pl.when(jnp.logical_and(i == 0, j == kr_block))
    def _():
        wb_ref[...] = jnp.zeros_like(wb_ref)
        wb_ref[:kr0 + MLA_ROPE, :] = w_ref[:kr0 + MLA_ROPE, :].astype(BF16)
        wb_ref[kr0 + LANES:kr0 + LANES + half, :] = w_ref[kr0 + half:kr0 + MLA_ROPE, :].astype(BF16)
        wb_ref[kr0 + LANES + half:kr0 + LANES + MLA_ROPE, :] = w_ref[kr0:kr0 + half, :].astype(BF16)

    o_ref[...] = _nt_dot(a_ref[...], wb_ref[...]).astype(o_ref.dtype)


def _in_proj(xn, w_in_t, d, tm=1024):
    t = xn.shape[0]
    c3 = 3 * SB_W + MLA_Q_RANK + MLA_KV_RANK + MLA_ROPE
    n_cols = COL_GATE + N_BRANCH * d
    kr_block = (COL_DKV + MLA_KV_RANK) // IN_TN
    assert (kr_block + 1) * IN_TN == COL_MEMQ and (COL_DKV + MLA_KV_RANK) % IN_TN + 2 * LANES <= IN_TN
    assert (n_cols - COL_MEMQ) % IN_TN == 0 and w_in_t.shape[0] - c3 == n_cols - COL_MEMQ and c3 % IN_ROW_ALIGN == 0
    tm = min(tm, t)

    def row0(j):
        return pl.multiple_of(jnp.where(j <= kr_block, j * IN_TN, c3 + (j - kr_block - 1) * IN_TN), IN_ROW_ALIGN)

    return pl.pallas_call(
        functools.partial(_in_proj_kernel, kr_block=kr_block),
        grid=(n_cols // IN_TN, t // tm),
        in_specs=[pl.BlockSpec((tm, d), lambda j, i: (i, 0)),
                  pl.BlockSpec((pl.Element(IN_TN), pl.Element(d)), lambda j, i: (row0(j), 0))],
        out_specs=pl.BlockSpec((tm, IN_TN), lambda j, i: (i, j)),
        out_shape=jax.ShapeDtypeStruct((t, n_cols), BF16),
        scratch_shapes=[pltpu.VMEM((IN_TN, d), BF16)],
        compiler_params=_cparams("arbitrary", "arbitrary"),
        name="in_proj",
    )(xn, w_in_t)


def _pack_mla_weights(w_uq, w_ukv):
    half = MLA_ROPE // 2
    rq = w_uq.shape[0]
    wq = w_uq.reshape(rq, MLA_HEADS, MLA_NOPE + MLA_ROPE)
    nope, t1, t2 = wq[:, :, :MLA_NOPE], wq[:, :, MLA_NOPE:MLA_NOPE + half], wq[:, :, MLA_NOPE + half:]
    zq = jnp.zeros((rq, MLA_HEADS, LANES - MLA_ROPE), w_uq.dtype)
    w_a = jnp.concatenate([nope, t1, t2, zq], axis=2).reshape(rq, MLA_HEADS * MLA_QK_PAD)
    w_b = jnp.concatenate([t2, t1, zq], axis=2).reshape(rq, MLA_HEADS * LANES)
    wab = jnp.concatenate([w_a, w_b], axis=1).astype(BF16)
    rkv = w_ukv.shape[0]
    wkv = w_ukv.reshape(rkv, MLA_HEADS, MLA_NOPE + MLA_DV)
    wkv = jnp.concatenate([wkv[:, :, :MLA_NOPE].reshape(rkv, -1), wkv[:, :, MLA_NOPE:].reshape(rkv, -1)], axis=1)
    return wab, wkv.astype(BF16)


def _rope_maps(positions):
    half = MLA_ROPE // 2
    freqs = ROPE_BASE ** (-jnp.arange(half, dtype=F32) / half)
    ang = positions.astype(F32).reshape(-1)[:, None] * freqs
    cos, sin = jnp.cos(ang), jnp.sin(ang)
    z = jnp.zeros((ang.shape[0], LANES - MLA_ROPE), F32)
    return jnp.concatenate([cos, cos, z], axis=1), jnp.concatenate([-sin, sin, z], axis=1)


def _layer(h, mem, cmap, smap, g_mix, w_in, mla_g_q, mla_w_uq, mla_g_kv, mla_w_ukv, g_mem, w_mem_kv,
           w_branch, w_out, g_ffn, peer_w_q, peer_sub_keys, peer_u, peer_v):
    b, s, d = h.shape
    t = b * s
    n_mem = mem.shape[1]
    h2 = h.reshape(t, d)

    xn = _rmsnorm(h2, g_mix, BF16)
    proj = _in_proj(xn, w_in.T, d)
    proj3 = proj.reshape(b, s, -1)

    y_sb = _sb_attention(proj3).reshape(t, SB_W)

    wab, wkv = _pack_mla_weights(mla_w_uq, mla_w_ukv)
    qcat, kcat, v = _mla_prep(proj, cmap, smap, mla_g_q, mla_g_kv, wab, wkv)
    y_mla = _mla_attention(qcat.reshape(b, s, -1), kcat.reshape(b, s, -1), v.reshape(b, s, -1)).reshape(t, -1)

    mem_n = _rmsnorm(mem.reshape(b * n_mem, d), g_mem, BF16)
    mkv = _matmul(mem_n, w_mem_kv.astype(BF16), BF16, tm=512, tn=1024, name="mem_kv")
    y_mem = _mem_attention(proj3, mkv.reshape(b, n_mem, 2 * MEM_W)).reshape(t, MEM_W)

    merged = _merge(y_sb, y_mla, y_mem, proj, w_branch.astype(BF16), d)
    h1 = _matmul(merged, w_out.astype(BF16), F32, tm=512, tn=1024, residual=h2, name="out_proj")

    xn2 = _rmsnorm(h1, g_ffn, BF16)
    q3 = _matmul_split(xn2, peer_w_q.astype(BF16), F32, tm=512, tn=1024, name="peer_q")
    nw, r1, e1 = _peer_route(q3, peer_sub_keys.astype(BF16))
    y = _peer_dense(xn2, peer_u.astype(BF16), peer_v.astype(BF16), nw, r1, e1)
    return h1, y


def kernel(x, mem, positions, g_mix, w_in, mla_g_q, mla_w_uq, mla_g_kv, mla_w_ukv, g_mem, w_mem_kv, w_branch,
           w_out, g_ffn, peer_w_q, peer_sub_keys, peer_u, peer_v, g_final):
    b, s, d = x.shape
    cmap, smap = _rope_maps(positions)
    depth = w_in.shape[0]
    h = x
    for layer in range(depth):
        h1, y = _layer(h, mem, cmap, smap, g_mix[layer], w_in[layer], mla_g_q[layer], mla_w_uq[layer],
                       mla_g_kv[layer], mla_w_ukv[layer], g_mem[layer], w_mem_kv[layer], w_branch[layer],
                       w_out[layer], g_ffn[layer], peer_w_q[layer], peer_sub_keys[layer], peer_u[layer],
                       peer_v[layer])
        if layer + 1 < depth:
            h = (h1 + y).reshape(b, s, d)
    return _add_rmsnorm(h1, y, g_final).reshape(b, s, d)
```

```python
import functools
import math

import jax
import jax.numpy as jnp
from jax import lax
from jax.experimental import pallas as pl
from jax.experimental.pallas import tpu as pltpu

F32 = jnp.float32
BF16 = jnp.bfloat16

EPS = 1e-6
NEG = -1e30
LANES = 128

SB_HEADS = 16
SB_DH = 128
SB_W = SB_HEADS * SB_DH
MLA_HEADS = 16
MLA_Q_RANK = 1024
MLA_KV_RANK = 512
MLA_NOPE = 128
MLA_ROPE = 64
MLA_DV = 128
MLA_QK_PAD = 256
ROPE_BASE = 10000.0
MEM_HEADS = 4
MEM_DH = 512
MEM_W = MEM_HEADS * MEM_DH
N_BRANCH = 3
BRANCH_W = 2048
PEER_HEADS = 8
PEER_NKEYS = 128
PEER_DK = 256
PEER_TOPK = 16

COL_DQ = 3 * SB_W
COL_DKV = COL_DQ + MLA_Q_RANK
DKV_PAD = 1024
COL_MEMQ = COL_DKV + DKV_PAD
COL_GATE = COL_MEMQ + MEM_W

VMEM_LIMIT_BYTES = 56 * 1024 * 1024


def _cparams(*sem):
    return pltpu.CompilerParams(dimension_semantics=sem, vmem_limit_bytes=VMEM_LIMIT_BYTES)


def _nt_dot(a, b):
    return lax.dot_general(a, b, (((1,), (1,)), ((), ())), preferred_element_type=F32)


def _rmsnorm_kernel(x_ref, g_ref, o_ref):
    x = x_ref[...].astype(F32)
    ms = jnp.mean(x * x, axis=-1, keepdims=True)
    o_ref[...] = (x * lax.rsqrt(ms + EPS) * g_ref[...]).astype(o_ref.dtype)


def _rmsnorm(x, g, out_dtype, tm=256):
    m, d = x.shape
    tm = min(tm, m)
    return pl.pallas_call(
        _rmsnorm_kernel,
        grid=(m // tm,),
        in_specs=[pl.BlockSpec((tm, d), lambda i: (i, 0)), pl.BlockSpec((1, d), lambda i: (0, 0))],
        out_specs=pl.BlockSpec((tm, d), lambda i: (i, 0)),
        out_shape=jax.ShapeDtypeStruct((m, d), out_dtype),
        compiler_params=_cparams("parallel"),
        name="rmsnorm",
    )(x, g.reshape(1, d).astype(F32))


def _add_rmsnorm_kernel(a_ref, b_ref, g_ref, o_ref):
    x = a_ref[...] + b_ref[...]
    ms = jnp.mean(x * x, axis=-1, keepdims=True)
    o_ref[...] = (x * lax.rsqrt(ms + EPS) * g_ref[...]).astype(o_ref.dtype)


def _add_rmsnorm(a, b, g, tm=256):
    m, d = a.shape
    tm = min(tm, m)
    return pl.pallas_call(
        _add_rmsnorm_kernel,
        grid=(m // tm,),
        in_specs=[pl.BlockSpec((tm, d), lambda i: (i, 0)), pl.BlockSpec((tm, d), lambda i: (i, 0)),
                  pl.BlockSpec((1, d), lambda i: (0, 0))],
        out_specs=pl.BlockSpec((tm, d), lambda i: (i, 0)),
        out_shape=jax.ShapeDtypeStruct((m, d), F32),
        compiler_params=_cparams("parallel"),
        name="add_rmsnorm",
    )(a, b, g.reshape(1, d).astype(F32))


def _mm_kernel(a_ref, b_ref, o_ref):
    o_ref[...] = jnp.dot(a_ref[...], b_ref[...], preferred_element_type=F32).astype(o_ref.dtype)


def _mm_res_kernel(a_ref, b_ref, r_ref, o_ref):
    acc = jnp.dot(a_ref[...], b_ref[...], preferred_element_type=F32)
    o_ref[...] = (acc + r_ref[...]).astype(o_ref.dtype)


def _mm_split_kernel(a_ref, b_ref, o_ref):
    acc = jnp.dot(a_ref[...], b_ref[...], preferred_element_type=F32)
    for c in range(o_ref.shape[0]):
        o_ref[c] = acc[:, c * LANES:(c + 1) * LANES].astype(o_ref.dtype)


def _matmul_split(a, b, out_dtype, tm, tn, name):
    m, k = a.shape
    _, n = b.shape
    tm, tn = min(tm, m), min(tn, n)
    return pl.pallas_call(
        _mm_split_kernel,
        grid=(n // tn, m // tm),
        in_specs=[pl.BlockSpec((tm, k), lambda j, i: (i, 0)), pl.BlockSpec((k, tn), lambda j, i: (0, j))],
        out_specs=pl.BlockSpec((tn // LANES, tm, LANES), lambda j, i: (j, i, 0)),
        out_shape=jax.ShapeDtypeStruct((n // LANES, m, LANES), out_dtype),
        compiler_params=_cparams("parallel", "parallel"),
        name=name,
    )(a, b)


def _matmul(a, b, out_dtype, tm, tn, residual=None, name="matmul"):
    m, k = a.shape
    _, n = b.shape
    tm, tn = min(tm, m), min(tn, n)
    in_specs = [pl.BlockSpec((tm, k), lambda j, i: (i, 0)), pl.BlockSpec((k, tn), lambda j, i: (0, j))]
    args = [a, b]
    body = _mm_kernel
    if residual is not None:
        in_specs.append(pl.BlockSpec((tm, tn), lambda j, i: (i, j)))
        args.append(residual)
        body = _mm_res_kernel
    return pl.pallas_call(
        body,
        grid=(n // tn, m // tm),
        in_specs=in_specs,
        out_specs=pl.BlockSpec((tm, tn), lambda j, i: (i, j)),
        out_shape=jax.ShapeDtypeStruct((m, n), out_dtype),
        compiler_params=_cparams("parallel", "parallel"),
        name=name,
    )(*args)


SB_BQ = 256
SB_BK = 128
SB_HG = 8
SB_PAIR = 2 * SB_DH
SB_STOP = -151.5


def _sb_kernel(q_ref, k_ref, v_ref, o_ref, *, scale):
    qi = pl.program_id(2)
    bq = q_ref.shape[1]
    pairs = range(SB_HG // 2)
    lo_half, hi_half = slice(0, SB_DH), slice(SB_DH, SB_PAIR)
    qs = [(q_ref[0, :, p * SB_PAIR:(p + 1) * SB_PAIR].astype(F32) * scale).astype(BF16) for p in pairs]
    r = lax.broadcasted_iota(jnp.int32, (2 * SB_BK, 2 * SB_BK), 0) & (SB_BK - 1)
    c = lax.broadcasted_iota(jnp.int32, (2 * SB_BK, 2 * SB_BK), 1)
    u2 = jnp.where((c >= SB_BK) | (r > c), 1.0, 0.0).astype(BF16)
    qpos = qi * bq + lax.broadcasted_iota(jnp.int32, (bq, SB_PAIR), 0)
    kofs = lax.broadcasted_iota(jnp.int32, (bq, SB_PAIR), 1) & (SB_BK - 1)
    zero_blk = jnp.zeros((SB_BK, SB_DH), BF16)

    def blockdiag(x):
        return jnp.concatenate([jnp.concatenate([x[:, lo_half], zero_blk], axis=1),
                                jnp.concatenate([zero_blk, x[:, hi_half]], axis=1)], axis=0)

    def block(kb, runs, accs, masked):
        start = pl.multiple_of(kb * SB_BK, SB_BK)
        if masked:
            strict = (kb * SB_BK + kofs) < qpos
        zs = [_nt_dot(qs[p], blockdiag(k_ref[0, pl.ds(start, SB_BK), p * SB_PAIR:(p + 1) * SB_PAIR])) for p in pairs]
        lks = [-(jnp.maximum(z, 0.0) + jnp.log2(1.0 + jnp.exp2(-jnp.abs(z)))) for z in zs]
        lkms = [jnp.where(strict, lk, 0.0) for lk in lks] if masked else lks
        his = [lkm.astype(BF16) for lkm in lkms]
        los = [(lkm - hi.astype(F32)).astype(BF16) for lkm, hi in zip(lkms, his)]
        css = [[jnp.dot(jnp.concatenate([his[p][:, half], los[p][:, half]], axis=1), u2, preferred_element_type=F32)
                for half in (lo_half, hi_half)] for p in pairs]
        sufs = [jnp.concatenate([css[p][0][:, :SB_BK], css[p][1][:, :SB_BK]], axis=1) for p in pairs]
        tots = [jnp.concatenate([css[p][0][:, SB_BK:], css[p][1][:, SB_BK:]], axis=1) for p in pairs]
        ws = [jnp.exp2(zs[p] + lks[p] + runs[p] + sufs[p]) for p in pairs]
        if masked:
            ws = [jnp.where(strict, w, 0.0) for w in ws]
        new_accs = [accs[p] + jnp.dot(ws[p].astype(BF16),
                                      blockdiag(v_ref[0, pl.ds(start, SB_BK), p * SB_PAIR:(p + 1) * SB_PAIR]),
                                      preferred_element_type=F32) for p in pairs]
        new_runs = [runs[p] + tots[p] for p in pairs]
        return tuple(new_runs), tuple(new_accs)

    def least_decayed(runs):
        mx = jnp.max(runs[0])
        for p in pairs[1:]:
            mx = jnp.maximum(mx, jnp.max(runs[p]))
        return mx

    runs = tuple(jnp.zeros((bq, SB_PAIR), F32) for _ in pairs)
    accs = tuple(jnp.zeros((bq, SB_PAIR), F32) for _ in pairs)
    nd = bq // SB_BK
    for d in range(nd):
        runs, accs = block(qi * nd + (nd - 1 - d), runs, accs, True)

    def cond(state):
        kb, mx, _, _ = state
        return jnp.logical_and(kb >= 0, mx > SB_STOP)

    def body(state):
        kb, _, rs, ac = state
        rs, ac = block(kb, rs, ac, False)
        return kb - 1, least_decayed(rs), rs, ac

    _, _, _, accs = lax.while_loop(cond, body, (qi * nd - 1, least_decayed(runs), runs, accs))
    for p in pairs:
        o_ref[0, :, p * SB_PAIR:(p + 1) * SB_PAIR] = accs[p].astype(o_ref.dtype)


def _sb_attention(proj3):
    b, s, _ = proj3.shape
    bq = min(SB_BQ, s)
    ng = SB_HEADS // SB_HG
    wg = SB_HG * SB_DH
    return pl.pallas_call(
        functools.partial(_sb_kernel, scale=math.log2(math.e) / math.sqrt(SB_DH)),
        grid=(b, ng, s // bq),
        in_specs=[
            pl.BlockSpec((1, bq, wg), lambda bi, g, qi: (bi, qi, g)),
            pl.BlockSpec((1, s, wg), lambda bi, g, qi: (bi, 0, ng + g), pipeline_mode=pl.Buffered(1)),
            pl.BlockSpec((1, s, wg), lambda bi, g, qi: (bi, 0, 2 * ng + g), pipeline_mode=pl.Buffered(1)),
        ],
        out_specs=pl.BlockSpec((1, bq, wg), lambda bi, g, qi: (bi, qi, g)),
        out_shape=jax.ShapeDtypeStruct((b, s, SB_W), BF16),
        compiler_params=_cparams("parallel", "parallel", "arbitrary"),
        name="sb_attention",
    )(proj3, proj3, proj3)


def _mla_prep_kernel(dq_ref, dkv_ref, c_ref, s_ref, gq_ref, gkv_ref, wab_ref, wkv_ref,
                     qcat_ref, kcat_ref, v_ref, *, scale):
    cmap = c_ref[...]
    smap = s_ref[...]
    dq = dq_ref[...].astype(F32)
    cq = dq * lax.rsqrt(jnp.mean(dq * dq, axis=-1, keepdims=True) + EPS) * gq_ref[...]
    qab = jnp.dot(cq.astype(BF16), wab_ref[...], preferred_element_type=F32)
    swap0 = MLA_HEADS * MLA_QK_PAD
    for h in range(MLA_HEADS):
        lo = h * MLA_QK_PAD
        qcat_ref[:, lo:lo + LANES] = (qab[:, lo:lo + LANES] * scale).astype(BF16)
        rope = qab[:, lo + LANES:lo + 2 * LANES] * cmap + qab[:, swap0 + h * LANES:swap0 + (h + 1) * LANES] * smap
        qcat_ref[:, lo + LANES:lo + 2 * LANES] = (rope * scale).astype(BF16)
    dkv = dkv_ref[...].astype(F32)
    ckv_in = dkv[:, :MLA_KV_RANK]
    ckv = ckv_in * lax.rsqrt(jnp.mean(ckv_in * ckv_in, axis=-1, keepdims=True) + EPS) * gkv_ref[...]
    kv = jnp.dot(ckv.astype(BF16), wkv_ref[...], preferred_element_type=F32)
    kr = (dkv[:, MLA_KV_RANK:MLA_KV_RANK + LANES] * cmap
          + dkv[:, MLA_KV_RANK + LANES:MLA_KV_RANK + 2 * LANES] * smap).astype(BF16)
    for h in range(MLA_HEADS):
        lo = h * MLA_QK_PAD
        kcat_ref[:, lo:lo + LANES] = kv[:, h * LANES:(h + 1) * LANES].astype(BF16)
        kcat_ref[:, lo + LANES:lo + 2 * LANES] = kr
    v_ref[...] = kv[:, MLA_HEADS * MLA_NOPE:].astype(BF16)


def _mla_prep(proj, cmap, smap, gq, gkv, wab, wkv, tt=256):
    t = proj.shape[0]
    tt = min(tt, t)
    wq = MLA_HEADS * MLA_QK_PAD
    wv = MLA_HEADS * MLA_DV
    return pl.pallas_call(
        functools.partial(_mla_prep_kernel, scale=math.log2(math.e) / math.sqrt(MLA_NOPE + MLA_ROPE)),
        grid=(t // tt,),
        in_specs=[
            pl.BlockSpec((tt, MLA_Q_RANK), lambda i: (i, COL_DQ // MLA_Q_RANK)),
            pl.BlockSpec((tt, DKV_PAD), lambda i: (i, COL_DKV // DKV_PAD)),
            pl.BlockSpec((tt, LANES), lambda i: (i, 0)),
            pl.BlockSpec((tt, LANES), lambda i: (i, 0)),
            pl.BlockSpec((1, MLA_Q_RANK), lambda i: (0, 0)),
            pl.BlockSpec((1, MLA_KV_RANK), lambda i: (0, 0)),
            pl.BlockSpec(wab.shape, lambda i: (0, 0)),
            pl.BlockSpec(wkv.shape, lambda i: (0, 0)),
        ],
        out_specs=[
            pl.BlockSpec((tt, wq), lambda i: (i, 0)),
            pl.BlockSpec((tt, wq), lambda i: (i, 0)),
            pl.BlockSpec((tt, wv), lambda i: (i, 0)),
        ],
        out_shape=[
            jax.ShapeDtypeStruct((t, wq), BF16),
            jax.ShapeDtypeStruct((t, wq), BF16),
            jax.ShapeDtypeStruct((t, wv), BF16),
        ],
        compiler_params=_cparams("parallel"),
        name="mla_prep",
    )(proj, proj, cmap, smap, gq.reshape(1, -1).astype(F32), gkv.reshape(1, -1).astype(F32), wab, wkv)


MLA_BQ = 512
MLA_BK = 512
MLA_HG = 4


def _mla_kernel(q_ref, k_ref, v_ref, o_ref):
    qi = pl.program_id(2)
    bq = q_ref.shape[1]
    bk = min(MLA_BK, bq)
    heads = range(MLA_HG)
    qs = [q_ref[0, :, h * MLA_QK_PAD:(h + 1) * MLA_QK_PAD] for h in heads]
    kpos = lax.broadcasted_iota(jnp.int32, (bk, bq), 0)
    qpos = qi * bq + lax.broadcasted_iota(jnp.int32, (bk, bq), 1)

    def block(kb, ms, ls, accs, masked):
        start = pl.multiple_of(kb * bk, bk)
        ss = [_nt_dot(k_ref[0, pl.ds(start, bk), h * MLA_QK_PAD:(h + 1) * MLA_QK_PAD], qs[h]) for h in heads]
        if masked:
            allowed = (kb * bk + kpos) <= qpos
            ss = [jnp.where(allowed, s, NEG) for s in ss]
        new_ms = [jnp.maximum(ms[h], jnp.max(ss[h], axis=0, keepdims=True)) for h in heads]
        ps = [jnp.exp2(ss[h] - new_ms[h]) for h in heads]
        alphas = [jnp.exp2(ms[h] - new_ms[h]) for h in heads]
        new_ls = [alphas[h] * ls[h] + jnp.sum(ps[h], axis=0, keepdims=True) for h in heads]
        pvs = [lax.dot_general(v_ref[0, pl.ds(start, bk), h * MLA_DV:(h + 1) * MLA_DV], ps[h].astype(BF16),
                               (((0,), (0,)), ((), ())), preferred_element_type=F32) for h in heads]
        new_accs = [alphas[h] * accs[h] + pvs[h] for h in heads]
        return tuple(new_ms), tuple(new_ls), tuple(new_accs)

    ms = tuple(jnp.full((1, bq), NEG, F32) for _ in heads)
    ls = tuple(jnp.zeros((1, bq), F32) for _ in heads)
    accs = tuple(jnp.zeros((MLA_DV, bq), F32) for _ in heads)
    nd = bq // bk
    ms, ls, accs = lax.fori_loop(0, qi * nd, lambda kb, cr: block(kb, cr[0], cr[1], cr[2], False), (ms, ls, accs))
    for d in range(nd):
        ms, ls, accs = block(qi * nd + d, ms, ls, accs, True)
    for h in heads:
        o_ref[0, :, h * MLA_DV:(h + 1) * MLA_DV] = (accs[h] / ls[h]).T.astype(o_ref.dtype)


def _mla_attention(qcat3, kcat3, v3):
    b, s, _ = qcat3.shape
    bq = min(MLA_BQ, s)
    wq = MLA_HG * MLA_QK_PAD
    wv = MLA_HG * MLA_DV
    return pl.pallas_call(
        _mla_kernel,
        grid=(b, MLA_HEADS // MLA_HG, s // bq),
        in_specs=[
            pl.BlockSpec((1, bq, wq), lambda bi, g, qi: (bi, qi, g)),
            pl.BlockSpec((1, s, wq), lambda bi, g, qi: (bi, 0, g), pipeline_mode=pl.Buffered(1)),
            pl.BlockSpec((1, s, wv), lambda bi, g, qi: (bi, 0, g), pipeline_mode=pl.Buffered(1)),
        ],
        out_specs=pl.BlockSpec((1, bq, wv), lambda bi, g, qi: (bi, qi, g)),
        out_shape=jax.ShapeDtypeStruct((b, s, MLA_HEADS * MLA_DV), BF16),
        compiler_params=_cparams("parallel", "parallel", "arbitrary"),
        name="mla_attention",
    )(qcat3, kcat3, v3)


def _mem_kernel(q_ref, kv_ref, o_ref, *, scale):
    for h in range(MEM_HEADS):
        cols = slice(h * MEM_DH, (h + 1) * MEM_DH)
        s = _nt_dot(q_ref[0, :, cols], kv_ref[0, :, cols]) * scale
        m = jnp.max(s, axis=-1, keepdims=True)
        p = jnp.exp(s - m)
        p = p / jnp.sum(p, axis=-1, keepdims=True)
        vals = kv_ref[0, :, MEM_W + h * MEM_DH:MEM_W + (h + 1) * MEM_DH]
        o_ref[0, :, cols] = jnp.dot(p.astype(BF16), vals, preferred_element_type=F32).astype(o_ref.dtype)


def _mem_attention(proj3, mkv3, tt=512):
    b, s, _ = proj3.shape
    n_mem = mkv3.shape[1]
    tt = min(tt, s)
    return pl.pallas_call(
        functools.partial(_mem_kernel, scale=1.0 / math.sqrt(MEM_DH)),
        grid=(b, s // tt),
        in_specs=[
            pl.BlockSpec((1, tt, MEM_W), lambda bi, ti: (bi, ti, COL_MEMQ // MEM_W)),
            pl.BlockSpec((1, n_mem, 2 * MEM_W), lambda bi, ti: (bi, 0, 0)),
        ],
        out_specs=pl.BlockSpec((1, tt, MEM_W), lambda bi, ti: (bi, ti, 0)),
        out_shape=jax.ShapeDtypeStruct((b, s, MEM_W), BF16),
        compiler_params=_cparams("parallel", "parallel"),
        name="mem_attention",
    )(proj3, mkv3)


def _merge_kernel(ya_ref, yb_ref, yc_ref, ga_ref, gb_ref, gc_ref, w_ref, o_ref):
    acc = jax.nn.sigmoid(ga_ref[...].astype(F32)) * jnp.dot(ya_ref[...], w_ref[0], preferred_element_type=F32)
    acc += jax.nn.sigmoid(gb_ref[...].astype(F32)) * jnp.dot(yb_ref[...], w_ref[1], preferred_element_type=F32)
    acc += jax.nn.sigmoid(gc_ref[...].astype(F32)) * jnp.dot(yc_ref[...], w_ref[2], preferred_element_type=F32)
    o_ref[...] = acc.astype(o_ref.dtype)


def _merge(y_sb, y_mla, y_mem, proj, wb, d, tm=1024, tn=512):
    t = y_sb.shape[0]
    tm, tn = min(tm, t), min(tn, d)
    g0 = COL_GATE // tn
    gstep = d // tn

    def gate_spec(br):
        return pl.BlockSpec((tm, tn), lambda j, i: (i, g0 + br * gstep + j))

    yspec = pl.BlockSpec((tm, BRANCH_W), lambda j, i: (i, 0))
    return pl.pallas_call(
        _merge_kernel,
        grid=(d // tn, t // tm),
        in_specs=[yspec, yspec, yspec, gate_spec(0), gate_spec(1), gate_spec(2),
                  pl.BlockSpec((N_BRANCH, BRANCH_W, tn), lambda j, i: (0, 0, j))],
        out_specs=pl.BlockSpec((tm, tn), lambda j, i: (i, j)),
        out_shape=jax.ShapeDtypeStruct((t, d), BF16),
        compiler_params=_cparams("parallel", "parallel"),
        name="branch_merge",
    )(y_sb, y_mla, y_mem, proj, proj, proj, wb)


PEER_EB = 512
PEER_GROUP = PEER_EB // PEER_NKEYS
PEER_ROUTE_UNROLL = 4
PEER_RANK_NONE = 255.0
PEER_CAND_WIDTH = [PEER_TOPK // (a + 1) for a in range(PEER_TOPK)]


def _take_top(vals, n, with_rank=False):
    out = []
    rem = vals
    rank = jnp.full(vals.shape, PEER_RANK_NONE, F32) if with_rank else None
    for k in range(n):
        mx = jnp.max(rem, axis=0, keepdims=True)
        out.append(mx)
        hit = rem == mx
        if with_rank:
            rank = jnp.where(hit, float(k), rank)
        rem = jnp.where(hit, -jnp.inf, rem)
    return (out, rank) if with_rank else out


def _peer_route_kernel(q_ref, keys_ref, nw_ref, r1_ref, e1_ref):
    def head(h):
        s0 = _nt_dot(keys_ref[h, 0], q_ref[2 * h].astype(BF16))
        s1 = _nt_dot(keys_ref[h, 1], q_ref[2 * h + 1].astype(BF16))
        top0 = _take_top(s0, PEER_TOPK)
        top1, rank1 = _take_top(s1, PEER_TOPK, with_rank=True)
        cand = [[top0[a] + top1[b] for b in range(PEER_CAND_WIDTH[a])] for a in range(PEER_TOPK)]
        flat = [c for row in cand for c in row]
        pad = -len(flat) % 8
        stacked = jnp.concatenate(flat + [jnp.full_like(flat[0], -jnp.inf)] * pad, axis=0)
        ctop = _take_top(stacked, PEER_TOPK)
        cmax, tau = ctop[0], ctop[PEER_TOPK - 1]
        zsum = jnp.zeros_like(cmax)
        for cv in ctop:
            zsum = zsum + jnp.exp(cv - cmax)
        n0 = jnp.zeros_like(s0)
        for a in range(PEER_TOPK):
            count = jnp.zeros_like(tau)
            for c in cand[a]:
                count = count + jnp.where(c >= tau, 1.0, 0.0)
            n0 = jnp.where(s0 == top0[a], count, n0)
        w0 = jnp.exp(s0 - top0[0]) / zsum
        for g in range(PEER_NKEYS // PEER_GROUP):
            keys = slice(g * PEER_GROUP, (g + 1) * PEER_GROUP)
            nw_ref[g, h, :PEER_GROUP, :] = n0[keys]
            nw_ref[g, h, PEER_GROUP:, :] = w0[keys]
        r1_ref[h] = pltpu.bitcast(rank1.astype(BF16), jnp.uint32)
        e1_ref[h] = pltpu.bitcast(jnp.exp(s1 - top1[0]).astype(BF16), jnp.uint32)

    def head_group(hg, carry):
        for k in range(PEER_ROUTE_UNROLL):
            head(PEER_ROUTE_UNROLL * hg + k)
        return carry

    lax.fori_loop(0, PEER_HEADS // PEER_ROUTE_UNROLL, head_group, 0)


def _peer_route(q3, keys, tt=256):
    _, t, dh = q3.shape
    tt = min(tt, t)
    ngrp = PEER_NKEYS // PEER_GROUP
    nwspec = pl.BlockSpec((ngrp, PEER_HEADS, 2 * PEER_GROUP, tt), lambda i: (0, 0, 0, i))
    pspec = pl.BlockSpec((PEER_HEADS, PEER_NKEYS // 2, tt), lambda i: (0, 0, i))
    packed = jax.ShapeDtypeStruct((PEER_HEADS, PEER_NKEYS // 2, t), jnp.uint32)
    return pl.pallas_call(
        _peer_route_kernel,
        grid=(t // tt,),
        in_specs=[pl.BlockSpec((2 * PEER_HEADS, tt, dh), lambda i: (0, i, 0)),
                  pl.BlockSpec(keys.shape, lambda i: (0, 0, 0, 0))],
        out_specs=[nwspec, pspec, pspec],
        out_shape=[jax.ShapeDtypeStruct((ngrp, PEER_HEADS, 2 * PEER_GROUP, t), F32), packed, packed],
        compiler_params=_cparams("parallel"),
        name="peer_route",
    )(q3, keys)


PEER_ROWS = 16


def _peer_dense_kernel(xn_ref, u_ref, v_ref, nw_ref, r1_ref, e1_ref, o_ref, at_ref):
    e = pl.program_id(1)

    @pl.when(e == 0)
    def _():
        o_ref[...] = jnp.zeros_like(o_ref)

    pre = _nt_dot(u_ref[...], xn_ref[...])
    at_ref[...] = (0.5 * pre * (1.0 + lax.erf(pre * (1.0 / math.sqrt(2.0))))).astype(BF16)
    tt = at_ref.shape[1]
    zero = jnp.zeros((PEER_ROWS, LANES), BF16)
    for ii in range(PEER_GROUP):
        for l0 in range(0, tt, LANES):
            lanes = slice(l0, l0 + LANES)
            counts = [jnp.broadcast_to(nw_ref[0, h, ii:ii + 1, lanes], (PEER_ROWS, LANES)).astype(BF16)
                      for h in range(PEER_HEADS)]
            weights = [jnp.broadcast_to(nw_ref[0, h, PEER_GROUP + ii:PEER_GROUP + ii + 1, lanes],
                                        (PEER_ROWS, LANES)).astype(BF16) for h in range(PEER_HEADS)]
            for r0 in range(0, PEER_NKEYS, PEER_ROWS):
                words = slice(r0 // 2, (r0 + PEER_ROWS) // 2)
                gate = zero
                for h in range(PEER_HEADS):
                    rank = pltpu.bitcast(r1_ref[h, words, lanes], BF16)
                    e1 = pltpu.bitcast(e1_ref[h, words, lanes], BF16)
                    gate = gate + jnp.where(rank < counts[h], e1 * weights[h], zero)
                e0 = ii * PEER_NKEYS + r0
                at_ref[e0:e0 + PEER_ROWS, lanes] = at_ref[e0:e0 + PEER_ROWS, lanes] * gate
    o_ref[...] += lax.dot_general(at_ref[...], v_ref[...], (((0,), (0,)), ((), ())), preferred_element_type=F32)


def _peer_dense(xn, u, v, nw, r1, e1, tt=512):
    t, d = xn.shape
    n = u.shape[0]
    tt = min(tt, t)
    hspec = pl.BlockSpec((PEER_HEADS, PEER_NKEYS // 2, tt), lambda ti, e: (0, 0, ti))
    return pl.pallas_call(
        _peer_dense_kernel,
        grid=(t // tt, n // PEER_EB),
        in_specs=[
            pl.BlockSpec((tt, d), lambda ti, e: (ti, 0)),
            pl.BlockSpec((PEER_EB, d), lambda ti, e: (e, 0)),
            pl.BlockSpec((PEER_EB, d), lambda ti, e: (e, 0)),
            pl.BlockSpec((1, PEER_HEADS, 2 * PEER_GROUP, tt), lambda ti, e: (e, 0, 0, ti)),
            hspec,
            hspec,
        ],
        out_specs=pl.BlockSpec((tt, d), lambda ti, e: (ti, 0)),
        out_shape=jax.ShapeDtypeStruct((t, d), F32),
        scratch_shapes=[pltpu.VMEM((PEER_EB, tt), BF16)],
        compiler_params=_cparams("parallel", "arbitrary"),
        name="peer_dense",
    )(xn, u, v, nw, r1, e1)


IN_TN = 512
IN_ROW_ALIGN = 64


def _in_proj_kernel(a_ref, w_ref, o_ref, wb_ref, *, kr_block):
    j = pl.program_id(0)
    i = pl.program_id(1)
    half = MLA_ROPE // 2
    kr0 = (COL_DKV + MLA_KV_RANK) % IN_TN

    @pl.when(jnp.logical_and(i == 0, j != kr_block))
    def _():
        wb_ref[...] = w_ref[...].astype(BF16)

    @pl.when(jnp.logical_and(i == 0, j == kr_block))
    def _():
        wb_ref[...] = jnp.zeros_like(wb_ref)
        wb_ref[:kr0 + MLA_ROPE, :] = w_ref[:kr0 + MLA_ROPE, :].astype(BF16)
        wb_ref[kr0 + LANES:kr0 + LANES + half, :] = w_ref[kr0 + half:kr0 + MLA_ROPE, :].astype(BF16)
        wb_ref[kr0 + LANES + half:kr0 + LANES + MLA_ROPE, :] = w_ref[kr0:kr0 + half, :].astype(BF16)

    o_ref[...] = _nt_dot(a_ref[...], wb_ref[...]).astype(o_ref.dtype)


def _in_proj(xn, w_in_t, d, tm=1024):
    t = xn.shape[0]
    c3 = 3 * SB_W + MLA_Q_RANK + MLA_KV_RANK + MLA_ROPE
    n_cols = COL_GATE + N_BRANCH * d
    kr_block = (COL_DKV + MLA_KV_RANK) // IN_TN
    assert (kr_block + 1) * IN_TN == COL_MEMQ and (COL_DKV + MLA_KV_RANK) % IN_TN + 2 * LANES <= IN_TN
    assert (n_cols - COL_MEMQ) % IN_TN == 0 and w_in_t.shape[0] - c3 == n_cols - COL_MEMQ and c3 % IN_ROW_ALIGN == 0
    tm = min(tm, t)

    def row0(j):
        return pl.multiple_of(jnp.where(j <= kr_block, j * IN_TN, c3 + (j - kr_block - 1) * IN_TN), IN_ROW_ALIGN)

    return pl.pallas_call(
        functools.partial(_in_proj_kernel, kr_block=kr_block),
        grid=(n_cols // IN_TN, t // tm),
        in_specs=[pl.BlockSpec((tm, d), lambda j, i: (i, 0)),
                  pl.BlockSpec((pl.Element(IN_TN), pl.Element(d)), lambda j, i: (row0(j), 0))],
        out_specs=pl.BlockSpec((tm, IN_TN), lambda j, i: (i, j)),
        out_shape=jax.ShapeDtypeStruct((t, n_cols), BF16),
        scratch_shapes=[pltpu.VMEM((IN_TN, d), BF16)],
        compiler_params=_cparams("arbitrary", "arbitrary"),
        name="in_proj",
    )(xn, w_in_t)


def _pack_mla_weights(w_uq, w_ukv):
    half = MLA_ROPE // 2
    rq = w_uq.shape[0]
    wq = w_uq.reshape(rq, MLA_HEADS, MLA_NOPE + MLA_ROPE)
    nope, t1, t2 = wq[:, :, :MLA_NOPE], wq[:, :, MLA_NOPE:MLA_NOPE + half], wq[:, :, MLA_NOPE + half:]
    zq = jnp.zeros((rq, MLA_HEADS, LANES - MLA_ROPE), w_uq.dtype)
    w_a = jnp.concatenate([nope, t1, t2, zq], axis=2).reshape(rq, MLA_HEADS * MLA_QK_PAD)
    w_b = jnp.concatenate([t2, t1, zq], axis=2).reshape(rq, MLA_HEADS * LANES)
    wab = jnp.concatenate([w_a, w_b], axis=1).astype(BF16)
    rkv = w_ukv.shape[0]
    wkv = w_ukv.reshape(rkv, MLA_HEADS, MLA_NOPE + MLA_DV)
    wkv = jnp.concatenate([wkv[:, :, :MLA_NOPE].reshape(rkv, -1), wkv[:, :, MLA_NOPE:].reshape(rkv, -1)], axis=1)
    return wab, wkv.astype(BF16)


def _rope_maps(positions):
    half = MLA_ROPE // 2
    freqs = ROPE_BASE ** (-jnp.arange(half, dtype=F32) / half)
    ang = positions.astype(F32).reshape(-1)[:, None] * freqs
    cos, sin = jnp.cos(ang), jnp.sin(ang)
    z = jnp.zeros((ang.shape[0], LANES - MLA_ROPE), F32)
    return jnp.concatenate([cos, cos, z], axis=1), jnp.concatenate([-sin, sin, z], axis=1)


def _layer(h, mem, cmap, smap, g_mix, w_in, mla_g_q, mla_w_uq, mla_g_kv, mla_w_ukv, g_mem, w_mem_kv,
           w_branch, w_out, g_ffn, peer_w_q, peer_sub_keys, peer_u, peer_v):
    b, s, d = h.shape
    t = b * s
    n_mem = mem.shape[1]
    h2 = h.reshape(t, d)

    xn = _rmsnorm(h2, g_mix, BF16)
    proj = _in_proj(xn, w_in.T, d)
    proj3 = proj.reshape(b, s, -1)

    y_sb = _sb_attention(proj3).reshape(t, SB_W)

    wab, wkv = _pack_mla_weights(mla_w_uq, mla_w_ukv)
    qcat, kcat, v = _mla_prep(proj, cmap, smap, mla_g_q, mla_g_kv, wab, wkv)
    y_mla = _mla_attention(qcat.reshape(b, s, -1), kcat.reshape(b, s, -1), v.reshape(b, s, -1)).reshape(t, -1)

    mem_n = _rmsnorm(mem.reshape(b * n_mem, d), g_mem, BF16)
    mkv = _matmul(mem_n, w_mem_kv.astype(BF16), BF16, tm=512, tn=1024, name="mem_kv")
    y_mem = _mem_attention(proj3, mkv.reshape(b, n_mem, 2 * MEM_W)).reshape(t, MEM_W)

    merged = _merge(y_sb, y_mla, y_mem, proj, w_branch.astype(BF16), d)
    h1 = _matmul(merged, w_out.astype(BF16), F32, tm=512, tn=1024, residual=h2, name="out_proj")

    xn2 = _rmsnorm(h1, g_ffn, BF16)
    assert PEER_DK // 2 == LANES and peer_w_q.shape[1] == PEER_HEADS * PEER_DK
    q3 = _matmul_split(xn2, peer_w_q.astype(BF16), F32, tm=512, tn=1024, name="peer_q")
    nw, r1, e1 = _peer_route(q3, peer_sub_keys.astype(BF16))
    y = _peer_dense(xn2, peer_u.astype(BF16), peer_v.astype(BF16), nw, r1, e1)
    return h1, y


def kernel(x, mem, positions, g_mix, w_in, mla_g_q, mla_w_uq, mla_g_kv, mla_w_ukv, g_mem, w_mem_kv, w_branch,
           w_out, g_ffn, peer_w_q, peer_sub_keys, peer_u, peer_v, g_final):
    b, s, d = x.shape
    cmap, smap = _rope_maps(positions)
    depth = w_in.shape[0]
    h = x
    for layer in range(depth):
        h1, y = _layer(h, mem, cmap, smap, g_mix[layer], w_in[layer], mla_g_q[layer], mla_w_uq[layer],
                       mla_g_kv[layer], mla_w_ukv[layer], g_mem[layer], w_mem_kv[layer], w_branch[layer],
                       w_out[layer], g_ffn[layer], peer_w_q[layer], peer_sub_keys[layer], peer_u[layer],
                       peer_v[layer])
        if layer + 1 < depth:
            h = (h1 + y).reshape(b, s, d)
    return _add_rmsnorm(h1, y, g_final).reshape(b, s, d)
```

```python
import functools
import math

import jax
import jax.numpy as jnp
from jax import lax
from jax.experimental import pallas as pl
from jax.experimental.pallas import tpu as pltpu

F32 = jnp.float32
BF16 = jnp.bfloat16

EPS = 1e-6
NEG = -1e30
LANES = 128

SB_HEADS = 16
SB_DH = 128
SB_W = SB_HEADS * SB_DH
MLA_HEADS = 16
MLA_Q_RANK = 1024
MLA_KV_RANK = 512
MLA_NOPE = 128
MLA_ROPE = 64
MLA_DV = 128
MLA_QK_PAD = 256
ROPE_BASE = 10000.0
MEM_HEADS = 4
MEM_DH = 512
MEM_W = MEM_HEADS * MEM_DH
N_BRANCH = 3
BRANCH_W = 2048
PEER_HEADS = 8
PEER_NKEYS = 128
PEER_DK = 256
PEER_TOPK = 16

COL_DQ = 3 * SB_W
COL_DKV = COL_DQ + MLA_Q_RANK
DKV_PAD = 1024
COL_MEMQ = COL_DKV + DKV_PAD
COL_GATE = COL_MEMQ + MEM_W

VMEM_LIMIT_BYTES = 56 * 1024 * 1024


def _cparams(*sem):
    return pltpu.CompilerParams(dimension_semantics=sem, vmem_limit_bytes=VMEM_LIMIT_BYTES)


def _nt_dot(a, b):
    return lax.dot_general(a, b, (((1,), (1,)), ((), ())), preferred_element_type=F32)


def _rmsnorm_kernel(x_ref, g_ref, o_ref):
    x = x_ref[...].astype(F32)
    ms = jnp.mean(x * x, axis=-1, keepdims=True)
    o_ref[...] = (x * lax.rsqrt(ms + EPS) * g_ref[...]).astype(o_ref.dtype)


def _rmsnorm(x, g, out_dtype, tm=256):
    m, d = x.shape
    tm = min(tm, m)
    return pl.pallas_call(
        _rmsnorm_kernel,
        grid=(m // tm,),
        in_specs=[pl.BlockSpec((tm, d), lambda i: (i, 0)), pl.BlockSpec((1, d), lambda i: (0, 0))],
        out_specs=pl.BlockSpec((tm, d), lambda i: (i, 0)),
        out_shape=jax.ShapeDtypeStruct((m, d), out_dtype),
        compiler_params=_cparams("parallel"),
        name="rmsnorm",
    )(x, g.reshape(1, d).astype(F32))


def _mm_kernel(a_ref, b_ref, o_ref):
    o_ref[...] = jnp.dot(a_ref[...], b_ref[...], preferred_element_type=F32).astype(o_ref.dtype)


def _mm_res_kernel(a_ref, b_ref, r_ref, o_ref):
    acc = jnp.dot(a_ref[...], b_ref[...], preferred_element_type=F32)
    o_ref[...] = (acc + r_ref[...]).astype(o_ref.dtype)


def _mm_split_kernel(a_ref, b_ref, o_ref):
    acc = jnp.dot(a_ref[...], b_ref[...], preferred_element_type=F32)
    for c in range(o_ref.shape[0]):
        o_ref[c] = acc[:, c * LANES:(c + 1) * LANES].astype(o_ref.dtype)


def _matmul_split(a, b, out_dtype, tm, tn, name):
    m, k = a.shape
    _, n = b.shape
    tm, tn = min(tm, m), min(tn, n)
    return pl.pallas_call(
        _mm_split_kernel,
        grid=(n // tn, m // tm),
        in_specs=[pl.BlockSpec((tm, k), lambda j, i: (i, 0)), pl.BlockSpec((k, tn), lambda j, i: (0, j))],
        out_specs=pl.BlockSpec((tn // LANES, tm, LANES), lambda j, i: (j, i, 0)),
        out_shape=jax.ShapeDtypeStruct((n // LANES, m, LANES), out_dtype),
        compiler_params=_cparams("parallel", "parallel"),
        name=name,
    )(a, b)


def _matmul(a, b, out_dtype, tm, tn, residual=None, name="matmul"):
    m, k = a.shape
    _, n = b.shape
    tm, tn = min(tm, m), min(tn, n)
    in_specs = [pl.BlockSpec((tm, k), lambda j, i: (i, 0)), pl.BlockSpec((k, tn), lambda j, i: (0, j))]
    args = [a, b]
    body = _mm_kernel
    if residual is not None:
        in_specs.append(pl.BlockSpec((tm, tn), lambda j, i: (i, j)))
        args.append(residual)
        body = _mm_res_kernel
    return pl.pallas_call(
        body,
        grid=(n // tn, m // tm),
        in_specs=in_specs,
        out_specs=pl.BlockSpec((tm, tn), lambda j, i: (i, j)),
        out_shape=jax.ShapeDtypeStruct((m, n), out_dtype),
        compiler_params=_cparams("parallel", "parallel"),
        name=name,
    )(*args)


SB_BQ = 256
SB_BK = 128
SB_HG = 8
SB_PAIR = 2 * SB_DH
SB_STOP = -151.5


def _sb_kernel(q_ref, k_ref, v_ref, o_ref, *, scale):
    qi = pl.program_id(2)
    bq = q_ref.shape[1]
    pairs = range(SB_HG // 2)
    lo_half, hi_half = slice(0, SB_DH), slice(SB_DH, SB_PAIR)
    qs = [(q_ref[0, :, p * SB_PAIR:(p + 1) * SB_PAIR].astype(F32) * scale).astype(BF16) for p in pairs]
    r = lax.broadcasted_iota(jnp.int32, (2 * SB_BK, 2 * SB_BK), 0) & (SB_BK - 1)
    c = lax.broadcasted_iota(jnp.int32, (2 * SB_BK, 2 * SB_BK), 1)
    u2 = jnp.where((c >= SB_BK) | (r > c), 1.0, 0.0).astype(BF16)
    qpos = qi * bq + lax.broadcasted_iota(jnp.int32, (bq, SB_PAIR), 0)
    kofs = lax.broadcasted_iota(jnp.int32, (bq, SB_PAIR), 1) & (SB_BK - 1)
    zero_blk = jnp.zeros((SB_BK, SB_DH), BF16)

    def blockdiag(x):
        return jnp.concatenate([jnp.concatenate([x[:, lo_half], zero_blk], axis=1),
                                jnp.concatenate([zero_blk, x[:, hi_half]], axis=1)], axis=0)

    def block(kb, runs, accs, masked):
        start = pl.multiple_of(kb * SB_BK, SB_BK)
        if masked:
            strict = (kb * SB_BK + kofs) < qpos
        zs = [_nt_dot(qs[p], blockdiag(k_ref[0, pl.ds(start, SB_BK), p * SB_PAIR:(p + 1) * SB_PAIR])) for p in pairs]
        lks = [-(jnp.maximum(z, 0.0) + jnp.log2(1.0 + jnp.exp2(-jnp.abs(z)))) for z in zs]
        lkms = [jnp.where(strict, lk, 0.0) for lk in lks] if masked else lks
        his = [lkm.astype(BF16) for lkm in lkms]
        los = [(lkm - hi.astype(F32)).astype(BF16) for lkm, hi in zip(lkms, his)]
        css = [[jnp.dot(jnp.concatenate([his[p][:, half], los[p][:, half]], axis=1), u2, preferred_element_type=F32)
                for half in (lo_half, hi_half)] for p in pairs]
        sufs = [jnp.concatenate([css[p][0][:, :SB_BK], css[p][1][:, :SB_BK]], axis=1) for p in pairs]
        tots = [jnp.concatenate([css[p][0][:, SB_BK:], css[p][1][:, SB_BK:]], axis=1) for p in pairs]
        ws = [jnp.exp2(zs[p] + lks[p] + runs[p] + sufs[p]) for p in pairs]
        if masked:
            ws = [jnp.where(strict, w, 0.0) for w in ws]
        new_accs = [accs[p] + jnp.dot(ws[p].astype(BF16),
                                      blockdiag(v_ref[0, pl.ds(start, SB_BK), p * SB_PAIR:(p + 1) * SB_PAIR]),
                                      preferred_element_type=F32) for p in pairs]
        new_runs = [runs[p] + tots[p] for p in pairs]
        return tuple(new_runs), tuple(new_accs)

    def least_decayed(runs):
        mx = jnp.max(runs[0])
        for p in pairs[1:]:
            mx = jnp.maximum(mx, jnp.max(runs[p]))
        return mx

    runs = tuple(jnp.zeros((bq, SB_PAIR), F32) for _ in pairs)
    accs = tuple(jnp.zeros((bq, SB_PAIR), F32) for _ in pairs)
    nd = bq // SB_BK
    for d in range(nd):
        runs, accs = block(qi * nd + (nd - 1 - d), runs, accs, True)

    def cond(state):
        kb, mx, _, _ = state
        return jnp.logical_and(kb >= 0, mx > SB_STOP)

    def body(state):
        kb, _, rs, ac = state
        rs, ac = block(kb, rs, ac, False)
        return kb - 1, least_decayed(rs), rs, ac

    _, _, _, accs = lax.while_loop(cond, body, (qi * nd - 1, least_decayed(runs), runs, accs))
    for p in pairs:
        o_ref[0, :, p * SB_PAIR:(p + 1) * SB_PAIR] = accs[p].astype(o_ref.dtype)


def _sb_attention(proj3):
    b, s, _ = proj3.shape
    bq = min(SB_BQ, s)
    ng = SB_HEADS // SB_HG
    wg = SB_HG * SB_DH
    return pl.pallas_call(
        functools.partial(_sb_kernel, scale=math.log2(math.e) / math.sqrt(SB_DH)),
        grid=(b, ng, s // bq),
        in_specs=[
            pl.BlockSpec((1, bq, wg), lambda bi, g, qi: (bi, qi, g)),
            pl.BlockSpec((1, s, wg), lambda bi, g, qi: (bi, 0, ng + g), pipeline_mode=pl.Buffered(1)),
            pl.BlockSpec((1, s, wg), lambda bi, g, qi: (bi, 0, 2 * ng + g), pipeline_mode=pl.Buffered(1)),
        ],
        out_specs=pl.BlockSpec((1, bq, wg), lambda bi, g, qi: (bi, qi, g)),
        out_shape=jax.ShapeDtypeStruct((b, s, SB_W), BF16),
        compiler_params=_cparams("parallel", "parallel", "arbitrary"),
        name="sb_attention",
    )(proj3, proj3, proj3)


def _mla_prep_kernel(dq_ref, dkv_ref, c_ref, s_ref, gq_ref, gkv_ref, wab_ref, wkv_ref,
                     qcat_ref, kcat_ref, v_ref, *, scale):
    cmap = c_ref[...]
    smap = s_ref[...]
    dq = dq_ref[...].astype(F32)
    cq = dq * lax.rsqrt(jnp.mean(dq * dq, axis=-1, keepdims=True) + EPS) * gq_ref[...]
    qab = jnp.dot(cq.astype(BF16), wab_ref[...], preferred_element_type=F32)
    swap0 = MLA_HEADS * MLA_QK_PAD
    for h in range(MLA_HEADS):
        lo = h * MLA_QK_PAD
        qcat_ref[:, lo:lo + LANES] = (qab[:, lo:lo + LANES] * scale).astype(BF16)
        rope = qab[:, lo + LANES:lo + 2 * LANES] * cmap + qab[:, swap0 + h * LANES:swap0 + (h + 1) * LANES] * smap
        qcat_ref[:, lo + LANES:lo + 2 * LANES] = (rope * scale).astype(BF16)
    dkv = dkv_ref[...].astype(F32)
    ckv_in = dkv[:, :MLA_KV_RANK]
    ckv = ckv_in * lax.rsqrt(jnp.mean(ckv_in * ckv_in, axis=-1, keepdims=True) + EPS) * gkv_ref[...]
    kv = jnp.dot(ckv.astype(BF16), wkv_ref[...], preferred_element_type=F32)
    kr = (dkv[:, MLA_KV_RANK:MLA_KV_RANK + LANES] * cmap
          + dkv[:, MLA_KV_RANK + LANES:MLA_KV_RANK + 2 * LANES] * smap).astype(BF16)
    for h in range(MLA_HEADS):
        lo = h * MLA_QK_PAD
        kcat_ref[:, lo:lo + LANES] = kv[:, h * LANES:(h + 1) * LANES].astype(BF16)
        kcat_ref[:, lo + LANES:lo + 2 * LANES] = kr
    v_ref[...] = kv[:, MLA_HEADS * MLA_NOPE:].astype(BF16)


def _mla_prep(proj, cmap, smap, gq, gkv, wab, wkv, tt=256):
    t = proj.shape[0]
    tt = min(tt, t)
    wq = MLA_HEADS * MLA_QK_PAD
    wv = MLA_HEADS * MLA_DV
    return pl.pallas_call(
        functools.partial(_mla_prep_kernel, scale=math.log2(math.e) / math.sqrt(MLA_NOPE + MLA_ROPE)),
        grid=(t // tt,),
        in_specs=[
            pl.BlockSpec((tt, MLA_Q_RANK), lambda i: (i, COL_DQ // MLA_Q_RANK)),
            pl.BlockSpec((tt, DKV_PAD), lambda i: (i, COL_DKV // DKV_PAD)),
            pl.BlockSpec((tt, LANES), lambda i: (i, 0)),
            pl.BlockSpec((tt, LANES), lambda i: (i, 0)),
            pl.BlockSpec((1, MLA_Q_RANK), lambda i: (0, 0)),
            pl.BlockSpec((1, MLA_KV_RANK), lambda i: (0, 0)),
            pl.BlockSpec(wab.shape, lambda i: (0, 0)),
            pl.BlockSpec(wkv.shape, lambda i: (0, 0)),
        ],
        out_specs=[
            pl.BlockSpec((tt, wq), lambda i: (i, 0)),
            pl.BlockSpec((tt, wq), lambda i: (i, 0)),
            pl.BlockSpec((tt, wv), lambda i: (i, 0)),
        ],
        out_shape=[
            jax.ShapeDtypeStruct((t, wq), BF16),
            jax.ShapeDtypeStruct((t, wq), BF16),
            jax.ShapeDtypeStruct((t, wv), BF16),
        ],
        compiler_params=_cparams("parallel"),
        name="mla_prep",
    )(proj, proj, cmap, smap, gq.reshape(1, -1).astype(F32), gkv.reshape(1, -1).astype(F32), wab, wkv)


MLA_BQ = 512
MLA_BK = 512
MLA_HG = 4


def _mla_kernel(q_ref, k_ref, v_ref, o_ref):
    qi = pl.program_id(2)
    bq = q_ref.shape[1]
    bk = min(MLA_BK, bq)
    heads = range(MLA_HG)
    qs = [q_ref[0, :, h * MLA_QK_PAD:(h + 1) * MLA_QK_PAD] for h in heads]
    kpos = lax.broadcasted_iota(jnp.int32, (bk, bq), 0)
    qpos = qi * bq + lax.broadcasted_iota(jnp.int32, (bk, bq), 1)

    def block(kb, ms, ls, accs, masked):
        start = pl.multiple_of(kb * bk, bk)
        ss = [_nt_dot(k_ref[0, pl.ds(start, bk), h * MLA_QK_PAD:(h + 1) * MLA_QK_PAD], qs[h]) for h in heads]
        if masked:
            allowed = (kb * bk + kpos) <= qpos
            ss = [jnp.where(allowed, s, NEG) for s in ss]
        new_ms = [jnp.maximum(ms[h], jnp.max(ss[h], axis=0, keepdims=True)) for h in heads]
        ps = [jnp.exp2(ss[h] - new_ms[h]) for h in heads]
        alphas = [jnp.exp2(ms[h] - new_ms[h]) for h in heads]
        new_ls = [alphas[h] * ls[h] + jnp.sum(ps[h], axis=0, keepdims=True) for h in heads]
        pvs = [lax.dot_general(v_ref[0, pl.ds(start, bk), h * MLA_DV:(h + 1) * MLA_DV], ps[h].astype(BF16),
                               (((0,), (0,)), ((), ())), preferred_element_type=F32) for h in heads]
        new_accs = [alphas[h] * accs[h] + pvs[h] for h in heads]
        return tuple(new_ms), tuple(new_ls), tuple(new_accs)

    ms = tuple(jnp.full((1, bq), NEG, F32) for _ in heads)
    ls = tuple(jnp.zeros((1, bq), F32) for _ in heads)
    accs = tuple(jnp.zeros((MLA_DV, bq), F32) for _ in heads)
    nd = bq // bk
    ms, ls, accs = lax.fori_loop(0, qi * nd, lambda kb, cr: block(kb, cr[0], cr[1], cr[2], False), (ms, ls, accs))
    for d in range(nd):
        ms, ls, accs = block(qi * nd + d, ms, ls, accs, True)
    for h in heads:
        o_ref[0, :, h * MLA_DV:(h + 1) * MLA_DV] = (accs[h] / ls[h]).T.astype(o_ref.dtype)


def _mla_attention(qcat3, kcat3, v3):
    b, s, _ = qcat3.shape
    bq = min(MLA_BQ, s)
    wq = MLA_HG * MLA_QK_PAD
    wv = MLA_HG * MLA_DV
    return pl.pallas_call(
        _mla_kernel,
        grid=(b, MLA_HEADS // MLA_HG, s // bq),
        in_specs=[
            pl.BlockSpec((1, bq, wq), lambda bi, g, qi: (bi, qi, g)),
            pl.BlockSpec((1, s, wq), lambda bi, g, qi: (bi, 0, g), pipeline_mode=pl.Buffered(1)),
            pl.BlockSpec((1, s, wv), lambda bi, g, qi: (bi, 0, g), pipeline_mode=pl.Buffered(1)),
        ],
        out_specs=pl.BlockSpec((1, bq, wv), lambda bi, g, qi: (bi, qi, g)),
        out_shape=jax.ShapeDtypeStruct((b, s, MLA_HEADS * MLA_DV), BF16),
        compiler_params=_cparams("parallel", "parallel", "arbitrary"),
        name="mla_attention",
    )(qcat3, kcat3, v3)


def _mem_kernel(q_ref, kv_ref, o_ref, *, scale):
    for h in range(MEM_HEADS):
        cols = slice(h * MEM_DH, (h + 1) * MEM_DH)
        s = _nt_dot(q_ref[0, :, cols], kv_ref[0, :, cols]) * scale
        m = jnp.max(s, axis=-1, keepdims=True)
        p = jnp.exp(s - m)
        p = p / jnp.sum(p, axis=-1, keepdims=True)
        vals = kv_ref[0, :, MEM_W + h * MEM_DH:MEM_W + (h + 1) * MEM_DH]
        o_ref[0, :, cols] = jnp.dot(p.astype(BF16), vals, preferred_element_type=F32).astype(o_ref.dtype)


def _mem_attention(proj3, mkv3, tt=512):
    b, s, _ = proj3.shape
    n_mem = mkv3.shape[1]
    tt = min(tt, s)
    return pl.pallas_call(
        functools.partial(_mem_kernel, scale=1.0 / math.sqrt(MEM_DH)),
        grid=(b, s // tt),
        in_specs=[
            pl.BlockSpec((1, tt, MEM_W), lambda bi, ti: (bi, ti, COL_MEMQ // MEM_W)),
            pl.BlockSpec((1, n_mem, 2 * MEM_W), lambda bi, ti: (bi, 0, 0)),
        ],
        out_specs=pl.BlockSpec((1, tt, MEM_W), lambda bi, ti: (bi, ti, 0)),
        out_shape=jax.ShapeDtypeStruct((b, s, MEM_W), BF16),
        compiler_params=_cparams("parallel", "parallel"),
        name="mem_attention",
    )(proj3, mkv3)


def _merge_kernel(ya_ref, yb_ref, yc_ref, ga_ref, gb_ref, gc_ref, w_ref, o_ref):
    acc = jax.nn.sigmoid(ga_ref[...].astype(F32)) * jnp.dot(ya_ref[...], w_ref[0], preferred_element_type=F32)
    acc += jax.nn.sigmoid(gb_ref[...].astype(F32)) * jnp.dot(yb_ref[...], w_ref[1], preferred_element_type=F32)
    acc += jax.nn.sigmoid(gc_ref[...].astype(F32)) * jnp.dot(yc_ref[...], w_ref[2], preferred_element_type=F32)
    o_ref[...] = acc.astype(o_ref.dtype)


def _merge(y_sb, y_mla, y_mem, proj, wb, d, tm=1024, tn=512):
    t = y_sb.shape[0]
    tm, tn = min(tm, t), min(tn, d)
    g0 = COL_GATE // tn
    gstep = d // tn

    def gate_spec(br):
        return pl.BlockSpec((tm, tn), lambda j, i: (i, g0 + br * gstep + j))

    yspec = pl.BlockSpec((tm, BRANCH_W), lambda j, i: (i, 0))
    return pl.pallas_call(
        _merge_kernel,
        grid=(d // tn, t // tm),
        in_specs=[yspec, yspec, yspec, gate_spec(0), gate_spec(1), gate_spec(2),
                  pl.BlockSpec((N_BRANCH, BRANCH_W, tn), lambda j, i: (0, 0, j))],
        out_specs=pl.BlockSpec((tm, tn), lambda j, i: (i, j)),
        out_shape=jax.ShapeDtypeStruct((t, d), BF16),
        compiler_params=_cparams("parallel", "parallel"),
        name="branch_merge",
    )(y_sb, y_mla, y_mem, proj, proj, proj, wb)


PEER_EB = 512
PEER_GROUP = PEER_EB // PEER_NKEYS
PEER_ROUTE_UNROLL = 4
PEER_RANK_NONE = 255.0
PEER_CAND_WIDTH = [PEER_TOPK // (a + 1) for a in range(PEER_TOPK)]


def _take_top(vals, n, with_rank=False):
    out = []
    rem = vals
    rank = jnp.full(vals.shape, PEER_RANK_NONE, F32) if with_rank else None
    for k in range(n):
        mx = jnp.max(rem, axis=0, keepdims=True)
        out.append(mx)
        hit = rem == mx
        if with_rank:
            rank = jnp.where(hit, float(k), rank)
        rem = jnp.where(hit, -jnp.inf, rem)
    return (out, rank) if with_rank else out


def _peer_route_kernel(q_ref, keys_ref, nw_ref, r1_ref, e1_ref):
    def head(h):
        s0 = _nt_dot(keys_ref[h, 0], q_ref[2 * h].astype(BF16))
        s1 = _nt_dot(keys_ref[h, 1], q_ref[2 * h + 1].astype(BF16))
        top0 = _take_top(s0, PEER_TOPK)
        top1, rank1 = _take_top(s1, PEER_TOPK, with_rank=True)
        cand = [[top0[a] + top1[b] for b in range(PEER_CAND_WIDTH[a])] for a in range(PEER_TOPK)]
        flat = [c for row in cand for c in row]
        pad = -len(flat) % 8
        stacked = jnp.concatenate(flat + [jnp.full_like(flat[0], -jnp.inf)] * pad, axis=0)
        ctop = _take_top(stacked, PEER_TOPK)
        cmax, tau = ctop[0], ctop[PEER_TOPK - 1]
        zsum = jnp.zeros_like(cmax)
        for cv in ctop:
            zsum = zsum + jnp.exp(cv - cmax)
        n0 = jnp.zeros_like(s0)
        for a in range(PEER_TOPK):
            count = jnp.zeros_like(tau)
            for c in cand[a]:
                count = count + jnp.where(c >= tau, 1.0, 0.0)
            n0 = jnp.where(s0 == top0[a], count, n0)
        w0 = jnp.exp(s0 - top0[0]) / zsum
        for g in range(PEER_NKEYS // PEER_GROUP):
            keys = slice(g * PEER_GROUP, (g + 1) * PEER_GROUP)
            nw_ref[g, h, :PEER_GROUP, :] = n0[keys]
            nw_ref[g, h, PEER_GROUP:, :] = w0[keys]
        r1_ref[h] = pltpu.bitcast(rank1.astype(BF16), jnp.uint32)
        e1_ref[h] = pltpu.bitcast(jnp.exp(s1 - top1[0]).astype(BF16), jnp.uint32)

    def head_group(hg, carry):
        for k in range(PEER_ROUTE_UNROLL):
            head(PEER_ROUTE_UNROLL * hg + k)
        return carry

    lax.fori_loop(0, PEER_HEADS // PEER_ROUTE_UNROLL, head_group, 0)


def _peer_route(q3, keys, tt=256):
    _, t, dh = q3.shape
    tt = min(tt, t)
    ngrp = PEER_NKEYS // PEER_GROUP
    nwspec = pl.BlockSpec((ngrp, PEER_HEADS, 2 * PEER_GROUP, tt), lambda i: (0, 0, 0, i))
    pspec = pl.BlockSpec((PEER_HEADS, PEER_NKEYS // 2, tt), lambda i: (0, 0, i))
    packed = jax.ShapeDtypeStruct((PEER_HEADS, PEER_NKEYS // 2, t), jnp.uint32)
    return pl.pallas_call(
        _peer_route_kernel,
        grid=(t // tt,),
        in_specs=[pl.BlockSpec((2 * PEER_HEADS, tt, dh), lambda i: (0, i, 0)),
                  pl.BlockSpec(keys.shape, lambda i: (0, 0, 0, 0))],
        out_specs=[nwspec, pspec, pspec],
        out_shape=[jax.ShapeDtypeStruct((ngrp, PEER_HEADS, 2 * PEER_GROUP, t), F32), packed, packed],
        compiler_params=_cparams("parallel"),
        name="peer_route",
    )(q3, keys)


PEER_ROWS = 16
PEER_OUT_ROWS = 32


def _peer_dense_kernel(xn_ref, u_ref, v_ref, nw_ref, r1_ref, e1_ref, h_hbm, g_ref, o_ref, at_ref, h_ref, h_sem,
                       *, normalize):
    e = pl.program_id(1)
    tt = o_ref.shape[0]
    rows0 = pl.multiple_of(pl.program_id(0) * tt, tt)
    h_copy = pltpu.make_async_copy(h_hbm.at[pl.ds(rows0, tt), :], h_ref, h_sem)

    @pl.when(e == 0)
    def _():
        h_copy.start()
        o_ref[...] = jnp.zeros_like(o_ref)

    pre = _nt_dot(u_ref[...], xn_ref[...])
    at_ref[...] = (0.5 * pre * (1.0 + lax.erf(pre * (1.0 / math.sqrt(2.0))))).astype(BF16)
    tt = at_ref.shape[1]
    zero = jnp.zeros((PEER_ROWS, LANES), BF16)
    for ii in range(PEER_GROUP):
        for l0 in range(0, tt, LANES):
            lanes = slice(l0, l0 + LANES)
            counts = [jnp.broadcast_to(nw_ref[0, h, ii:ii + 1, lanes], (PEER_ROWS, LANES)).astype(BF16)
                      for h in range(PEER_HEADS)]
            weights = [jnp.broadcast_to(nw_ref[0, h, PEER_GROUP + ii:PEER_GROUP + ii + 1, lanes],
                                        (PEER_ROWS, LANES)).astype(BF16) for h in range(PEER_HEADS)]
            for r0 in range(0, PEER_NKEYS, PEER_ROWS):
                words = slice(r0 // 2, (r0 + PEER_ROWS) // 2)
                gate = zero
                for h in range(PEER_HEADS):
                    rank = pltpu.bitcast(r1_ref[h, words, lanes], BF16)
                    e1 = pltpu.bitcast(e1_ref[h, words, lanes], BF16)
                    gate = gate + jnp.where(rank < counts[h], e1 * weights[h], zero)
                e0 = ii * PEER_NKEYS + r0
                at_ref[e0:e0 + PEER_ROWS, lanes] = at_ref[e0:e0 + PEER_ROWS, lanes] * gate
    o_ref[...] += lax.dot_general(at_ref[...], v_ref[...], (((0,), (0,)), ((), ())), preferred_element_type=F32)

    @pl.when(e == pl.num_programs(1) - 1)
    def _():
        h_copy.wait()

        def strip(r, carry):
            rows = pl.ds(pl.multiple_of(r * PEER_OUT_ROWS, PEER_OUT_ROWS), PEER_OUT_ROWS)
            x = h_ref[rows, :] + o_ref[rows, :]
            if normalize:
                x = x * lax.rsqrt(jnp.mean(x * x, axis=-1, keepdims=True) + EPS) * g_ref[...]
            o_ref[rows, :] = x
            return carry

        lax.fori_loop(0, o_ref.shape[0] // PEER_OUT_ROWS, strip, 0)


def _peer_dense(xn, u, v, nw, r1, e1, h1, gain, tt=512):
    t, d = xn.shape
    n = u.shape[0]
    tt = min(tt, t)
    hspec = pl.BlockSpec((PEER_HEADS, PEER_NKEYS // 2, tt), lambda ti, e: (0, 0, ti))
    normalize = gain is not None
    gain = jnp.ones((d,), F32) if gain is None else gain
    return pl.pallas_call(
        functools.partial(_peer_dense_kernel, normalize=normalize),
        grid=(t // tt, n // PEER_EB),
        in_specs=[
            pl.BlockSpec((tt, d), lambda ti, e: (ti, 0)),
            pl.BlockSpec((PEER_EB, d), lambda ti, e: (e, 0)),
            pl.BlockSpec((PEER_EB, d), lambda ti, e: (e, 0)),
            pl.BlockSpec((1, PEER_HEADS, 2 * PEER_GROUP, tt), lambda ti, e: (e, 0, 0, ti)),
            hspec,
            hspec,
            pl.BlockSpec(memory_space=pl.ANY),
            pl.BlockSpec((1, d), lambda ti, e: (0, 0)),
        ],
        out_specs=pl.BlockSpec((tt, d), lambda ti, e: (ti, 0)),
        out_shape=jax.ShapeDtypeStruct((t, d), F32),
        scratch_shapes=[pltpu.VMEM((PEER_EB, tt), BF16), pltpu.VMEM((tt, d), F32), pltpu.SemaphoreType.DMA(())],
        compiler_params=_cparams("arbitrary", "arbitrary"),
        name="peer_dense",
    )(xn, u, v, nw, r1, e1, h1, gain.reshape(1, d).astype(F32))


IN_TN = 512
IN_ROW_ALIGN = 64


def _in_proj_kernel(a_ref, w_ref, o_ref, wb_ref, *, kr_block):
    j = pl.program_id(0)
    i = pl.program_id(1)
    half = MLA_ROPE // 2
    kr0 = (COL_DKV + MLA_KV_RANK) % IN_TN

    @pl.when(jnp.logical_and(i == 0, j != kr_block))
    def _():
        wb_ref[...] = w_ref[...].astype(BF16)

    @pl.when(jnp.logical_and(i == 0, j == kr_block))
    def _():
        wb_ref[...] = jnp.zeros_like(wb_ref)
        wb_ref[:kr0 + MLA_ROPE, :] = w_ref[:kr0 + MLA_ROPE, :].astype(BF16)
        wb_ref[kr0 + LANES:kr0 + LANES + half, :] = w_ref[kr0 + half:kr0 + MLA_ROPE, :].astype(BF16)
        wb_ref[kr0 + LANES + half:kr0 + LANES + MLA_ROPE, :] = w_ref[kr0:kr0 + half, :].astype(BF16)

    o_ref[...] = _nt_dot(a_ref[...], wb_ref[...]).astype(o_ref.dtype)


def _in_proj(xn, w_in_t, d, tm=1024):
    t = xn.shape[0]
    c3 = 3 * SB_W + MLA_Q_RANK + MLA_KV_RANK + MLA_ROPE
    n_cols = COL_GATE + N_BRANCH * d
    kr_block = (COL_DKV + MLA_KV_RANK) // IN_TN
    assert (kr_block + 1) * IN_TN == COL_MEMQ and (COL_DKV + MLA_KV_RANK) % IN_TN + 2 * LANES <= IN_TN
    assert (n_cols - COL_MEMQ) % IN_TN == 0 and w_in_t.shape[0] - c3 == n_cols - COL_MEMQ and c3 % IN_ROW_ALIGN == 0
    tm = min(tm, t)

    def row0(j):
        return pl.multiple_of(jnp.where(j <= kr_block, j * IN_TN, c3 + (j - kr_block - 1) * IN_TN), IN_ROW_ALIGN)

    return pl.pallas_call(
        functools.partial(_in_proj_kernel, kr_block=kr_block),
        grid=(n_cols // IN_TN, t // tm),
        in_specs=[pl.BlockSpec((tm, d), lambda j, i: (i, 0)),
                  pl.BlockSpec((pl.Element(IN_TN), pl.Element(d)), lambda j, i: (row0(j), 0))],
        out_specs=pl.BlockSpec((tm, IN_TN), lambda j, i: (i, j)),
        out_shape=jax.ShapeDtypeStruct((t, n_cols), BF16),
        scratch_shapes=[pltpu.VMEM((IN_TN, d), BF16)],
        compiler_params=_cparams("arbitrary", "arbitrary"),
        name="in_proj",
    )(xn, w_in_t)


def _pack_mla_weights(w_uq, w_ukv):
    half = MLA_ROPE // 2
    rq = w_uq.shape[0]
    wq = w_uq.reshape(rq, MLA_HEADS, MLA_NOPE + MLA_ROPE)
    nope, t1, t2 = wq[:, :, :MLA_NOPE], wq[:, :, MLA_NOPE:MLA_NOPE + half], wq[:, :, MLA_NOPE + half:]
    zq = jnp.zeros((rq, MLA_HEADS, LANES - MLA_ROPE), w_uq.dtype)
    w_a = jnp.concatenate([nope, t1, t2, zq], axis=2).reshape(rq, MLA_HEADS * MLA_QK_PAD)
    w_b = jnp.concatenate([t2, t1, zq], axis=2).reshape(rq, MLA_HEADS * LANES)
    wab = jnp.concatenate([w_a, w_b], axis=1).astype(BF16)
    rkv = w_ukv.shape[0]
    wkv = w_ukv.reshape(rkv, MLA_HEADS, MLA_NOPE + MLA_DV)
    wkv = jnp.concatenate([wkv[:, :, :MLA_NOPE].reshape(rkv, -1), wkv[:, :, MLA_NOPE:].reshape(rkv, -1)], axis=1)
    return wab, wkv.astype(BF16)


def _rope_maps(positions):
    half = MLA_ROPE // 2
    freqs = ROPE_BASE ** (-jnp.arange(half, dtype=F32) / half)
    ang = positions.astype(F32).reshape(-1)[:, None] * freqs
    cos, sin = jnp.cos(ang), jnp.sin(ang)
    z = jnp.zeros((ang.shape[0], LANES - MLA_ROPE), F32)
    return jnp.concatenate([cos, cos, z], axis=1), jnp.concatenate([-sin, sin, z], axis=1)


def _layer(h, mem, cmap, smap, g_mix, w_in, mla_g_q, mla_w_uq, mla_g_kv, mla_w_ukv, g_mem, w_mem_kv,
           w_branch, w_out, g_ffn, peer_w_q, peer_sub_keys, peer_u, peer_v, g_out):
    b, s, d = h.shape
    t = b * s
    n_mem = mem.shape[1]
    h2 = h.reshape(t, d)

    xn = _rmsnorm(h2, g_mix, BF16)
    proj = _in_proj(xn, w_in.T, d)
    proj3 = proj.reshape(b, s, -1)

    y_sb = _sb_attention(proj3).reshape(t, SB_W)

    wab, wkv = _pack_mla_weights(mla_w_uq, mla_w_ukv)
    qcat, kcat, v = _mla_prep(proj, cmap, smap, mla_g_q, mla_g_kv, wab, wkv)
    y_mla = _mla_attention(qcat.reshape(b, s, -1), kcat.reshape(b, s, -1), v.reshape(b, s, -1)).reshape(t, -1)

    mem_n = _rmsnorm(mem.reshape(b * n_mem, d), g_mem, BF16)
    mkv = _matmul(mem_n, w_mem_kv.astype(BF16), BF16, tm=512, tn=1024, name="mem_kv")
    y_mem = _mem_attention(proj3, mkv.reshape(b, n_mem, 2 * MEM_W)).reshape(t, MEM_W)

    merged = _merge(y_sb, y_mla, y_mem, proj, w_branch.astype(BF16), d)
    h1 = _matmul(merged, w_out.astype(BF16), F32, tm=512, tn=1024, residual=h2, name="out_proj")

    xn2 = _rmsnorm(h1, g_ffn, BF16)
    assert PEER_DK // 2 == LANES and peer_w_q.shape[1] == PEER_HEADS * PEER_DK
    q3 = _matmul_split(xn2, peer_w_q.astype(BF16), F32, tm=512, tn=1024, name="peer_q")
    nw, r1, e1 = _peer_route(q3, peer_sub_keys.astype(BF16))
    return _peer_dense(xn2, peer_u.astype(BF16), peer_v.astype(BF16), nw, r1, e1, h1, g_out)


def kernel(x, mem, positions, g_mix, w_in, mla_g_q, mla_w_uq, mla_g_kv, mla_w_ukv, g_mem, w_mem_kv, w_branch,
           w_out, g_ffn, peer_w_q, peer_sub_keys, peer_u, peer_v, g_final):
    b, s, d = x.shape
    cmap, smap = _rope_maps(positions)
    depth = w_in.shape[0]
    h = x
    for layer in range(depth):
        g_out = g_final if layer + 1 == depth else None
        h = _layer(h, mem, cmap, smap, g_mix[layer], w_in[layer], mla_g_q[layer], mla_w_uq[layer],
                   mla_g_kv[layer], mla_w_ukv[layer], g_mem[layer], w_mem_kv[layer], w_branch[layer],
                   w_out[layer], g_ffn[layer], peer_w_q[layer], peer_sub_keys[layer], peer_u[layer],
                   peer_v[layer], g_out).reshape(b, s, d)
    return h
```

```python
import functools
import math

import jax
import jax.numpy as jnp
from jax import lax
from jax.experimental import pallas as pl
from jax.experimental.pallas import tpu as pltpu

F32 = jnp.float32
BF16 = jnp.bfloat16

EPS = 1e-6
NEG = -1e30
LANES = 128

SB_HEADS = 16
SB_DH = 128
SB_W = SB_HEADS * SB_DH
MLA_HEADS = 16
MLA_Q_RANK = 1024
MLA_KV_RANK = 512
MLA_NOPE = 128
MLA_ROPE = 64
MLA_DV = 128
MLA_QK_PAD = 256
ROPE_BASE = 10000.0
MEM_HEADS = 4
MEM_DH = 512
MEM_W = MEM_HEADS * MEM_DH
N_BRANCH = 3
BRANCH_W = 2048
PEER_HEADS = 8
PEER_NKEYS = 128
PEER_DK = 256
PEER_TOPK = 16

COL_DQ = 3 * SB_W
COL_DKV = COL_DQ + MLA_Q_RANK
DKV_PAD = 1024
COL_MEMQ = COL_DKV + DKV_PAD
COL_GATE = COL_MEMQ + MEM_W

VMEM_LIMIT_BYTES = 56 * 1024 * 1024


def _cparams(*sem):
    return pltpu.CompilerParams(dimension_semantics=sem, vmem_limit_bytes=VMEM_LIMIT_BYTES)


def _nt_dot(a, b):
    return lax.dot_general(a, b, (((1,), (1,)), ((), ())), preferred_element_type=F32)


def _rmsnorm_kernel(x_ref, g_ref, o_ref):
    x = x_ref[...].astype(F32)
    ms = jnp.mean(x * x, axis=-1, keepdims=True)
    o_ref[...] = (x * lax.rsqrt(ms + EPS) * g_ref[...]).astype(o_ref.dtype)


def _rmsnorm(x, g, out_dtype, tm=256):
    m, d = x.shape
    tm = min(tm, m)
    return pl.pallas_call(
        _rmsnorm_kernel,
        grid=(m // tm,),
        in_specs=[pl.BlockSpec((tm, d), lambda i: (i, 0)), pl.BlockSpec((1, d), lambda i: (0, 0))],
        out_specs=pl.BlockSpec((tm, d), lambda i: (i, 0)),
        out_shape=jax.ShapeDtypeStruct((m, d), out_dtype),
        compiler_params=_cparams("parallel"),
        name="rmsnorm",
    )(x, g.reshape(1, d).astype(F32))


def _mm_kernel(a_ref, b_ref, o_ref):
    o_ref[...] = jnp.dot(a_ref[...], b_ref[...], preferred_element_type=F32).astype(o_ref.dtype)


def _mm_res_kernel(a_ref, b_ref, r_ref, o_ref):
    acc = jnp.dot(a_ref[...], b_ref[...], preferred_element_type=F32)
    o_ref[...] = (acc + r_ref[...]).astype(o_ref.dtype)


def _mm_split_kernel(a_ref, b_ref, o_ref):
    acc = jnp.dot(a_ref[...], b_ref[...], preferred_element_type=F32)
    for c in range(o_ref.shape[0]):
        o_ref[c] = acc[:, c * LANES:(c + 1) * LANES].astype(o_ref.dtype)


def _matmul_split(a, b, out_dtype, tm, tn, name):
    m, k = a.shape
    _, n = b.shape
    tm, tn = min(tm, m), min(tn, n)
    return pl.pallas_call(
        _mm_split_kernel,
        grid=(n // tn, m // tm),
        in_specs=[pl.BlockSpec((tm, k), lambda j, i: (i, 0)), pl.BlockSpec((k, tn), lambda j, i: (0, j))],
        out_specs=pl.BlockSpec((tn // LANES, tm, LANES), lambda j, i: (j, i, 0)),
        out_shape=jax.ShapeDtypeStruct((n // LANES, m, LANES), out_dtype),
        compiler_params=_cparams("parallel", "parallel"),
        name=name,
    )(a, b)


def _matmul(a, b, out_dtype, tm, tn, residual=None, name="matmul"):
    m, k = a.shape
    _, n = b.shape
    tm, tn = min(tm, m), min(tn, n)
    in_specs = [pl.BlockSpec((tm, k), lambda j, i: (i, 0)), pl.BlockSpec((k, tn), lambda j, i: (0, j))]
    args = [a, b]
    body = _mm_kernel
    if residual is not None:
        in_specs.append(pl.BlockSpec((tm, tn), lambda j, i: (i, j)))
        args.append(residual)
        body = _mm_res_kernel
    return pl.pallas_call(
        body,
        grid=(n // tn, m // tm),
        in_specs=in_specs,
        out_specs=pl.BlockSpec((tm, tn), lambda j, i: (i, j)),
        out_shape=jax.ShapeDtypeStruct((m, n), out_dtype),
        compiler_params=_cparams("parallel", "parallel"),
        name=name,
    )(*args)


SB_BQ = 256
SB_BK = 128
SB_HG = 8
SB_PAIR = 2 * SB_DH
SB_STOP = -151.5


def _sb_kernel(q_ref, k_ref, v_ref, o_ref, *, scale):
    qi = pl.program_id(2)
    bq = q_ref.shape[1]
    pairs = range(SB_HG // 2)
    lo_half, hi_half = slice(0, SB_DH), slice(SB_DH, SB_PAIR)
    qs = [(q_ref[0, :, p * SB_PAIR:(p + 1) * SB_PAIR].astype(F32) * scale).astype(BF16) for p in pairs]
    r = lax.broadcasted_iota(jnp.int32, (2 * SB_BK, 2 * SB_BK), 0) & (SB_BK - 1)
    c = lax.broadcasted_iota(jnp.int32, (2 * SB_BK, 2 * SB_BK), 1)
    u2 = jnp.where((c >= SB_BK) | (r > c), 1.0, 0.0).astype(BF16)
    qpos = qi * bq + lax.broadcasted_iota(jnp.int32, (bq, SB_PAIR), 0)
    kofs = lax.broadcasted_iota(jnp.int32, (bq, SB_PAIR), 1) & (SB_BK - 1)
    zero_blk = jnp.zeros((SB_BK, SB_DH), BF16)

    def blockdiag(x):
        return jnp.concatenate([jnp.concatenate([x[:, lo_half], zero_blk], axis=1),
                                jnp.concatenate([zero_blk, x[:, hi_half]], axis=1)], axis=0)

    def block(kb, runs, accs, masked):
        start = pl.multiple_of(kb * SB_BK, SB_BK)
        if masked:
            strict = (kb * SB_BK + kofs) < qpos
        zs = [_nt_dot(qs[p], blockdiag(k_ref[0, pl.ds(start, SB_BK), p * SB_PAIR:(p + 1) * SB_PAIR])) for p in pairs]
        lks = [-(jnp.maximum(z, 0.0) + jnp.log2(1.0 + jnp.exp2(-jnp.abs(z)))) for z in zs]
        lkms = [jnp.where(strict, lk, 0.0) for lk in lks] if masked else lks
        his = [lkm.astype(BF16) for lkm in lkms]
        los = [(lkm - hi.astype(F32)).astype(BF16) for lkm, hi in zip(lkms, his)]
        css = [[jnp.dot(jnp.concatenate([his[p][:, half], los[p][:, half]], axis=1), u2, preferred_element_type=F32)
                for half in (lo_half, hi_half)] for p in pairs]
        sufs = [jnp.concatenate([css[p][0][:, :SB_BK], css[p][1][:, :SB_BK]], axis=1) for p in pairs]
        tots = [jnp.concatenate([css[p][0][:, SB_BK:], css[p][1][:, SB_BK:]], axis=1) for p in pairs]
        ws = [jnp.exp2(zs[p] + lks[p] + runs[p] + sufs[p]) for p in pairs]
        if masked:
            ws = [jnp.where(strict, w, 0.0) for w in ws]
        new_accs = [accs[p] + jnp.dot(ws[p].astype(BF16),
                                      blockdiag(v_ref[0, pl.ds(start, SB_BK), p * SB_PAIR:(p + 1) * SB_PAIR]),
                                      preferred_element_type=F32) for p in pairs]
        new_runs = [runs[p] + tots[p] for p in pairs]
        return tuple(new_runs), tuple(new_accs)

    def least_decayed(runs):
        mx = jnp.max(runs[0])
        for p in pairs[1:]:
            mx = jnp.maximum(mx, jnp.max(runs[p]))
        return mx

    runs = tuple(jnp.zeros((bq, SB_PAIR), F32) for _ in pairs)
    accs = tuple(jnp.zeros((bq, SB_PAIR), F32) for _ in pairs)
    nd = bq // SB_BK
    for d in range(nd):
        runs, accs = block(qi * nd + (nd - 1 - d), runs, accs, True)

    def cond(state):
        kb, mx, _, _ = state
        return jnp.logical_and(kb >= 0, mx > SB_STOP)

    def body(state):
        kb, _, rs, ac = state
        rs, ac = block(kb, rs, ac, False)
        return kb - 1, least_decayed(rs), rs, ac

    _, _, _, accs = lax.while_loop(cond, body, (qi * nd - 1, least_decayed(runs), runs, accs))
    for p in pairs:
        o_ref[0, :, p * SB_PAIR:(p + 1) * SB_PAIR] = accs[p].astype(o_ref.dtype)


def _sb_attention(proj3):
    b, s, _ = proj3.shape
    bq = min(SB_BQ, s)
    ng = SB_HEADS // SB_HG
    wg = SB_HG * SB_DH
    return pl.pallas_call(
        functools.partial(_sb_kernel, scale=math.log2(math.e) / math.sqrt(SB_DH)),
        grid=(b, ng, s // bq),
        in_specs=[
            pl.BlockSpec((1, bq, wg), lambda bi, g, qi: (bi, qi, g)),
            pl.BlockSpec((1, s, wg), lambda bi, g, qi: (bi, 0, ng + g), pipeline_mode=pl.Buffered(1)),
            pl.BlockSpec((1, s, wg), lambda bi, g, qi: (bi, 0, 2 * ng + g), pipeline_mode=pl.Buffered(1)),
        ],
        out_specs=pl.BlockSpec((1, bq, wg), lambda bi, g, qi: (bi, qi, g)),
        out_shape=jax.ShapeDtypeStruct((b, s, SB_W), BF16),
        compiler_params=_cparams("parallel", "parallel", "arbitrary"),
        name="sb_attention",
    )(proj3, proj3, proj3)


def _mla_prep_kernel(dq_ref, dkv_ref, c_ref, s_ref, gq_ref, gkv_ref, wab_ref, wkv_ref,
                     qcat_ref, kcat_ref, v_ref, *, scale):
    cmap = c_ref[...]
    smap = s_ref[...]
    dq = dq_ref[...].astype(F32)
    cq = dq * lax.rsqrt(jnp.mean(dq * dq, axis=-1, keepdims=True) + EPS) * gq_ref[...]
    qab = jnp.dot(cq.astype(BF16), wab_ref[...], preferred_element_type=F32)
    swap0 = MLA_HEADS * MLA_QK_PAD
    for h in range(MLA_HEADS):
        lo = h * MLA_QK_PAD
        qcat_ref[:, lo:lo + LANES] = (qab[:, lo:lo + LANES] * scale).astype(BF16)
        rope = qab[:, lo + LANES:lo + 2 * LANES] * cmap + qab[:, swap0 + h * LANES:swap0 + (h + 1) * LANES] * smap
        qcat_ref[:, lo + LANES:lo + 2 * LANES] = (rope * scale).astype(BF16)
    dkv = dkv_ref[...].astype(F32)
    ckv_in = dkv[:, :MLA_KV_RANK]
    ckv = ckv_in * lax.rsqrt(jnp.mean(ckv_in * ckv_in, axis=-1, keepdims=True) + EPS) * gkv_ref[...]
    kv = jnp.dot(ckv.astype(BF16), wkv_ref[...], preferred_element_type=F32)
    kr = (dkv[:, MLA_KV_RANK:MLA_KV_RANK + LANES] * cmap
          + dkv[:, MLA_KV_RANK + LANES:MLA_KV_RANK + 2 * LANES] * smap).astype(BF16)
    for h in range(MLA_HEADS):
        lo = h * MLA_QK_PAD
        kcat_ref[:, lo:lo + LANES] = kv[:, h * LANES:(h + 1) * LANES].astype(BF16)
        kcat_ref[:, lo + LANES:lo + 2 * LANES] = kr
    v_ref[...] = kv[:, MLA_HEADS * MLA_NOPE:].astype(BF16)


def _mla_prep(proj, cmap, smap, gq, gkv, wab, wkv, tt=256):
    t = proj.shape[0]
    tt = min(tt, t)
    wq = MLA_HEADS * MLA_QK_PAD
    wv = MLA_HEADS * MLA_DV
    return pl.pallas_call(
        functools.partial(_mla_prep_kernel, scale=math.log2(math.e) / math.sqrt(MLA_NOPE + MLA_ROPE)),
        grid=(t // tt,),
        in_specs=[
            pl.BlockSpec((tt, MLA_Q_RANK), lambda i: (i, COL_DQ // MLA_Q_RANK)),
            pl.BlockSpec((tt, DKV_PAD), lambda i: (i, COL_DKV // DKV_PAD)),
            pl.BlockSpec((tt, LANES), lambda i: (i, 0)),
            pl.BlockSpec((tt, LANES), lambda i: (i, 0)),
            pl.BlockSpec((1, MLA_Q_RANK), lambda i: (0, 0)),
            pl.BlockSpec((1, MLA_KV_RANK), lambda i: (0, 0)),
            pl.BlockSpec(wab.shape, lambda i: (0, 0)),
            pl.BlockSpec(wkv.shape, lambda i: (0, 0)),
        ],
        out_specs=[
            pl.BlockSpec((tt, wq), lambda i: (i, 0)),
            pl.BlockSpec((tt, wq), lambda i: (i, 0)),
            pl.BlockSpec((tt, wv), lambda i: (i, 0)),
        ],
        out_shape=[
            jax.ShapeDtypeStruct((t, wq), BF16),
            jax.ShapeDtypeStruct((t, wq), BF16),
            jax.ShapeDtypeStruct((t, wv), BF16),
        ],
        compiler_params=_cparams("parallel"),
        name="mla_prep",
    )(proj, proj, cmap, smap, gq.reshape(1, -1).astype(F32), gkv.reshape(1, -1).astype(F32), wab, wkv)


MLA_BQ = 512
MLA_BK = 512
MLA_HG = 4


def _mla_kernel(q_ref, k_ref, v_ref, o_ref):
    qi = pl.program_id(2)
    bq = q_ref.shape[1]
    bk = min(MLA_BK, bq)
    heads = range(MLA_HG)
    qs = [q_ref[0, :, h * MLA_QK_PAD:(h + 1) * MLA_QK_PAD] for h in heads]
    kpos = lax.broadcasted_iota(jnp.int32, (bk, bq), 0)
    qpos = qi * bq + lax.broadcasted_iota(jnp.int32, (bk, bq), 1)

    def block(kb, ms, ls, accs, masked):
        start = pl.multiple_of(kb * bk, bk)
        ss = [_nt_dot(k_ref[0, pl.ds(start, bk), h * MLA_QK_PAD:(h + 1) * MLA_QK_PAD], qs[h]) for h in heads]
        if masked:
            allowed = (kb * bk + kpos) <= qpos
            ss = [jnp.where(allowed, s, NEG) for s in ss]
        new_ms = [jnp.maximum(ms[h], jnp.max(ss[h], axis=0, keepdims=True)) for h in heads]
        ps = [jnp.exp2(ss[h] - new_ms[h]) for h in heads]
        alphas = [jnp.exp2(ms[h] - new_ms[h]) for h in heads]
        new_ls = [alphas[h] * ls[h] + jnp.sum(ps[h], axis=0, keepdims=True) for h in heads]
        pvs = [lax.dot_general(v_ref[0, pl.ds(start, bk), h * MLA_DV:(h + 1) * MLA_DV], ps[h].astype(BF16),
                               (((0,), (0,)), ((), ())), preferred_element_type=F32) for h in heads]
        new_accs = [alphas[h] * accs[h] + pvs[h] for h in heads]
        return tuple(new_ms), tuple(new_ls), tuple(new_accs)

    ms = tuple(jnp.full((1, bq), NEG, F32) for _ in heads)
    ls = tuple(jnp.zeros((1, bq), F32) for _ in heads)
    accs = tuple(jnp.zeros((MLA_DV, bq), F32) for _ in heads)
    nd = bq // bk
    ms, ls, accs = lax.fori_loop(0, qi * nd, lambda kb, cr: block(kb, cr[0], cr[1], cr[2], False), (ms, ls, accs))
    for d in range(nd):
        ms, ls, accs = block(qi * nd + d, ms, ls, accs, True)
    for h in heads:
        o_ref[0, :, h * MLA_DV:(h + 1) * MLA_DV] = (accs[h] / ls[h]).T.astype(o_ref.dtype)


def _mla_attention(qcat3, kcat3, v3):
    b, s, _ = qcat3.shape
    bq = min(MLA_BQ, s)
    wq = MLA_HG * MLA_QK_PAD
    wv = MLA_HG * MLA_DV
    return pl.pallas_call(
        _mla_kernel,
        grid=(b, MLA_HEADS // MLA_HG, s // bq),
        in_specs=[
            pl.BlockSpec((1, bq, wq), lambda bi, g, qi: (bi, qi, g)),
            pl.BlockSpec((1, s, wq), lambda bi, g, qi: (bi, 0, g), pipeline_mode=pl.Buffered(1)),
            pl.BlockSpec((1, s, wv), lambda bi, g, qi: (bi, 0, g), pipeline_mode=pl.Buffered(1)),
        ],
        out_specs=pl.BlockSpec((1, bq, wv), lambda bi, g, qi: (bi, qi, g)),
        out_shape=jax.ShapeDtypeStruct((b, s, MLA_HEADS * MLA_DV), BF16),
        compiler_params=_cparams("parallel", "parallel", "arbitrary"),
        name="mla_attention",
    )(qcat3, kcat3, v3)


def _mem_kernel(q_ref, kv_ref, o_ref, *, scale):
    for h in range(MEM_HEADS):
        cols = slice(h * MEM_DH, (h + 1) * MEM_DH)
        s = _nt_dot(q_ref[0, :, cols], kv_ref[0, :, cols]) * scale
        m = jnp.max(s, axis=-1, keepdims=True)
        p = jnp.exp(s - m)
        p = p / jnp.sum(p, axis=-1, keepdims=True)
        vals = kv_ref[0, :, MEM_W + h * MEM_DH:MEM_W + (h + 1) * MEM_DH]
        o_ref[0, :, cols] = jnp.dot(p.astype(BF16), vals, preferred_element_type=F32).astype(o_ref.dtype)


def _mem_attention(proj3, mkv3, tt=512):
    b, s, _ = proj3.shape
    n_mem = mkv3.shape[1]
    tt = min(tt, s)
    return pl.pallas_call(
        functools.partial(_mem_kernel, scale=1.0 / math.sqrt(MEM_DH)),
        grid=(b, s // tt),
        in_specs=[
            pl.BlockSpec((1, tt, MEM_W), lambda bi, ti: (bi, ti, COL_MEMQ // MEM_W)),
            pl.BlockSpec((1, n_mem, 2 * MEM_W), lambda bi, ti: (bi, 0, 0)),
        ],
        out_specs=pl.BlockSpec((1, tt, MEM_W), lambda bi, ti: (bi, ti, 0)),
        out_shape=jax.ShapeDtypeStruct((b, s, MEM_W), BF16),
        compiler_params=_cparams("parallel", "parallel"),
        name="mem_attention",
    )(proj3, mkv3)


def _merge_kernel(ya_ref, yb_ref, yc_ref, ga_ref, gb_ref, gc_ref, w_ref, o_ref):
    acc = jax.nn.sigmoid(ga_ref[...].astype(F32)) * jnp.dot(ya_ref[...], w_ref[0], preferred_element_type=F32)
    acc += jax.nn.sigmoid(gb_ref[...].astype(F32)) * jnp.dot(yb_ref[...], w_ref[1], preferred_element_type=F32)
    acc += jax.nn.sigmoid(gc_ref[...].astype(F32)) * jnp.dot(yc_ref[...], w_ref[2], preferred_element_type=F32)
    o_ref[...] = acc.astype(o_ref.dtype)


def _merge(y_sb, y_mla, y_mem, proj, wb, d, tm=1024, tn=512):
    t = y_sb.shape[0]
    tm, tn = min(tm, t), min(tn, d)
    g0 = COL_GATE // tn
    gstep = d // tn

    def gate_spec(br):
        return pl.BlockSpec((tm, tn), lambda j, i: (i, g0 + br * gstep + j))

    yspec = pl.BlockSpec((tm, BRANCH_W), lambda j, i: (i, 0))
    return pl.pallas_call(
        _merge_kernel,
        grid=(d // tn, t // tm),
        in_specs=[yspec, yspec, yspec, gate_spec(0), gate_spec(1), gate_spec(2),
                  pl.BlockSpec((N_BRANCH, BRANCH_W, tn), lambda j, i: (0, 0, j))],
        out_specs=pl.BlockSpec((tm, tn), lambda j, i: (i, j)),
        out_shape=jax.ShapeDtypeStruct((t, d), BF16),
        compiler_params=_cparams("parallel", "parallel"),
        name="branch_merge",
    )(y_sb, y_mla, y_mem, proj, proj, proj, wb)


PEER_EB = 512
PEER_GROUP = PEER_EB // PEER_NKEYS
PEER_ROUTE_UNROLL = 4
PEER_RANK_NONE = 255.0
PEER_CAND_WIDTH = [PEER_TOPK // (a + 1) for a in range(PEER_TOPK)]


def _take_top(vals, n, with_rank=False):
    out = []
    rem = vals
    rank = jnp.full(vals.shape, PEER_RANK_NONE, F32) if with_rank else None
    for k in range(n):
        mx = jnp.max(rem, axis=0, keepdims=True)
        out.append(mx)
        hit = rem == mx
        if with_rank:
            rank = jnp.where(hit, float(k), rank)
        rem = jnp.where(hit, -jnp.inf, rem)
    return (out, rank) if with_rank else out


def _peer_route_kernel(q_ref, keys_ref, nw_ref, r1_ref, e1_ref):
    def head(h):
        s0 = _nt_dot(keys_ref[h, 0], q_ref[2 * h].astype(BF16))
        s1 = _nt_dot(keys_ref[h, 1], q_ref[2 * h + 1].astype(BF16))
        top0 = _take_top(s0, PEER_TOPK)
        top1, rank1 = _take_top(s1, PEER_TOPK, with_rank=True)
        cand = [[top0[a] + top1[b] for b in range(PEER_CAND_WIDTH[a])] for a in range(PEER_TOPK)]
        flat = [c for row in cand for c in row]
        pad = -len(flat) % 8
        stacked = jnp.concatenate(flat + [jnp.full_like(flat[0], -jnp.inf)] * pad, axis=0)
        ctop = _take_top(stacked, PEER_TOPK)
        cmax, tau = ctop[0], ctop[PEER_TOPK - 1]
        zsum = jnp.zeros_like(cmax)
        for cv in ctop:
            zsum = zsum + jnp.exp(cv - cmax)
        n0 = jnp.zeros_like(s0)
        for a in range(PEER_TOPK):
            count = jnp.zeros_like(tau)
            for c in cand[a]:
                count = count + jnp.where(c >= tau, 1.0, 0.0)
            n0 = jnp.where(s0 == top0[a], count, n0)
        w0 = jnp.exp(s0 - top0[0]) / zsum
        for g in range(PEER_NKEYS // PEER_GROUP):
            keys = slice(g * PEER_GROUP, (g + 1) * PEER_GROUP)
            nw_ref[g, h, :PEER_GROUP, :] = n0[keys]
            nw_ref[g, h, PEER_GROUP:, :] = w0[keys]
        r1_ref[h] = pltpu.bitcast(rank1.astype(BF16), jnp.uint32)
        e1_ref[h] = pltpu.bitcast(jnp.exp(s1 - top1[0]).astype(BF16), jnp.uint32)

    def head_group(hg, carry):
        for k in range(PEER_ROUTE_UNROLL):
            head(PEER_ROUTE_UNROLL * hg + k)
        return carry

    lax.fori_loop(0, PEER_HEADS // PEER_ROUTE_UNROLL, head_group, 0)


def _peer_route(q3, keys, tt=256):
    _, t, dh = q3.shape
    tt = min(tt, t)
    ngrp = PEER_NKEYS // PEER_GROUP
    nwspec = pl.BlockSpec((ngrp, PEER_HEADS, 2 * PEER_GROUP, tt), lambda i: (0, 0, 0, i))
    pspec = pl.BlockSpec((PEER_HEADS, PEER_NKEYS // 2, tt), lambda i: (0, 0, i))
    packed = jax.ShapeDtypeStruct((PEER_HEADS, PEER_NKEYS // 2, t), jnp.uint32)
    return pl.pallas_call(
        _peer_route_kernel,
        grid=(t // tt,),
        in_specs=[pl.BlockSpec((2 * PEER_HEADS, tt, dh), lambda i: (0, i, 0)),
                  pl.BlockSpec(keys.shape, lambda i: (0, 0, 0, 0))],
        out_specs=[nwspec, pspec, pspec],
        out_shape=[jax.ShapeDtypeStruct((ngrp, PEER_HEADS, 2 * PEER_GROUP, t), F32), packed, packed],
        compiler_params=_cparams("parallel"),
        name="peer_route",
    )(q3, keys)


PEER_ROWS = 16
PEER_OUT_ROWS = 128


def _peer_dense_kernel(xn_ref, u_ref, v_ref, nw_ref, r1_ref, e1_ref, h_hbm, g_ref, o_ref, at_ref, h_ref, h_sem,
                       *, normalize):
    e = pl.program_id(1)
    tt = o_ref.shape[0]
    rows0 = pl.multiple_of(pl.program_id(0) * tt, tt)
    h_copy = pltpu.make_async_copy(h_hbm.at[pl.ds(rows0, tt), :], h_ref, h_sem)

    @pl.when(e == 0)
    def _():
        h_copy.start()
        o_ref[...] = jnp.zeros_like(o_ref)

    pre = _nt_dot(u_ref[...], xn_ref[...])
    at_ref[...] = (0.5 * pre * (1.0 + lax.erf(pre * (1.0 / math.sqrt(2.0))))).astype(BF16)
    tt = at_ref.shape[1]
    zero = jnp.zeros((PEER_ROWS, LANES), BF16)
    for ii in range(PEER_GROUP):
        for l0 in range(0, tt, LANES):
            lanes = slice(l0, l0 + LANES)
            counts = [jnp.broadcast_to(nw_ref[0, h, ii:ii + 1, lanes], (PEER_ROWS, LANES)).astype(BF16)
                      for h in range(PEER_HEADS)]
            weights = [jnp.broadcast_to(nw_ref[0, h, PEER_GROUP + ii:PEER_GROUP + ii + 1, lanes],
                                        (PEER_ROWS, LANES)).astype(BF16) for h in range(PEER_HEADS)]
            for r0 in range(0, PEER_NKEYS, PEER_ROWS):
                words = slice(r0 // 2, (r0 + PEER_ROWS) // 2)
                gate = zero
                for h in range(PEER_HEADS):
                    rank = pltpu.bitcast(r1_ref[h, words, lanes], BF16)
                    e1 = pltpu.bitcast(e1_ref[h, words, lanes], BF16)
                    gate = gate + jnp.where(rank < counts[h], e1 * weights[h], zero)
                e0 = ii * PEER_NKEYS + r0
                at_ref[e0:e0 + PEER_ROWS, lanes] = at_ref[e0:e0 + PEER_ROWS, lanes] * gate
    o_ref[...] += lax.dot_general(at_ref[...], v_ref[...], (((0,), (0,)), ((), ())), preferred_element_type=F32)

    @pl.when(e == pl.num_programs(1) - 1)
    def _():
        h_copy.wait()

        def strip(r, carry):
            rows = pl.ds(pl.multiple_of(r * PEER_OUT_ROWS, PEER_OUT_ROWS), PEER_OUT_ROWS)
            x = h_ref[rows, :] + o_ref[rows, :]
            if normalize:
                x = x * lax.rsqrt(jnp.mean(x * x, axis=-1, keepdims=True) + EPS) * g_ref[...]
            o_ref[rows, :] = x
            return carry

        lax.fori_loop(0, o_ref.shape[0] // PEER_OUT_ROWS, strip, 0)


def _peer_dense(xn, u, v, nw, r1, e1, h1, gain, tt=512):
    t, d = xn.shape
    n = u.shape[0]
    tt = min(tt, t)
    hspec = pl.BlockSpec((PEER_HEADS, PEER_NKEYS // 2, tt), lambda ti, e: (0, 0, ti))
    normalize = gain is not None
    gain = jnp.ones((d,), F32) if gain is None else gain
    return pl.pallas_call(
        functools.partial(_peer_dense_kernel, normalize=normalize),
        grid=(t // tt, n // PEER_EB),
        in_specs=[
            pl.BlockSpec((tt, d), lambda ti, e: (ti, 0)),
            pl.BlockSpec((PEER_EB, d), lambda ti, e: (e, 0)),
            pl.BlockSpec((PEER_EB, d), lambda ti, e: (e, 0)),
            pl.BlockSpec((1, PEER_HEADS, 2 * PEER_GROUP, tt), lambda ti, e: (e, 0, 0, ti)),
            hspec,
            hspec,
            pl.BlockSpec(memory_space=pl.ANY),
            pl.BlockSpec((1, d), lambda ti, e: (0, 0)),
        ],
        out_specs=pl.BlockSpec((tt, d), lambda ti, e: (ti, 0)),
        out_shape=jax.ShapeDtypeStruct((t, d), F32),
        scratch_shapes=[pltpu.VMEM((PEER_EB, tt), BF16), pltpu.VMEM((tt, d), F32), pltpu.SemaphoreType.DMA(())],
        compiler_params=_cparams("arbitrary", "arbitrary"),
        name="peer_dense",
    )(xn, u, v, nw, r1, e1, h1, gain.reshape(1, d).astype(F32))


IN_TN = 512
IN_ROW_ALIGN = 64


def _in_proj_kernel(a_ref, w_ref, o_ref, wb_ref, *, kr_block):
    j = pl.program_id(0)
    i = pl.program_id(1)
    half = MLA_ROPE // 2
    kr0 = (COL_DKV + MLA_KV_RANK) % IN_TN

    @pl.when(jnp.logical_and(i == 0, j != kr_block))
    def _():
        wb_ref[...] = w_ref[...].astype(BF16)

    @pl.when(jnp.logical_and(i == 0, j == kr_block))
    def _():
        wb_ref[...] = jnp.zeros_like(wb_ref)
        wb_ref[:kr0 + MLA_ROPE, :] = w_ref[:kr0 + MLA_ROPE, :].astype(BF16)
        wb_ref[kr0 + LANES:kr0 + LANES + half, :] = w_ref[kr0 + half:kr0 + MLA_ROPE, :].astype(BF16)
        wb_ref[kr0 + LANES + half:kr0 + LANES + MLA_ROPE, :] = w_ref[kr0:kr0 + half, :].astype(BF16)

    o_ref[...] = _nt_dot(a_ref[...], wb_ref[...]).astype(o_ref.dtype)


def _in_proj(xn, w_in_t, d, tm=1024):
    t = xn.shape[0]
    c3 = 3 * SB_W + MLA_Q_RANK + MLA_KV_RANK + MLA_ROPE
    n_cols = COL_GATE + N_BRANCH * d
    kr_block = (COL_DKV + MLA_KV_RANK) // IN_TN
    assert (kr_block + 1) * IN_TN == COL_MEMQ and (COL_DKV + MLA_KV_RANK) % IN_TN + 2 * LANES <= IN_TN
    assert (n_cols - COL_MEMQ) % IN_TN == 0 and w_in_t.shape[0] - c3 == n_cols - COL_MEMQ and c3 % IN_ROW_ALIGN == 0
    tm = min(tm, t)

    def row0(j):
        return pl.multiple_of(jnp.where(j <= kr_block, j * IN_TN, c3 + (j - kr_block - 1) * IN_TN), IN_ROW_ALIGN)

    return pl.pallas_call(
        functools.partial(_in_proj_kernel, kr_block=kr_block),
        grid=(n_cols // IN_TN, t // tm),
        in_specs=[pl.BlockSpec((tm, d), lambda j, i: (i, 0)),
                  pl.BlockSpec((pl.Element(IN_TN), pl.Element(d)), lambda j, i: (row0(j), 0))],
        out_specs=pl.BlockSpec((tm, IN_TN), lambda j, i: (i, j)),
        out_shape=jax.ShapeDtypeStruct((t, n_cols), BF16),
        scratch_shapes=[pltpu.VMEM((IN_TN, d), BF16)],
        compiler_params=_cparams("arbitrary", "arbitrary"),
        name="in_proj",
    )(xn, w_in_t)


def _pack_mla_weights(w_uq, w_ukv):
    half = MLA_ROPE // 2
    rq = w_uq.shape[0]
    wq = w_uq.reshape(rq, MLA_HEADS, MLA_NOPE + MLA_ROPE)
    nope, t1, t2 = wq[:, :, :MLA_NOPE], wq[:, :, MLA_NOPE:MLA_NOPE + half], wq[:, :, MLA_NOPE + half:]
    zq = jnp.zeros((rq, MLA_HEADS, LANES - MLA_ROPE), w_uq.dtype)
    w_a = jnp.concatenate([nope, t1, t2, zq], axis=2).reshape(rq, MLA_HEADS * MLA_QK_PAD)
    w_b = jnp.concatenate([t2, t1, zq], axis=2).reshape(rq, MLA_HEADS * LANES)
    wab = jnp.concatenate([w_a, w_b], axis=1).astype(BF16)
    rkv = w_ukv.shape[0]
    wkv = w_ukv.reshape(rkv, MLA_HEADS, MLA_NOPE + MLA_DV)
    wkv = jnp.concatenate([wkv[:, :, :MLA_NOPE].reshape(rkv, -1), wkv[:, :, MLA_NOPE:].reshape(rkv, -1)], axis=1)
    return wab, wkv.astype(BF16)


def _rope_maps(positions):
    half = MLA_ROPE // 2
    freqs = ROPE_BASE ** (-jnp.arange(half, dtype=F32) / half)
    ang = positions.astype(F32).reshape(-1)[:, None] * freqs
    cos, sin = jnp.cos(ang), jnp.sin(ang)
    z = jnp.zeros((ang.shape[0], LANES - MLA_ROPE), F32)
    return jnp.concatenate([cos, cos, z], axis=1), jnp.concatenate([-sin, sin, z], axis=1)


def _layer(h, mem, cmap, smap, g_mix, w_in, mla_g_q, mla_w_uq, mla_g_kv, mla_w_ukv, g_mem, w_mem_kv,
           w_branch, w_out, g_ffn, peer_w_q, peer_sub_keys, peer_u, peer_v, g_out):
    b, s, d = h.shape
    t = b * s
    n_mem = mem.shape[1]
    h2 = h.reshape(t, d)

    xn = _rmsnorm(h2, g_mix, BF16)
    proj = _in_proj(xn, w_in.T, d)
    proj3 = proj.reshape(b, s, -1)

    y_sb = _sb_attention(proj3).reshape(t, SB_W)

    wab, wkv = _pack_mla_weights(mla_w_uq, mla_w_ukv)
    qcat, kcat, v = _mla_prep(proj, cmap, smap, mla_g_q, mla_g_kv, wab, wkv)
    y_mla = _mla_attention(qcat.reshape(b, s, -1), kcat.reshape(b, s, -1), v.reshape(b, s, -1)).reshape(t, -1)

    mem_n = _rmsnorm(mem.reshape(b * n_mem, d), g_mem, BF16)
    mkv = _matmul(mem_n, w_mem_kv.astype(BF16), BF16, tm=512, tn=1024, name="mem_kv")
    y_mem = _mem_attention(proj3, mkv.reshape(b, n_mem, 2 * MEM_W)).reshape(t, MEM_W)

    merged = _merge(y_sb, y_mla, y_mem, proj, w_branch.astype(BF16), d)
    h1 = _matmul(merged, w_out.astype(BF16), F32, tm=512, tn=1024, residual=h2, name="out_proj")

    xn2 = _rmsnorm(h1, g_ffn, BF16)
    assert PEER_DK // 2 == LANES and peer_w_q.shape[1] == PEER_HEADS * PEER_DK
    q3 = _matmul_split(xn2, peer_w_q.astype(BF16), F32, tm=512, tn=1024, name="peer_q")
    nw, r1, e1 = _peer_route(q3, peer_sub_keys.astype(BF16))
    return _peer_dense(xn2, peer_u.astype(BF16), peer_v.astype(BF16), nw, r1, e1, h1, g_out)


def kernel(x, mem, positions, g_mix, w_in, mla_g_q, mla_w_uq, mla_g_kv, mla_w_ukv, g_mem, w_mem_kv, w_branch,
           w_out, g_ffn, peer_w_q, peer_sub_keys, peer_u, peer_v, g_final):
    b, s, d = x.shape
    cmap, smap = _rope_maps(positions)
    depth = w_in.shape[0]
    h = x
    for layer in range(depth):
        g_out = g_final if layer + 1 == depth else None
        h = _layer(h, mem, cmap, smap, g_mix[layer], w_in[layer], mla_g_q[layer], mla_w_uq[layer],
                   mla_g_kv[layer], mla_w_ukv[layer], g_mem[layer], w_mem_kv[layer], w_branch[layer],
                   w_out[layer], g_ffn[layer], peer_w_q[layer], peer_sub_keys[layer], peer_u[layer],
                   peer_v[layer], g_out).reshape(b, s, d)
    return h
```
